```python
import jax, jax.numpy as jnp
from jax import lax
import numpy as np

D_MODEL = 1024
BATCH = 8
SEQ = 4096
DEPTH = 4

N_META = 16
D_MIX = D_MODEL
LRU_WIDTH = D_MIX // 2
LRU_HEADS = 8
LRU_HEAD_DIM = LRU_WIDTH // LRU_HEADS
CONV_WIDTH = 4
LRU_C = 8.0
MLA_HEADS = 8
QK_NOPE = 64
QK_ROPE = 32
V_DIM = (D_MIX - LRU_WIDTH) // MLA_HEADS
Q_LORA = 3 * D_MODEL // 8
KV_LORA = D_MODEL // 4
IN_COLS = 2 * LRU_WIDTH + Q_LORA + KV_LORA + QK_ROPE
D_FF = 11 * D_MODEL // 4
ROPE_THETA = 10000.0
Q_BLOCK = 128
EPS = 1e-6

kernel_name = 'hymba_rglru_mla_macaron_sandwich'


def rms_norm(x, g):
    xf = x.astype(jnp.float32)
    y = xf * lax.rsqrt(jnp.mean(xf * xf, axis=-1, keepdims=True) + EPS)
    return (y * g.astype(jnp.float32)).astype(x.dtype)


def swiglu(u, w_gate, w_up, w_down):
    return (jax.nn.silu(u @ w_gate) * (u @ w_up)) @ w_down


def rope_tables(T):
    pos = jnp.arange(T, dtype=jnp.float32)
    inv_freq = 1.0 / (ROPE_THETA ** (jnp.arange(0, QK_ROPE, 2, dtype=jnp.float32) / QK_ROPE))
    ang = pos[:, None] * inv_freq[None, :]
    return jnp.cos(ang), jnp.sin(ang)


def apply_rope(x, cos, sin):
    x1, x2 = jnp.split(x, 2, axis=-1)
    cos = cos.astype(x.dtype)
    sin = sin.astype(x.dtype)
    return jnp.concatenate([x1 * cos - x2 * sin, x2 * cos + x1 * sin], axis=-1)


def _lru_combine(left, right):
    a_l, b_l = left
    a_r, b_r = right
    return a_l * a_r, a_r * b_l + b_r


def rglru_group(xr, gr, conv_w, conv_b, w_a, b_a, w_x, b_x, lam):
    B, T, W = xr.shape
    xp = jnp.pad(xr, ((0, 0), (CONV_WIDTH - 1, 0), (0, 0)))
    xc = xp[:, CONV_WIDTH - 1:] * conv_w[CONV_WIDTH - 1] + conv_b
    for k in range(CONV_WIDTH - 1):
        xc = xc + xp[:, k:k + T] * conv_w[k]
    xh = xc.reshape(B, T, LRU_HEADS, LRU_HEAD_DIM)
    r = jax.nn.sigmoid(jnp.einsum('bthi,hij->bthj', xh, w_a).reshape(B, T, W) + b_a)
    i = jax.nn.sigmoid(jnp.einsum('bthi,hij->bthj', xh, w_x).reshape(B, T, W) + b_x)
    log_a = -LRU_C * r.astype(jnp.float32) * jax.nn.softplus(-lam.astype(jnp.float32))
    a = jnp.exp(log_a)
    b = jnp.sqrt(-jnp.expm1(2.0 * log_a)) * (i * xc).astype(jnp.float32)
    _, h = lax.associative_scan(_lru_combine, (a, b), axis=1)
    return h.astype(xr.dtype) * jax.nn.gelu(gr)


def causal_block_attention(q, k, v):
    B, T, H, Dqk = q.shape
    n_blk = -(-T // Q_BLOCK)
    pad = n_blk * Q_BLOCK - T
    qb = jnp.pad(q, ((0, 0), (0, pad), (0, 0), (0, 0)))
    qb = qb.reshape(B, n_blk, Q_BLOCK, H, Dqk).transpose(1, 0, 2, 3, 4)
    scale = Dqk ** -0.5
    kpos = jnp.arange(T)

    def one_block(args):
        q_blk, blk = args
        s = jnp.einsum('bqhd,bkhd->bhqk', q_blk, k).astype(jnp.float32) * scale
        qpos = blk * Q_BLOCK + jnp.arange(Q_BLOCK)
        s = jnp.where(kpos[None, :] <= qpos[:, None], s, -jnp.inf)
        p = jax.nn.softmax(s, axis=-1).astype(v.dtype)
        return jnp.einsum('bhqk,bkhd->bqhd', p, v)

    o = lax.map(one_block, (qb, jnp.arange(n_blk)))
    o = o.transpose(1, 0, 2, 3, 4).reshape(B, n_blk * Q_BLOCK, H, v.shape[-1])
    return o[:, :T]


def mla_group(cq, ckv, kr, cos, sin, q_norm_g, w_uq, kv_norm_g, w_ukv):
    B, T, _ = cq.shape
    q = (rms_norm(cq, q_norm_g) @ w_uq).reshape(B, T, MLA_HEADS, QK_NOPE + QK_ROPE)
    q_nope, q_rope = jnp.split(q, [QK_NOPE], axis=-1)
    q_rope = apply_rope(q_rope, cos[None, :, None, :], sin[None, :, None, :])
    kv = (rms_norm(ckv, kv_norm_g) @ w_ukv).reshape(B, T, MLA_HEADS, QK_NOPE + V_DIM)
    k_nope, v = jnp.split(kv, [QK_NOPE], axis=-1)
    k_rope = apply_rope(kr, cos[None], sin[None])
    k_rope = jnp.broadcast_to(k_rope[:, :, None, :], (B, T, MLA_HEADS, QK_ROPE))
    qf = jnp.concatenate([q_nope, q_rope], axis=-1)
    kf = jnp.concatenate([k_nope, k_rope], axis=-1)
    return causal_block_attention(qf, kf, v).reshape(B, T, MLA_HEADS * V_DIM)


def setup_inputs(seed: int = 0) -> dict:
    key = jax.random.key(seed)
    ks = jax.random.split(key, 32)

    def nrm(k, shape, scale):
        return jax.random.normal(k, shape, jnp.float32) * scale

    def gain(k, n):
        return 1.0 + 0.02 * jax.random.normal(k, (DEPTH, n), jnp.float32)

    a0 = jax.random.uniform(ks[15], (DEPTH, LRU_WIDTH), jnp.float32, 0.9, 0.999)
    lam = jnp.log(a0) - jnp.log1p(-a0)
    return {
        'x': nrm(ks[0], (BATCH, SEQ, D_MODEL), 1.0),
        'meta_tokens': nrm(ks[1], (N_META, D_MODEL), 1.0),
        'ffn1_pre_g': gain(ks[2], D_MODEL),
        'ffn1_w_gate': nrm(ks[3], (DEPTH, D_MODEL, D_FF), D_MODEL ** -0.5),
        'ffn1_w_up': nrm(ks[4], (DEPTH, D_MODEL, D_FF), D_MODEL ** -0.5),
        'ffn1_w_down': nrm(ks[5], (DEPTH, D_FF, D_MODEL), D_FF ** -0.5),
        'ffn1_post_g': gain(ks[6], D_MODEL),
        'mix_pre_g': gain(ks[7], D_MODEL),
        'w_in': nrm(ks[8], (DEPTH, D_MODEL, IN_COLS), D_MODEL ** -0.5),
        'lru_conv_w': nrm(ks[9], (DEPTH, CONV_WIDTH, LRU_WIDTH), CONV_WIDTH ** -0.5),
        'lru_conv_b': nrm(ks[10], (DEPTH, LRU_WIDTH), 0.01),
        'lru_w_a': nrm(ks[11], (DEPTH, LRU_HEADS, LRU_HEAD_DIM, LRU_HEAD_DIM), LRU_HEAD_DIM ** -0.5),
        'lru_b_a': nrm(ks[12], (DEPTH, LRU_WIDTH), 0.01),
        'lru_w_x': nrm(ks[13], (DEPTH, LRU_HEADS, LRU_HEAD_DIM, LRU_HEAD_DIM), LRU_HEAD_DIM ** -0.5),
        'lru_b_x': nrm(ks[14], (DEPTH, LRU_WIDTH), 0.01),
        'lru_lambda': lam,
        'mla_q_norm_g': gain(ks[16], Q_LORA),
        'mla_w_uq': nrm(ks[17], (DEPTH, Q_LORA, MLA_HEADS * (QK_NOPE + QK_ROPE)), Q_LORA ** -0.5),
        'mla_kv_norm_g': gain(ks[18], KV_LORA),
        'mla_w_ukv': nrm(ks[19], (DEPTH, KV_LORA, MLA_HEADS * (QK_NOPE + V_DIM)), KV_LORA ** -0.5),
        'lru_out_g': gain(ks[20], LRU_WIDTH),
        'mla_out_g': gain(ks[21], MLA_HEADS * V_DIM),
        'w_out': nrm(ks[22], (DEPTH, D_MIX, D_MODEL), D_MIX ** -0.5),
        'mix_post_g': gain(ks[23], D_MODEL),
        'ffn2_pre_g': gain(ks[24], D_MODEL),
        'ffn2_w_gate': nrm(ks[25], (DEPTH, D_MODEL, D_FF), D_MODEL ** -0.5),
        'ffn2_w_up': nrm(ks[26], (DEPTH, D_MODEL, D_FF), D_MODEL ** -0.5),
        'ffn2_w_down': nrm(ks[27], (DEPTH, D_FF, D_MODEL), D_FF ** -0.5),
        'ffn2_post_g': gain(ks[28], D_MODEL),
    }


def reference(x, meta_tokens, ffn1_pre_g, ffn1_w_gate, ffn1_w_up, ffn1_w_down, ffn1_post_g,
              mix_pre_g, w_in, lru_conv_w, lru_conv_b, lru_w_a, lru_b_a, lru_w_x, lru_b_x,
              lru_lambda, mla_q_norm_g, mla_w_uq, mla_kv_norm_g, mla_w_ukv, lru_out_g,
              mla_out_g, w_out, mix_post_g, ffn2_pre_g, ffn2_w_gate, ffn2_w_up, ffn2_w_down,
              ffn2_post_g):
    B = x.shape[0]
    meta = jnp.broadcast_to(meta_tokens.astype(x.dtype)[None], (B, N_META, D_MODEL))
    h = jnp.concatenate([meta, x], axis=1)
    T = h.shape[1]
    cos, sin = rope_tables(T)
    splits = [LRU_WIDTH, 2 * LRU_WIDTH, 2 * LRU_WIDTH + Q_LORA, 2 * LRU_WIDTH + Q_LORA + KV_LORA]
    for l in range(DEPTH):
        f = swiglu(rms_norm(h, ffn1_pre_g[l]), ffn1_w_gate[l], ffn1_w_up[l], ffn1_w_down[l])
        h = h + 0.5 * rms_norm(f, ffn1_post_g[l])
        z = rms_norm(h, mix_pre_g[l]) @ w_in[l]
        xr, gr, cq, ckv, kr = jnp.split(z, splits, axis=-1)
        y_lru = rglru_group(xr, gr, lru_conv_w[l], lru_conv_b[l], lru_w_a[l], lru_b_a[l],
                            lru_w_x[l], lru_b_x[l], lru_lambda[l])
        y_mla = mla_group(cq, ckv, kr, cos, sin, mla_q_norm_g[l], mla_w_uq[l],
                          mla_kv_norm_g[l], mla_w_ukv[l])
        y = jnp.concatenate([rms_norm(y_lru, lru_out_g[l]), rms_norm(y_mla, mla_out_g[l])],
                            axis=-1) @ w_out[l]
        h = h + rms_norm(y, mix_post_g[l])
        f = swiglu(rms_norm(h, ffn2_pre_g[l]), ffn2_w_gate[l], ffn2_w_up[l], ffn2_w_down[l])
        h = h + 0.5 * rms_norm(f, ffn2_post_g[l])
    return h[:, N_META:]
```

```python
import functools

import jax
import jax.numpy as jnp
from jax import lax
from jax.experimental import pallas as pl
from jax.experimental.pallas import tpu as pltpu

D_MODEL = 1024
N_META = 16
LRU_WIDTH = 512
LRU_HEADS = 8
LRU_HEAD_DIM = 64
CONV_WIDTH = 4
LRU_C = 8.0
MLA_HEADS = 8
QK_NOPE = 64
QK_ROPE = 32
V_DIM = 64
Q_LORA = 384
KV_LORA = 256
D_FF = 2816
ROPE_THETA = 10000.0
EPS = 1e-6

LANES = 128
SUBLANES = 8
HEAD_PAD = LANES
QK_WIDTH = MLA_HEADS * HEAD_PAD
IN_COLS_PAD = 2 * LRU_WIDTH + Q_LORA + KV_LORA + HEAD_PAD
SEQ_TILE = 384
FFN_ROWS = 512
FF_CHUNK = 256
NEG_BIG = -1e30
VMEM_LIMIT = 52 * 1024 * 1024

F32 = jnp.float32
BF16 = jnp.bfloat16


def _rms(x, g):
    return x * lax.rsqrt(jnp.mean(x * x, axis=-1, keepdims=True) + EPS) * g


def _const_spec(shape, layer=None):
    if layer is None:
        idx = lambda *_: (0,) * len(shape)
        return pl.BlockSpec(shape, idx, pipeline_mode=pl.Buffered(1))
    idx = lambda *_: (layer,) + (0,) * len(shape)
    return pl.BlockSpec((None,) + shape, idx, pipeline_mode=pl.Buffered(1))


def _ffn_kernel(h_ref, gpre_ref, wg_ref, wu_ref, wd_ref, gpost_ref, o_ref, a_ref):
    x = h_ref[...]
    u = _rms(x, gpre_ref[...]).astype(BF16)
    for c in range(D_FF // FF_CHUNK):
        sl = slice(c * FF_CHUNK, (c + 1) * FF_CHUNK)
        gate = jnp.dot(u, wg_ref[:, sl], preferred_element_type=F32)
        up = jnp.dot(u, wu_ref[:, sl], preferred_element_type=F32)
        a_ref[:, sl] = (gate * jax.nn.sigmoid(gate) * up).astype(BF16)
    f = jnp.dot(a_ref[...], wd_ref[...], preferred_element_type=F32)
    o_ref[...] = x + 0.5 * _rms(f, gpost_ref[...])


def _ffn(h2d, layer, gpre, wg, wu, wd, gpost):
    rows = h2d.shape[0]
    row_spec = pl.BlockSpec((FFN_ROWS, D_MODEL), lambda i: (i, 0))
    return pl.pallas_call(
        _ffn_kernel,
        grid=(rows // FFN_ROWS,),
        in_specs=[
            row_spec,
            _const_spec((1, D_MODEL), layer),
            _const_spec((D_MODEL, D_FF), layer),
            _const_spec((D_MODEL, D_FF), layer),
            _const_spec((D_FF, D_MODEL), layer),
            _const_spec((1, D_MODEL), layer),
        ],
        out_specs=row_spec,
        out_shape=jax.ShapeDtypeStruct(h2d.shape, F32),
        scratch_shapes=[pltpu.VMEM((FFN_ROWS, D_FF), BF16)],
        compiler_params=pltpu.CompilerParams(
            dimension_semantics=("arbitrary",), vmem_limit_bytes=VMEM_LIMIT),
        name="ffn",
    )(h2d, gpre, wg, wu, wd, gpost)


def _rope(x, c, sa, sb):
    n = x.shape[-1]
    half = QK_ROPE // 2
    return x * c + pltpu.roll(x, half, 1) * sa + pltpu.roll(x, n - half, 1) * sb


def _mix_in_kernel(h_ref, g_ref, win_ref, cw_ref, cb_ref, wax_ref, ba_ref, bx_ref, lam_ref,
                   gq_ref, wuq_ref, gkv_ref, wuk_ref, wuv_ref, glru_ref,
                   rc_ref, rsa_ref, rsb_ref,
                   ylru_ref, qt_ref, k_ref, vt_ref,
                   xprev_ref, hcar_ref):
    tt = h_ref.shape[0]

    @pl.when(pl.program_id(1) == 0)
    def _():
        xprev_ref[...] = jnp.zeros_like(xprev_ref)
        hcar_ref[...] = jnp.zeros_like(hcar_ref)

    u = _rms(h_ref[...], g_ref[...]).astype(BF16)
    z = jnp.dot(u, win_ref[...], preferred_element_type=F32)
    xr = z[:, 0:LRU_WIDTH]
    gr = z[:, LRU_WIDTH:2 * LRU_WIDTH]
    cq = z[:, 2 * LRU_WIDTH:2 * LRU_WIDTH + Q_LORA]
    ckv = z[:, 2 * LRU_WIDTH + Q_LORA:2 * LRU_WIDTH + Q_LORA + KV_LORA]
    krt = z[:, 2 * LRU_WIDTH + Q_LORA + KV_LORA:]

    xe = jnp.concatenate([xprev_ref[...], xr], axis=0)
    xc = xr * cw_ref[CONV_WIDTH - 1:CONV_WIDTH, :] + cb_ref[...]
    for j in range(1, CONV_WIDTH):
        w_j = cw_ref[CONV_WIDTH - 1 - j:CONV_WIDTH - j, :]
        xc = xc + pltpu.roll(xe, j, 0)[SUBLANES:] * w_j
    xprev_ref[...] = xr[tt - SUBLANES:]

    xcb = xc.astype(BF16)
    half = LRU_WIDTH // 2
    ri0 = jnp.dot(xcb[:, :half], wax_ref[0], preferred_element_type=F32)
    ri1 = jnp.dot(xcb[:, half:], wax_ref[1], preferred_element_type=F32)
    r = jax.nn.sigmoid(jnp.concatenate([ri0[:, :half], ri1[:, :half]], axis=1) + ba_ref[...])
    i = jax.nn.sigmoid(jnp.concatenate([ri0[:, half:], ri1[:, half:]], axis=1) + bx_ref[...])
    nl = -lam_ref[...]
    softplus = jnp.maximum(nl, 0.0) + jnp.log1p(jnp.exp(-jnp.abs(nl)))
    log_a = (-LRU_C) * r * softplus
    a = jnp.exp(log_a)
    b = jnp.sqrt(1.0 - a * a) * (i * xc)

    row = lax.broadcasted_iota(jnp.int32, a.shape, 0) % SUBLANES
    s = 1
    while s < SUBLANES:
        valid = row >= s
        b = jnp.where(valid, a * pltpu.roll(b, s, 0) + b, b)
        a = jnp.where(valid, a * pltpu.roll(a, s, 0), a)
        s *= 2
    carry = hcar_ref[...]
    groups = []
    for gidx in range(tt // SUBLANES):
        sl = slice(gidx * SUBLANES, (gidx + 1) * SUBLANES)
        hg = b[sl] + a[sl] * carry
        carry = hg[SUBLANES - 1:SUBLANES]
        groups.append(hg)
    hcar_ref[...] = carry
    hs = jnp.concatenate(groups, axis=0)

    cg = 0.7978845608028654
    gelu = 0.5 * gr * (1.0 + jnp.tanh(cg * (gr + 0.044715 * (gr * gr * gr))))
    ylru_ref[...] = _rms(hs * gelu, glru_ref[...]).astype(BF16)

    rc = rc_ref[...]
    rsa = rsa_ref[...]
    rsb = rsb_ref[...]
    tile_heads = lambda t: jnp.concatenate([t] * MLA_HEADS, axis=1)
    cqn = _rms(cq, gq_ref[...]).astype(BF16)
    q = jnp.dot(cqn, wuq_ref[...], preferred_element_type=F32)
    q = _rope(q, tile_heads(rc), tile_heads(rsa), tile_heads(rsb))
    q = q * ((QK_NOPE + QK_ROPE) ** -0.5)
    qt_ref[...] = q.T.astype(BF16)

    ckvn = _rms(ckv, gkv_ref[...]).astype(BF16)
    kn = jnp.dot(ckvn, wuk_ref[...], preferred_element_type=F32)
    kr = _rope(krt, rc, rsa, rsb)
    k_ref[...] = (kn + tile_heads(kr)).astype(BF16)
    v = jnp.dot(ckvn, wuv_ref[...], preferred_element_type=F32)
    vt_ref[...] = v.T.astype(BF16)


def _mix_in(h, layer, p, rope):
    bsz, tp, _ = h.shape
    tt = SEQ_TILE
    nt = tp // tt
    tbl_spec = pl.BlockSpec((tt, HEAD_PAD), lambda b, t: (t, 0))
    return pl.pallas_call(
        _mix_in_kernel,
        grid=(bsz, nt),
        in_specs=[
            pl.BlockSpec((None, tt, D_MODEL), lambda b, t: (b, t, 0)),
            _const_spec((1, D_MODEL), layer),
            _const_spec((D_MODEL, IN_COLS_PAD), layer),
            _const_spec((CONV_WIDTH, LRU_WIDTH), layer),
            _const_spec((1, LRU_WIDTH), layer),
            _const_spec((2, LRU_WIDTH // 2, LRU_WIDTH), layer),
            _const_spec((1, LRU_WIDTH), layer),
            _const_spec((1, LRU_WIDTH), layer),
            _const_spec((1, LRU_WIDTH), layer),
            _const_spec((1, Q_LORA), layer),
            _const_spec((Q_LORA, QK_WIDTH), layer),
            _const_spec((1, KV_LORA), layer),
            _const_spec((KV_LORA, QK_WIDTH), layer),
            _const_spec((KV_LORA, MLA_HEADS * V_DIM), layer),
            _const_spec((1, LRU_WIDTH), layer),
            tbl_spec, tbl_spec, tbl_spec,
        ],
        out_specs=[
            pl.BlockSpec((None, tt, LRU_WIDTH), lambda b, t: (b, t, 0)),
            pl.BlockSpec((None, None, QK_WIDTH, tt), lambda b, t: (b, t, 0, 0)),
            pl.BlockSpec((None, None, tt, QK_WIDTH), lambda b, t: (b, t, 0, 0)),
            pl.BlockSpec((None, None, MLA_HEADS * V_DIM, tt), lambda b, t: (b, t, 0, 0)),
        ],
        out_shape=[
            jax.ShapeDtypeStruct((bsz, tp, LRU_WIDTH), BF16),
            jax.ShapeDtypeStruct((bsz, nt, QK_WIDTH, tt), BF16),
            jax.ShapeDtypeStruct((bsz, nt, tt, QK_WIDTH), BF16),
            jax.ShapeDtypeStruct((bsz, nt, MLA_HEADS * V_DIM, tt), BF16),
        ],
        scratch_shapes=[pltpu.VMEM((SUBLANES, LRU_WIDTH), F32),
                        pltpu.VMEM((1, LRU_WIDTH), F32)],
        compiler_params=pltpu.CompilerParams(
            dimension_semantics=("arbitrary", "arbitrary"), vmem_limit_bytes=VMEM_LIMIT),
        name="mix_in",
    )(h, p["mix_pre_g"], p["w_in"], p["conv_w"], p["conv_b"], p["wax"], p["b_a"], p["b_x"],
      p["lam"], p["q_g"], p["w_uq"], p["kv_g"], p["w_uk"], p["w_uv"], p["lru_out_g"], *rope)


def _attn_kernel(qt_ref, k_ref, vt_ref, o_ref):
    qi = pl.program_id(2)
    qt = qt_ref[...]
    tq = qt.shape[1]

    def block(j, carry, masked):
        m, l, acc = carry
        s = jnp.dot(k_ref[j], qt, preferred_element_type=F32)
        if masked:
            key = lax.broadcasted_iota(jnp.int32, s.shape, 0)
            qry = lax.broadcasted_iota(jnp.int32, s.shape, 1)
            s = jnp.where(key <= qry, s, NEG_BIG)
        m_new = jnp.maximum(m, jnp.max(s, axis=0, keepdims=True))
        alpha = jnp.exp(m - m_new)
        p = jnp.exp(s - m_new)
        l = alpha * l + jnp.sum(p, axis=0, keepdims=True)
        acc = alpha * acc + jnp.dot(vt_ref[j], p.astype(BF16), preferred_element_type=F32)
        return m_new, l, acc

    init = (jnp.full((1, tq), NEG_BIG, F32), jnp.zeros((1, tq), F32),
            jnp.zeros((V_DIM, tq), F32))
    carry = lax.fori_loop(0, qi, lambda j, c: block(j, c, False), init)
    _, l, acc = block(qi, carry, True)
    o_ref[...] = acc / l


def _attention(qt, k, vt):
    bsz, nt, _, tt = qt.shape
    return pl.pallas_call(
        _attn_kernel,
        grid=(bsz, MLA_HEADS, nt),
        in_specs=[
            pl.BlockSpec((None, None, HEAD_PAD, tt), lambda b, h, i: (b, i, h, 0)),
            pl.BlockSpec((None, nt, tt, HEAD_PAD), lambda b, h, i: (b, 0, 0, h)),
            pl.BlockSpec((None, nt, V_DIM, tt), lambda b, h, i: (b, 0, h, 0)),
        ],
        out_specs=pl.BlockSpec((None, None, V_DIM, tt), lambda b, h, i: (b, i, h, 0)),
        out_shape=jax.ShapeDtypeStruct((bsz, nt, MLA_HEADS * V_DIM, tt), F32),
        compiler_params=pltpu.CompilerParams(
            dimension_semantics=("arbitrary", "arbitrary", "arbitrary"),
            vmem_limit_bytes=VMEM_LIMIT),
        name="attention",
    )(qt, k, vt)


def _mix_out_kernel(h_ref, ylru_ref, ot_ref, gmla_ref, wout_ref, gpost_ref, o_ref):
    ot = ot_ref[...]
    ms = jnp.mean(ot * ot, axis=0, keepdims=True)
    on = (ot * lax.rsqrt(ms + EPS) * gmla_ref[...]).T.astype(BF16)
    y = jnp.dot(ylru_ref[...], wout_ref[:LRU_WIDTH, :], preferred_element_type=F32)
    y = y + jnp.dot(on, wout_ref[LRU_WIDTH:, :], preferred_element_type=F32)
    o_ref[...] = h_ref[...] + _rms(y, gpost_ref[...])


def _mix_out(h, ylru, ot, layer, p):
    bsz, tp, _ = h.shape
    tt = SEQ_TILE
    row_spec = pl.BlockSpec((None, tt, D_MODEL), lambda b, t: (b, t, 0))
    return pl.pallas_call(
        _mix_out_kernel,
        grid=(bsz, tp // tt),
        in_specs=[
            row_spec,
            pl.BlockSpec((None, tt, LRU_WIDTH), lambda b, t: (b, t, 0)),
            pl.BlockSpec((None, None, MLA_HEADS * V_DIM, tt), lambda b, t: (b, t, 0, 0)),
            _const_spec((MLA_HEADS * V_DIM, 1), layer),
            _const_spec((D_MODEL, D_MODEL), layer),
            _const_spec((1, D_MODEL), layer),
        ],
        out_specs=row_spec,
        out_shape=jax.ShapeDtypeStruct(h.shape, F32),
        compiler_params=pltpu.CompilerParams(
            dimension_semantics=("arbitrary", "arbitrary"), vmem_limit_bytes=VMEM_LIMIT),
        name="mix_out",
    )(h, ylru, ot, p["mla_out_g"], p["w_out"], p["mix_post_g"])


def _rope_tables(tp):
    pos = jnp.arange(tp, dtype=F32)
    inv_freq = 1.0 / (ROPE_THETA ** (jnp.arange(0, QK_ROPE, 2, dtype=F32) / QK_ROPE))
    ang = pos[:, None] * inv_freq[None, :]
    cos, sin = jnp.cos(ang), jnp.sin(ang)
    half = QK_ROPE // 2
    zeros = lambda n: jnp.zeros((tp, n), F32)
    pad = HEAD_PAD - QK_NOPE - QK_ROPE
    c = jnp.concatenate([jnp.ones((tp, QK_NOPE), F32), cos, cos, zeros(pad)], axis=1)
    sa = jnp.concatenate([zeros(QK_NOPE + half), sin, zeros(pad)], axis=1)
    sb = jnp.concatenate([zeros(QK_NOPE), -sin, zeros(half + pad)], axis=1)
    return c, sa, sb


def _block_diag(w):
    depth = w.shape[0]
    per = LRU_HEADS // 2
    w5 = w.reshape(depth, 2, per, LRU_HEAD_DIM, LRU_HEAD_DIM)
    bd = jnp.einsum("dnhij,hg->dnhigj", w5, jnp.eye(per, dtype=w.dtype))
    return bd.reshape(depth, 2, per * LRU_HEAD_DIM, per * LRU_HEAD_DIM)


def _row(g):
    return g.reshape(g.shape[0], 1, g.shape[1])


@jax.jit
def kernel(x, meta_tokens, ffn1_pre_g, ffn1_w_gate, ffn1_w_up, ffn1_w_down, ffn1_post_g, mix_pre_g, w_in, lru_conv_w, lru_conv_b, lru_w_a, lru_b_a, lru_w_x, lru_b_x, lru_lambda, mla_q_norm_g, mla_w_uq, mla_kv_norm_g, mla_w_ukv, lru_out_g, mla_out_g, w_out, mix_post_g, ffn2_pre_g, ffn2_w_gate, ffn2_w_up, ffn2_w_down, ffn2_post_g):
    bsz, seq, _ = x.shape
    depth = w_in.shape[0]
    t_real = N_META + seq
    tp = -(-t_real // SEQ_TILE) * SEQ_TILE
    assert (bsz * tp) % FFN_ROWS == 0

    meta = jnp.broadcast_to(meta_tokens.astype(x.dtype)[None], (bsz, N_META, D_MODEL))
    h = jnp.concatenate([meta, x, jnp.zeros((bsz, tp - t_real, D_MODEL), x.dtype)], axis=1)
    rope = _rope_tables(tp)

    split = 2 * LRU_WIDTH + Q_LORA + KV_LORA
    zcols = lambda n: jnp.zeros((depth, D_MODEL, n), w_in.dtype)
    w_in_p = jnp.concatenate(
        [w_in[..., :split], zcols(QK_NOPE), w_in[..., split:],
         zcols(HEAD_PAD - QK_NOPE - QK_ROPE)], axis=-1).astype(BF16)
    w_uq_p = jnp.pad(mla_w_uq.reshape(depth, Q_LORA, MLA_HEADS, QK_NOPE + QK_ROPE),
                     ((0, 0), (0, 0), (0, 0), (0, HEAD_PAD - QK_NOPE - QK_ROPE)))
    w_ukv4 = mla_w_ukv.reshape(depth, KV_LORA, MLA_HEADS, QK_NOPE + V_DIM)
    w_uk_p = jnp.pad(w_ukv4[..., :QK_NOPE], ((0, 0), (0, 0), (0, 0), (0, HEAD_PAD - QK_NOPE)))
    mixp = {
        "mix_pre_g": _row(mix_pre_g),
        "w_in": w_in_p,
        "conv_w": lru_conv_w,
        "conv_b": _row(lru_conv_b),
        "wax": jnp.concatenate([_block_diag(lru_w_a), _block_diag(lru_w_x)], axis=-1).astype(BF16),
        "b_a": _row(lru_b_a),
        "b_x": _row(lru_b_x),
        "lam": _row(lru_lambda),
        "q_g": _row(mla_q_norm_g),
        "w_uq": w_uq_p.reshape(depth, Q_LORA, QK_WIDTH).astype(BF16),
        "kv_g": _row(mla_kv_norm_g),
        "w_uk": w_uk_p.reshape(depth, KV_LORA, QK_WIDTH).astype(BF16),
        "w_uv": w_ukv4[..., QK_NOPE:].reshape(depth, KV_LORA, MLA_HEADS * V_DIM).astype(BF16),
        "lru_out_g": _row(lru_out_g),
        "mla_out_g": mla_out_g.reshape(depth, MLA_HEADS * V_DIM, 1),
        "w_out": w_out.astype(BF16),
        "mix_post_g": _row(mix_post_g),
    }
    ffn1 = (_row(ffn1_pre_g), ffn1_w_gate.astype(BF16), ffn1_w_up.astype(BF16),
            ffn1_w_down.astype(BF16), _row(ffn1_post_g))
    ffn2 = (_row(ffn2_pre_g), ffn2_w_gate.astype(BF16), ffn2_w_up.astype(BF16),
            ffn2_w_down.astype(BF16), _row(ffn2_post_g))

    for layer in range(depth):
        h = _ffn(h.reshape(bsz * tp, D_MODEL), layer, *ffn1).reshape(bsz, tp, D_MODEL)
        ylru, qt, k, vt = _mix_in(h, layer, mixp, rope)
        ot = _attention(qt, k, vt)
        h = _mix_out(h, ylru, ot, layer, mixp)
        h = _ffn(h.reshape(bsz * tp, D_MODEL), layer, *ffn2).reshape(bsz, tp, D_MODEL)
    return h[:, N_META:t_real]
```

```python
import functools

import jax
import jax.numpy as jnp
from jax import lax
from jax.experimental import pallas as pl
from jax.experimental.pallas import tpu as pltpu

D_MODEL = 1024
N_META = 16
LRU_WIDTH = 512
LRU_HEADS = 8
LRU_HEAD_DIM = 64
CONV_WIDTH = 4
LRU_C = 8.0
MLA_HEADS = 8
QK_NOPE = 64
QK_ROPE = 32
V_DIM = 64
Q_LORA = 384
KV_LORA = 256
D_FF = 2816
ROPE_THETA = 10000.0
EPS = 1e-6

LANES = 128
SUBLANES = 8
HEAD_PAD = LANES
QK_WIDTH = MLA_HEADS * HEAD_PAD
IN_COLS_PAD = 2 * LRU_WIDTH + Q_LORA + KV_LORA + HEAD_PAD
SEQ_TILE = 256
FFN_ROWS = 512
FF_CHUNK = 256
NEG_BIG = -1e30
LOG2_E = 1.4426950408889634
VMEM_LIMIT = 52 * 1024 * 1024

F32 = jnp.float32
BF16 = jnp.bfloat16


def _rms(x, g):
    return x * lax.rsqrt(jnp.mean(x * x, axis=-1, keepdims=True) + EPS) * g


def _const_spec(shape, layer=None):
    if layer is None:
        idx = lambda *_: (0,) * len(shape)
        return pl.BlockSpec(shape, idx, pipeline_mode=pl.Buffered(1))
    idx = lambda *_: (layer,) + (0,) * len(shape)
    return pl.BlockSpec((None,) + shape, idx, pipeline_mode=pl.Buffered(1))


def _ffn_kernel(h_ref, gpre_ref, wg_ref, wu_ref, wd_ref, gpost_ref, o_ref, a_ref):
    x = h_ref[...]
    u = _rms(x, gpre_ref[...]).astype(BF16)
    for c in range(D_FF // FF_CHUNK):
        sl = slice(c * FF_CHUNK, (c + 1) * FF_CHUNK)
        gate = jnp.dot(u, wg_ref[:, sl], preferred_element_type=F32)
        up = jnp.dot(u, wu_ref[:, sl], preferred_element_type=F32)
        a_ref[:, sl] = (gate * jax.nn.sigmoid(gate) * up).astype(BF16)
    f = jnp.dot(a_ref[...], wd_ref[...], preferred_element_type=F32)
    o_ref[...] = x + 0.5 * _rms(f, gpost_ref[...])


def _ffn(h2d, layer, gpre, wg, wu, wd, gpost):
    rows = h2d.shape[0]
    row_spec = pl.BlockSpec((FFN_ROWS, D_MODEL), lambda i: (i, 0))
    return pl.pallas_call(
        _ffn_kernel,
        grid=(rows // FFN_ROWS,),
        in_specs=[
            row_spec,
            _const_spec((1, D_MODEL), layer),
            _const_spec((D_MODEL, D_FF), layer),
            _const_spec((D_MODEL, D_FF), layer),
            _const_spec((D_FF, D_MODEL), layer),
            _const_spec((1, D_MODEL), layer),
        ],
        out_specs=row_spec,
        out_shape=jax.ShapeDtypeStruct(h2d.shape, F32),
        scratch_shapes=[pltpu.VMEM((FFN_ROWS, D_FF), BF16)],
        compiler_params=pltpu.CompilerParams(
            dimension_semantics=("arbitrary",), vmem_limit_bytes=VMEM_LIMIT),
        name="ffn",
    )(h2d, gpre, wg, wu, wd, gpost)


def _rope(x, c, sa, sb):
    n = x.shape[-1]
    half = QK_ROPE // 2
    return x * c + pltpu.roll(x, half, 1) * sa + pltpu.roll(x, n - half, 1) * sb


def _mix_in_kernel(h_ref, g_ref, win_ref, cw_ref, cb_ref, wax_ref, ba_ref, bx_ref, lam_ref,
                   gq_ref, wuq_ref, gkv_ref, wuk_ref, wuv_ref, glru_ref,
                   rc_ref, rsa_ref, rsb_ref,
                   ylru_ref, qt_ref, k_ref, vt_ref,
                   xprev_ref, hcar_ref):
    tt = h_ref.shape[0]

    @pl.when(pl.program_id(1) == 0)
    def _():
        xprev_ref[...] = jnp.zeros_like(xprev_ref)
        hcar_ref[...] = jnp.zeros_like(hcar_ref)

    u = _rms(h_ref[...], g_ref[...]).astype(BF16)
    z = jnp.dot(u, win_ref[...], preferred_element_type=F32)
    xr = z[:, 0:LRU_WIDTH]
    gr = z[:, LRU_WIDTH:2 * LRU_WIDTH]
    cq = z[:, 2 * LRU_WIDTH:2 * LRU_WIDTH + Q_LORA]
    ckv = z[:, 2 * LRU_WIDTH + Q_LORA:2 * LRU_WIDTH + Q_LORA + KV_LORA]
    krt = z[:, 2 * LRU_WIDTH + Q_LORA + KV_LORA:]

    xe = jnp.concatenate([xprev_ref[...], xr], axis=0)
    xc = xr * cw_ref[CONV_WIDTH - 1:CONV_WIDTH, :] + cb_ref[...]
    for j in range(1, CONV_WIDTH):
        w_j = cw_ref[CONV_WIDTH - 1 - j:CONV_WIDTH - j, :]
        xc = xc + pltpu.roll(xe, j, 0)[SUBLANES:] * w_j
    xprev_ref[...] = xr[tt - SUBLANES:]

    xcb = xc.astype(BF16)
    half = LRU_WIDTH // 2
    ri0 = jnp.dot(xcb[:, :half], wax_ref[0], preferred_element_type=F32)
    ri1 = jnp.dot(xcb[:, half:], wax_ref[1], preferred_element_type=F32)
    r = jax.nn.sigmoid(jnp.concatenate([ri0[:, :half], ri1[:, :half]], axis=1) + ba_ref[...])
    i = jax.nn.sigmoid(jnp.concatenate([ri0[:, half:], ri1[:, half:]], axis=1) + bx_ref[...])
    nl = -lam_ref[...]
    softplus = jnp.maximum(nl, 0.0) + jnp.log1p(jnp.exp(-jnp.abs(nl)))
    log_a = (-LRU_C) * r * softplus
    a = jnp.exp(log_a)
    b = jnp.sqrt(1.0 - a * a) * (i * xc)

    row = lax.broadcasted_iota(jnp.int32, a.shape, 0) % SUBLANES
    s = 1
    while s < SUBLANES:
        valid = row >= s
        b = jnp.where(valid, a * pltpu.roll(b, s, 0) + b, b)
        a = jnp.where(valid, a * pltpu.roll(a, s, 0), a)
        s *= 2
    carry = hcar_ref[...]
    groups = []
    for gidx in range(tt // SUBLANES):
        sl = slice(gidx * SUBLANES, (gidx + 1) * SUBLANES)
        hg = b[sl] + a[sl] * carry
        carry = hg[SUBLANES - 1:SUBLANES]
        groups.append(hg)
    hcar_ref[...] = carry
    hs = jnp.concatenate(groups, axis=0)

    cg = 0.7978845608028654
    gelu = 0.5 * gr * (1.0 + jnp.tanh(cg * (gr + 0.044715 * (gr * gr * gr))))
    ylru_ref[...] = _rms(hs * gelu, glru_ref[...]).astype(BF16)

    rc = rc_ref[...]
    rsa = rsa_ref[...]
    rsb = rsb_ref[...]
    tile_heads = lambda t: jnp.concatenate([t] * MLA_HEADS, axis=1)
    cqn = _rms(cq, gq_ref[...]).astype(BF16)
    q = jnp.dot(cqn, wuq_ref[...], preferred_element_type=F32)
    q = _rope(q, tile_heads(rc), tile_heads(rsa), tile_heads(rsb))
    q = q * (LOG2_E * (QK_NOPE + QK_ROPE) ** -0.5)
    qt_ref[...] = q.T.astype(BF16)

    ckvn = _rms(ckv, gkv_ref[...]).astype(BF16)
    kn = jnp.dot(ckvn, wuk_ref[...], preferred_element_type=F32)
    kr = _rope(krt, rc, rsa, rsb)
    k_ref[...] = (kn + tile_heads(kr)).astype(BF16)
    v = jnp.dot(ckvn, wuv_ref[...], preferred_element_type=F32)
    vt_ref[...] = v.T.astype(BF16)


def _mix_in(h, layer, p, rope):
    bsz, tp, _ = h.shape
    tt = SEQ_TILE
    nt = tp // tt
    tbl_spec = pl.BlockSpec((tt, HEAD_PAD), lambda b, t: (t, 0))
    return pl.pallas_call(
        _mix_in_kernel,
        grid=(bsz, nt),
        in_specs=[
            pl.BlockSpec((None, tt, D_MODEL), lambda b, t: (b, t, 0)),
            _const_spec((1, D_MODEL), layer),
            _const_spec((D_MODEL, IN_COLS_PAD), layer),
            _const_spec((CONV_WIDTH, LRU_WIDTH), layer),
            _const_spec((1, LRU_WIDTH), layer),
            _const_spec((2, LRU_WIDTH // 2, LRU_WIDTH), layer),
            _const_spec((1, LRU_WIDTH), layer),
            _const_spec((1, LRU_WIDTH), layer),
            _const_spec((1, LRU_WIDTH), layer),
            _const_spec((1, Q_LORA), layer),
            _const_spec((Q_LORA, QK_WIDTH), layer),
            _const_spec((1, KV_LORA), layer),
            _const_spec((KV_LORA, QK_WIDTH), layer),
            _const_spec((KV_LORA, MLA_HEADS * V_DIM), layer),
            _const_spec((1, LRU_WIDTH), layer),
            tbl_spec, tbl_spec, tbl_spec,
        ],
        out_specs=[
            pl.BlockSpec((None, tt, LRU_WIDTH), lambda b, t: (b, t, 0)),
            pl.BlockSpec((None, QK_WIDTH, tt), lambda b, t: (b, 0, t)),
            pl.BlockSpec((None, tt, QK_WIDTH), lambda b, t: (b, t, 0)),
            pl.BlockSpec((None, MLA_HEADS * V_DIM, tt), lambda b, t: (b, 0, t)),
        ],
        out_shape=[
            jax.ShapeDtypeStruct((bsz, tp, LRU_WIDTH), BF16),
            jax.ShapeDtypeStruct((bsz, QK_WIDTH, tp), BF16),
            jax.ShapeDtypeStruct((bsz, tp, QK_WIDTH), BF16),
            jax.ShapeDtypeStruct((bsz, MLA_HEADS * V_DIM, tp), BF16),
        ],
        scratch_shapes=[pltpu.VMEM((SUBLANES, LRU_WIDTH), F32),
                        pltpu.VMEM((1, LRU_WIDTH), F32)],
        compiler_params=pltpu.CompilerParams(
            dimension_semantics=("arbitrary", "arbitrary"), vmem_limit_bytes=VMEM_LIMIT),
        name="mix_in",
    )(h, p["mix_pre_g"], p["w_in"], p["conv_w"], p["conv_b"], p["wax"], p["b_a"], p["b_x"],
      p["lam"], p["q_g"], p["w_uq"], p["kv_g"], p["w_uk"], p["w_uv"], p["lru_out_g"], *rope)


def _group_max(s):
    m = s[0:SUBLANES]
    for r in range(1, s.shape[0] // SUBLANES):
        m = jnp.maximum(m, s[r * SUBLANES:(r + 1) * SUBLANES])
    return m


def _sublane_allmax(m):
    for shift in (1, 2, 4):
        m = jnp.maximum(m, pltpu.roll(m, shift, 0))
    return m


def _attn_kernel(qt_ref, k_ref, vt_ref, o_ref, m_ref, l_ref, acc_ref, s0_ref, s1_ref,
                 bm0_ref, bm1_ref):
    qi = pl.program_id(1)
    tk = s0_ref.shape[1]
    s_refs = (s0_ref, s1_ref)
    bm_refs = (bm0_ref, bm1_ref)
    m_ref[...] = jnp.full(m_ref.shape, NEG_BIG, F32)
    l_ref[...] = jnp.zeros(l_ref.shape, F32)
    acc_ref[...] = jnp.zeros(acc_ref.shape, F32)

    def scores(j, g, slot):
        off = pl.multiple_of(j * tk, tk)
        s = jnp.dot(k_ref[pl.ds(off, tk), g * HEAD_PAD:(g + 1) * HEAD_PAD],
                    qt_ref[g * HEAD_PAD:(g + 1) * HEAD_PAD, :], preferred_element_type=F32)
        s_refs[slot][g] = s
        bm_refs[slot][g] = _group_max(s)

    def softmax_pv(j, g, slot, masked):
        s = s_refs[slot][g]
        if masked:
            key = lax.broadcasted_iota(jnp.int32, s.shape, 0)
            qry = lax.broadcasted_iota(jnp.int32, s.shape, 1)
            s = jnp.where(key <= qry, s, NEG_BIG)
            bm = _group_max(s)
        else:
            bm = bm_refs[slot][g]
        m_old = m_ref[g]
        m_new = jnp.maximum(m_old, _sublane_allmax(bm))
        alpha = jnp.exp2(m_old - m_new)
        p = jnp.exp2(s - jnp.concatenate([m_new] * (tk // SUBLANES), axis=0))
        m_ref[g] = m_new
        off = pl.multiple_of(j * tk, tk)
        vb = vt_ref[g * V_DIM:(g + 1) * V_DIM, pl.ds(off, tk)]
        vb = jnp.concatenate([vb, jnp.ones((2 * SUBLANES, tk), BF16)], axis=0)
        pv = jnp.dot(vb, p.astype(BF16), preferred_element_type=F32)
        acc_ref[g] = (jnp.concatenate([alpha] * (V_DIM // SUBLANES), axis=0) * acc_ref[g]
                      + pv[:V_DIM])
        l_ref[g] = alpha * l_ref[g] + pv[V_DIM:V_DIM + SUBLANES]

    def step(j_next, slot_next, j, slot, masked=False):
        for g in range(MLA_HEADS):
            if j_next is not None:
                scores(j_next, g, slot_next)
            softmax_pv(j, g, slot, masked)

    for g in range(MLA_HEADS):
        scores(0, g, 0)

    def pair(i, carry):
        j = 2 * i
        step(j + 1, 1, j, 0)
        step(j + 2, 0, j + 1, 1)
        return carry

    lax.fori_loop(0, qi // 2, pair, 0)

    @pl.when(qi % 2 == 0)
    def _():
        step(None, None, qi, 0, True)

    @pl.when(qi % 2 == 1)
    def _():
        step(qi, 1, qi - 1, 0)
        step(None, None, qi, 1, True)

    for g in range(MLA_HEADS):
        inv_l = 1.0 / jnp.concatenate([l_ref[g]] * (V_DIM // SUBLANES), axis=0)
        o_ref[g * V_DIM:(g + 1) * V_DIM, :] = acc_ref[g] * inv_l


def _attention(qt, k, vt):
    bsz, _, tp = qt.shape
    tq = tk = SEQ_TILE
    vdim = MLA_HEADS * V_DIM
    stat = pltpu.VMEM((MLA_HEADS, SUBLANES, tq), F32)
    return pl.pallas_call(
        _attn_kernel,
        grid=(bsz, tp // tq),
        in_specs=[
            pl.BlockSpec((None, QK_WIDTH, tq), lambda b, i: (b, 0, i)),
            pl.BlockSpec((None, tp, QK_WIDTH), lambda b, i: (b, 0, 0)),
            pl.BlockSpec((None, vdim, tp), lambda b, i: (b, 0, 0)),
        ],
        out_specs=pl.BlockSpec((None, vdim, tq), lambda b, i: (b, 0, i)),
        out_shape=jax.ShapeDtypeStruct((bsz, vdim, tp), F32),
        scratch_shapes=[stat, stat, pltpu.VMEM((MLA_HEADS, V_DIM, tq), F32),
                        pltpu.VMEM((MLA_HEADS, tk, tq), F32),
                        pltpu.VMEM((MLA_HEADS, tk, tq), F32), stat, stat],
        compiler_params=pltpu.CompilerParams(
            dimension_semantics=("arbitrary", "arbitrary"), vmem_limit_bytes=VMEM_LIMIT),
        name="attention",
    )(qt, k, vt)


def _mix_out_kernel(h_ref, ylru_ref, ot_ref, gmla_ref, wout_ref, gpost_ref, o_ref):
    ot = ot_ref[...]
    ms = jnp.mean(ot * ot, axis=0, keepdims=True)
    on = (ot * lax.rsqrt(ms + EPS) * gmla_ref[...]).T.astype(BF16)
    y = jnp.dot(ylru_ref[...], wout_ref[:LRU_WIDTH, :], preferred_element_type=F32)
    y = y + jnp.dot(on, wout_ref[LRU_WIDTH:, :], preferred_element_type=F32)
    o_ref[...] = h_ref[...] + _rms(y, gpost_ref[...])


def _mix_out(h, ylru, ot, layer, p):
    bsz, tp, _ = h.shape
    tt = SEQ_TILE
    row_spec = pl.BlockSpec((None, tt, D_MODEL), lambda b, t: (b, t, 0))
    return pl.pallas_call(
        _mix_out_kernel,
        grid=(bsz, tp // tt),
        in_specs=[
            row_spec,
            pl.BlockSpec((None, tt, LRU_WIDTH), lambda b, t: (b, t, 0)),
            pl.BlockSpec((None, MLA_HEADS * V_DIM, tt), lambda b, t: (b, 0, t)),
            _const_spec((MLA_HEADS * V_DIM, 1), layer),
            _const_spec((D_MODEL, D_MODEL), layer),
            _const_spec((1, D_MODEL), layer),
        ],
        out_specs=row_spec,
        out_shape=jax.ShapeDtypeStruct(h.shape, F32),
        compiler_params=pltpu.CompilerParams(
            dimension_semantics=("arbitrary", "arbitrary"), vmem_limit_bytes=VMEM_LIMIT),
        name="mix_out",
    )(h, ylru, ot, p["mla_out_g"], p["w_out"], p["mix_post_g"])


def _rope_tables(tp):
    pos = jnp.arange(tp, dtype=F32)
    inv_freq = 1.0 / (ROPE_THETA ** (jnp.arange(0, QK_ROPE, 2, dtype=F32) / QK_ROPE))
    ang = pos[:, None] * inv_freq[None, :]
    cos, sin = jnp.cos(ang), jnp.sin(ang)
    half = QK_ROPE // 2
    zeros = lambda n: jnp.zeros((tp, n), F32)
    pad = HEAD_PAD - QK_NOPE - QK_ROPE
    c = jnp.concatenate([jnp.ones((tp, QK_NOPE), F32), cos, cos, zeros(pad)], axis=1)
    sa = jnp.concatenate([zeros(QK_NOPE + half), sin, zeros(pad)], axis=1)
    sb = jnp.concatenate([zeros(QK_NOPE), -sin, zeros(half + pad)], axis=1)
    return c, sa, sb


def _block_diag(w):
    depth = w.shape[0]
    per = LRU_HEADS // 2
    w5 = w.reshape(depth, 2, per, LRU_HEAD_DIM, LRU_HEAD_DIM)
    bd = jnp.einsum("dnhij,hg->dnhigj", w5, jnp.eye(per, dtype=w.dtype))
    return bd.reshape(depth, 2, per * LRU_HEAD_DIM, per * LRU_HEAD_DIM)


def _row(g):
    return g.reshape(g.shape[0], 1, g.shape[1])


@jax.jit
def kernel(x, meta_tokens, ffn1_pre_g, ffn1_w_gate, ffn1_w_up, ffn1_w_down, ffn1_post_g, mix_pre_g, w_in, lru_conv_w, lru_conv_b, lru_w_a, lru_b_a, lru_w_x, lru_b_x, lru_lambda, mla_q_norm_g, mla_w_uq, mla_kv_norm_g, mla_w_ukv, lru_out_g, mla_out_g, w_out, mix_post_g, ffn2_pre_g, ffn2_w_gate, ffn2_w_up, ffn2_w_down, ffn2_post_g):
    bsz, seq, _ = x.shape
    depth = w_in.shape[0]
    t_real = N_META + seq
    tp = -(-t_real // SEQ_TILE) * SEQ_TILE
    assert (bsz * tp) % FFN_ROWS == 0

    meta = jnp.broadcast_to(meta_tokens.astype(x.dtype)[None], (bsz, N_META, D_MODEL))
    h = jnp.concatenate([meta, x, jnp.zeros((bsz, tp - t_real, D_MODEL), x.dtype)], axis=1)
    rope = _rope_tables(tp)

    split = 2 * LRU_WIDTH + Q_LORA + KV_LORA
    zcols = lambda n: jnp.zeros((depth, D_MODEL, n), w_in.dtype)
    w_in_p = jnp.concatenate(
        [w_in[..., :split], zcols(QK_NOPE), w_in[..., split:],
         zcols(HEAD_PAD - QK_NOPE - QK_ROPE)], axis=-1).astype(BF16)
    w_uq_p = jnp.pad(mla_w_uq.reshape(depth, Q_LORA, MLA_HEADS, QK_NOPE + QK_ROPE),
                     ((0, 0), (0, 0), (0, 0), (0, HEAD_PAD - QK_NOPE - QK_ROPE)))
    w_ukv4 = mla_w_ukv.reshape(depth, KV_LORA, MLA_HEADS, QK_NOPE + V_DIM)
    w_uk_p = jnp.pad(w_ukv4[..., :QK_NOPE], ((0, 0), (0, 0), (0, 0), (0, HEAD_PAD - QK_NOPE)))
    mixp = {
        "mix_pre_g": _row(mix_pre_g),
        "w_in": w_in_p,
        "conv_w": lru_conv_w,
        "conv_b": _row(lru_conv_b),
        "wax": jnp.concatenate([_block_diag(lru_w_a), _block_diag(lru_w_x)], axis=-1).astype(BF16),
        "b_a": _row(lru_b_a),
        "b_x": _row(lru_b_x),
        "lam": _row(lru_lambda),
        "q_g": _row(mla_q_norm_g),
        "w_uq": w_uq_p.reshape(depth, Q_LORA, QK_WIDTH).astype(BF16),
        "kv_g": _row(mla_kv_norm_g),
        "w_uk": w_uk_p.reshape(depth, KV_LORA, QK_WIDTH).astype(BF16),
        "w_uv": w_ukv4[..., QK_NOPE:].reshape(depth, KV_LORA, MLA_HEADS * V_DIM).astype(BF16),
        "lru_out_g": _row(lru_out_g),
        "mla_out_g": mla_out_g.reshape(depth, MLA_HEADS * V_DIM, 1),
        "w_out": w_out.astype(BF16),
        "mix_post_g": _row(mix_post_g),
    }
    ffn1 = (_row(ffn1_pre_g), ffn1_w_gate.astype(BF16), ffn1_w_up.astype(BF16),
            ffn1_w_down.astype(BF16), _row(ffn1_post_g))
    ffn2 = (_row(ffn2_pre_g), ffn2_w_gate.astype(BF16), ffn2_w_up.astype(BF16),
            ffn2_w_down.astype(BF16), _row(ffn2_post_g))

    for layer in range(depth):
        h = _ffn(h.reshape(bsz * tp, D_MODEL), layer, *ffn1).reshape(bsz, tp, D_MODEL)
        ylru, qt, k, vt = _mix_in(h, layer, mixp, rope)
        ot = _attention(qt, k, vt)
        h = _mix_out(h, ylru, ot, layer, mixp)
        h = _ffn(h.reshape(bsz * tp, D_MODEL), layer, *ffn2).reshape(bsz, tp, D_MODEL)
    return h[:, N_META:t_real]
```

```python
import functools

import jax
import jax.numpy as jnp
from jax import lax
from jax.experimental import pallas as pl
from jax.experimental.pallas import tpu as pltpu

D_MODEL = 1024
N_META = 16
LRU_WIDTH = 512
LRU_HEADS = 8
LRU_HEAD_DIM = 64
CONV_WIDTH = 4
LRU_C = 8.0
MLA_HEADS = 8
QK_NOPE = 64
QK_ROPE = 32
V_DIM = 64
Q_LORA = 384
KV_LORA = 256
D_FF = 2816
ROPE_THETA = 10000.0
EPS = 1e-6

LANES = 128
SUBLANES = 8
HEAD_PAD = LANES
QK_WIDTH = MLA_HEADS * HEAD_PAD
IN_COLS_PAD = 2 * LRU_WIDTH + Q_LORA + KV_LORA + HEAD_PAD
SEQ_TILE = 256
META_ROWS = LANES
FFN_ROWS = 512
FF_CHUNK = 256
NEG_BIG = -1e30
LOG2_E = 1.4426950408889634
VMEM_LIMIT = 52 * 1024 * 1024

F32 = jnp.float32
BF16 = jnp.bfloat16


def _rms(x, g):
    return x * lax.rsqrt(jnp.mean(x * x, axis=-1, keepdims=True) + EPS) * g


def _const_spec(shape, layer=None):
    if layer is None:
        idx = lambda *_: (0,) * len(shape)
        return pl.BlockSpec(shape, idx, pipeline_mode=pl.Buffered(1))
    idx = lambda *_: (layer,) + (0,) * len(shape)
    return pl.BlockSpec((None,) + shape, idx, pipeline_mode=pl.Buffered(1))


def _ffn_kernel(h_ref, gpre_ref, wg_ref, wu_ref, wd_ref, gpost_ref, o_ref, a_ref):
    x = h_ref[...]
    u = _rms(x, gpre_ref[...]).astype(BF16)
    for c in range(D_FF // FF_CHUNK):
        sl = slice(c * FF_CHUNK, (c + 1) * FF_CHUNK)
        gate = jnp.dot(u, wg_ref[:, sl], preferred_element_type=F32)
        up = jnp.dot(u, wu_ref[:, sl], preferred_element_type=F32)
        a_ref[:, sl] = (gate * jax.nn.sigmoid(gate) * up).astype(BF16)
    f = jnp.dot(a_ref[...], wd_ref[...], preferred_element_type=F32)
    o_ref[...] = x + 0.5 * _rms(f, gpost_ref[...])


def _ffn(h3d, layer, gpre, wg, wu, wd, gpost):
    h2d = h3d.reshape(-1, D_MODEL)
    rows = h2d.shape[0]
    tile = min(FFN_ROWS, rows)
    assert rows % tile == 0
    row_spec = pl.BlockSpec((tile, D_MODEL), lambda i: (i, 0))
    out = pl.pallas_call(
        _ffn_kernel,
        grid=(rows // tile,),
        in_specs=[
            row_spec,
            _const_spec((1, D_MODEL), layer),
            _const_spec((D_MODEL, D_FF), layer),
            _const_spec((D_MODEL, D_FF), layer),
            _const_spec((D_FF, D_MODEL), layer),
            _const_spec((1, D_MODEL), layer),
        ],
        out_specs=row_spec,
        out_shape=jax.ShapeDtypeStruct(h2d.shape, F32),
        scratch_shapes=[pltpu.VMEM((tile, D_FF), BF16)],
        compiler_params=pltpu.CompilerParams(
            dimension_semantics=("arbitrary",), vmem_limit_bytes=VMEM_LIMIT),
        name="ffn",
    )(h2d, gpre, wg, wu, wd, gpost)
    return out.reshape(h3d.shape)


def _rope(x, c, sa, sb):
    n = x.shape[-1]
    half = QK_ROPE // 2
    return x * c + pltpu.roll(x, half, 1) * sa + pltpu.roll(x, n - half, 1) * sb


def _mix_in_kernel(emit_state, h_ref, g_ref, win_ref, cw_ref, cb_ref, wax_ref, ba_ref, bx_ref,
                   lam_ref, gq_ref, wuq_ref, gkv_ref, wuk_ref, wuv_ref, glru_ref,
                   rc_ref, rsa_ref, rsb_ref, xprev0_ref, hcar0_ref,
                   ylru_ref, qt_ref, k_ref, vt_ref, *rest):
    if emit_state:
        xr_out_ref, hs_out_ref, xprev_ref, hcar_ref = rest
    else:
        xprev_ref, hcar_ref = rest
    tt = h_ref.shape[0]

    @pl.when(pl.program_id(1) == 0)
    def _():
        xprev_ref[...] = xprev0_ref[...]
        hcar_ref[...] = hcar0_ref[...]

    u = _rms(h_ref[...], g_ref[...]).astype(BF16)
    z = jnp.dot(u, win_ref[...], preferred_element_type=F32)
    xr = z[:, 0:LRU_WIDTH]
    gr = z[:, LRU_WIDTH:2 * LRU_WIDTH]
    cq = z[:, 2 * LRU_WIDTH:2 * LRU_WIDTH + Q_LORA]
    ckv = z[:, 2 * LRU_WIDTH + Q_LORA:2 * LRU_WIDTH + Q_LORA + KV_LORA]
    krt = z[:, 2 * LRU_WIDTH + Q_LORA + KV_LORA:]

    xe = jnp.concatenate([xprev_ref[...], xr], axis=0)
    xc = xr * cw_ref[CONV_WIDTH - 1:CONV_WIDTH, :] + cb_ref[...]
    for j in range(1, CONV_WIDTH):
        w_j = cw_ref[CONV_WIDTH - 1 - j:CONV_WIDTH - j, :]
        xc = xc + pltpu.roll(xe, j, 0)[SUBLANES:] * w_j
    xprev_ref[...] = xr[tt - SUBLANES:]

    xcb = xc.astype(BF16)
    half = LRU_WIDTH // 2
    ri0 = jnp.dot(xcb[:, :half], wax_ref[0], preferred_element_type=F32)
    ri1 = jnp.dot(xcb[:, half:], wax_ref[1], preferred_element_type=F32)
    r = jax.nn.sigmoid(jnp.concatenate([ri0[:, :half], ri1[:, :half]], axis=1) + ba_ref[...])
    i = jax.nn.sigmoid(jnp.concatenate([ri0[:, half:], ri1[:, half:]], axis=1) + bx_ref[...])
    nl = -lam_ref[...]
    softplus = jnp.maximum(nl, 0.0) + jnp.log1p(jnp.exp(-jnp.abs(nl)))
    log_a = (-LRU_C) * r * softplus
    a = jnp.exp(log_a)
    b = jnp.sqrt(1.0 - a * a) * (i * xc)

    row = lax.broadcasted_iota(jnp.int32, a.shape, 0) % SUBLANES
    s = 1
    while s < SUBLANES:
        valid = row >= s
        b = jnp.where(valid, a * pltpu.roll(b, s, 0) + b, b)
        a = jnp.where(valid, a * pltpu.roll(a, s, 0), a)
        s *= 2
    carry = hcar_ref[...]
    groups = []
    for gidx in range(tt // SUBLANES):
        sl = slice(gidx * SUBLANES, (gidx + 1) * SUBLANES)
        hg = b[sl] + a[sl] * carry
        carry = hg[SUBLANES - 1:SUBLANES]
        groups.append(hg)
    hcar_ref[...] = carry
    hs = jnp.concatenate(groups, axis=0)
    if emit_state:
        xr_out_ref[...] = xr
        hs_out_ref[...] = hs

    cg = 0.7978845608028654
    gelu = 0.5 * gr * (1.0 + jnp.tanh(cg * (gr + 0.044715 * (gr * gr * gr))))
    ylru_ref[...] = _rms(hs * gelu, glru_ref[...]).astype(BF16)

    rc = rc_ref[...]
    rsa = rsa_ref[...]
    rsb = rsb_ref[...]
    tile_heads = lambda t: jnp.concatenate([t] * MLA_HEADS, axis=1)
    cqn = _rms(cq, gq_ref[...]).astype(BF16)
    q = jnp.dot(cqn, wuq_ref[...], preferred_element_type=F32)
    q = _rope(q, tile_heads(rc), tile_heads(rsa), tile_heads(rsb))
    q = q * (LOG2_E * (QK_NOPE + QK_ROPE) ** -0.5)
    qt_ref[...] = q.T.astype(BF16)

    ckvn = _rms(ckv, gkv_ref[...]).astype(BF16)
    kn = jnp.dot(ckvn, wuk_ref[...], preferred_element_type=F32)
    kr = _rope(krt, rc, rsa, rsb)
    k_ref[...] = (kn + tile_heads(kr)).astype(BF16)
    v = jnp.dot(ckvn, wuv_ref[...], preferred_element_type=F32)
    vt_ref[...] = v.T.astype(BF16)


def _mix_in(h, layer, p, rope, xprev0, hcar0, emit_state=False):
    bsz, tp, _ = h.shape
    tt = min(SEQ_TILE, tp)
    nt = tp // tt
    tbl_spec = pl.BlockSpec((tt, HEAD_PAD), lambda b, t: (t, 0))
    seq_spec = pl.BlockSpec((None, tt, LRU_WIDTH), lambda b, t: (b, t, 0))
    state_specs = [seq_spec, seq_spec] if emit_state else []
    state_shapes = [jax.ShapeDtypeStruct((bsz, tp, LRU_WIDTH), F32)] * 2 if emit_state else []
    return pl.pallas_call(
        functools.partial(_mix_in_kernel, emit_state),
        grid=(bsz, nt),
        in_specs=[
            pl.BlockSpec((None, tt, D_MODEL), lambda b, t: (b, t, 0)),
            _const_spec((1, D_MODEL), layer),
            _const_spec((D_MODEL, IN_COLS_PAD), layer),
            _const_spec((CONV_WIDTH, LRU_WIDTH), layer),
            _const_spec((1, LRU_WIDTH), layer),
            _const_spec((2, LRU_WIDTH // 2, LRU_WIDTH), layer),
            _const_spec((1, LRU_WIDTH), layer),
            _const_spec((1, LRU_WIDTH), layer),
            _const_spec((1, LRU_WIDTH), layer),
            _const_spec((1, Q_LORA), layer),
            _const_spec((Q_LORA, QK_WIDTH), layer),
            _const_spec((1, KV_LORA), layer),
            _const_spec((KV_LORA, QK_WIDTH), layer),
            _const_spec((KV_LORA, MLA_HEADS * V_DIM), layer),
            _const_spec((1, LRU_WIDTH), layer),
            tbl_spec, tbl_spec, tbl_spec,
            _const_spec((SUBLANES, LRU_WIDTH)),
            _const_spec((1, LRU_WIDTH)),
        ],
        out_specs=[
            seq_spec,
            pl.BlockSpec((None, QK_WIDTH, tt), lambda b, t: (b, 0, t)),
            pl.BlockSpec((None, tt, QK_WIDTH), lambda b, t: (b, t, 0)),
            pl.BlockSpec((None, MLA_HEADS * V_DIM, tt), lambda b, t: (b, 0, t)),
        ] + state_specs,
        out_shape=[
            jax.ShapeDtypeStruct((bsz, tp, LRU_WIDTH), BF16),
            jax.ShapeDtypeStruct((bsz, QK_WIDTH, tp), BF16),
            jax.ShapeDtypeStruct((bsz, tp, QK_WIDTH), BF16),
            jax.ShapeDtypeStruct((bsz, MLA_HEADS * V_DIM, tp), BF16),
        ] + state_shapes,
        scratch_shapes=[pltpu.VMEM((SUBLANES, LRU_WIDTH), F32),
                        pltpu.VMEM((1, LRU_WIDTH), F32)],
        compiler_params=pltpu.CompilerParams(
            dimension_semantics=("arbitrary", "arbitrary"), vmem_limit_bytes=VMEM_LIMIT),
        name="mix_in",
    )(h, p["mix_pre_g"], p["w_in"], p["conv_w"], p["conv_b"], p["wax"], p["b_a"], p["b_x"],
      p["lam"], p["q_g"], p["w_uq"], p["kv_g"], p["w_uk"], p["w_uv"], p["lru_out_g"], *rope,
      xprev0, hcar0)


def _group_max(s):
    m = s[0:SUBLANES]
    for r in range(1, s.shape[0] // SUBLANES):
        m = jnp.maximum(m, s[r * SUBLANES:(r + 1) * SUBLANES])
    return m


def _sublane_allmax(m):
    for shift in (1, 2, 4):
        m = jnp.maximum(m, pltpu.roll(m, shift, 0))
    return m


def _attn_kernel(has_prefix, qt_ref, k_ref, vt_ref, *rest):
    if has_prefix:
        kpre_ref, vtpre_ref = rest[:2]
        rest = rest[2:]
    o_ref, m_ref, l_ref, acc_ref, s0_ref, s1_ref, bm0_ref, bm1_ref = rest
    qi = pl.program_id(1)
    tq = qt_ref.shape[1]
    tk = s0_ref.shape[1]
    s_refs = (s0_ref, s1_ref)
    bm_refs = (bm0_ref, bm1_ref)
    ones_rows = 2 * SUBLANES

    def with_ones(vb):
        return jnp.concatenate([vb, jnp.ones((ones_rows, vb.shape[1]), BF16)], axis=0)

    if has_prefix:
        for g in range(MLA_HEADS):
            qt = qt_ref[g * HEAD_PAD:(g + 1) * HEAD_PAD, :]
            s = jnp.dot(kpre_ref[:, g * HEAD_PAD:(g + 1) * HEAD_PAD], qt,
                        preferred_element_type=F32)
            m0 = _sublane_allmax(_group_max(s))
            p = jnp.exp2(s - jnp.concatenate([m0] * (N_META // SUBLANES), axis=0))
            p = jnp.concatenate([p.astype(BF16), jnp.zeros((LANES - N_META, tq), BF16)], axis=0)
            pv = jnp.dot(with_ones(vtpre_ref[g * V_DIM:(g + 1) * V_DIM, :]), p,
                         preferred_element_type=F32)
            m_ref[g] = m0
            acc_ref[g] = pv[:V_DIM]
            l_ref[g] = pv[V_DIM:V_DIM + SUBLANES]
    else:
        m_ref[...] = jnp.full(m_ref.shape, NEG_BIG, F32)
        l_ref[...] = jnp.zeros(l_ref.shape, F32)
        acc_ref[...] = jnp.zeros(acc_ref.shape, F32)

    def scores(j, g, slot):
        off = pl.multiple_of(j * tk, tk)
        s = jnp.dot(k_ref[pl.ds(off, tk), g * HEAD_PAD:(g + 1) * HEAD_PAD],
                    qt_ref[g * HEAD_PAD:(g + 1) * HEAD_PAD, :], preferred_element_type=F32)
        s_refs[slot][g] = s
        bm_refs[slot][g] = _group_max(s)

    def softmax_pv(j, g, slot, masked):
        s = s_refs[slot][g]
        if masked:
            key = lax.broadcasted_iota(jnp.int32, s.shape, 0)
            qry = lax.broadcasted_iota(jnp.int32, s.shape, 1)
            s = jnp.where(key <= qry, s, NEG_BIG)
            bm = _group_max(s)
        else:
            bm = bm_refs[slot][g]
        m_old = m_ref[g]
        m_new = jnp.maximum(m_old, _sublane_allmax(bm))
        alpha = jnp.exp2(m_old - m_new)
        p = jnp.exp2(s - jnp.concatenate([m_new] * (tk // SUBLANES), axis=0))
        m_ref[g] = m_new
        off = pl.multiple_of(j * tk, tk)
        vb = with_ones(vt_ref[g * V_DIM:(g + 1) * V_DIM, pl.ds(off, tk)])
        pv = jnp.dot(vb, p.astype(BF16), preferred_element_type=F32)
        acc_ref[g] = (jnp.concatenate([alpha] * (V_DIM // SUBLANES), axis=0) * acc_ref[g]
                      + pv[:V_DIM])
        l_ref[g] = alpha * l_ref[g] + pv[V_DIM:V_DIM + SUBLANES]

    def step(j_next, slot_next, j, slot, masked=False):
        for g in range(MLA_HEADS):
            if j_next is not None:
                scores(j_next, g, slot_next)
            softmax_pv(j, g, slot, masked)

    for g in range(MLA_HEADS):
        scores(0, g, 0)

    def pair(i, carry):
        j = 2 * i
        step(j + 1, 1, j, 0)
        step(j + 2, 0, j + 1, 1)
        return carry

    lax.fori_loop(0, qi // 2, pair, 0)

    @pl.when(qi % 2 == 0)
    def _():
        step(None, None, qi, 0, True)

    @pl.when(qi % 2 == 1)
    def _():
        step(qi, 1, qi - 1, 0)
        step(None, None, qi, 1, True)

    for g in range(MLA_HEADS):
        inv_l = 1.0 / jnp.concatenate([l_ref[g]] * (V_DIM // SUBLANES), axis=0)
        o_ref[g * V_DIM:(g + 1) * V_DIM, :] = acc_ref[g] * inv_l


def _attention(qt, k, vt, prefix=None):
    bsz, _, tp = qt.shape
    tq = tk = min(SEQ_TILE, tp)
    vdim = MLA_HEADS * V_DIM
    stat = pltpu.VMEM((MLA_HEADS, SUBLANES, tq), F32)
    has_prefix = prefix is not None
    prefix_specs = [_const_spec((N_META, QK_WIDTH)), _const_spec((vdim, LANES))] if has_prefix else []
    return pl.pallas_call(
        functools.partial(_attn_kernel, has_prefix),
        grid=(bsz, tp // tq),
        in_specs=[
            pl.BlockSpec((None, QK_WIDTH, tq), lambda b, i: (b, 0, i)),
            pl.BlockSpec((None, tp, QK_WIDTH), lambda b, i: (b, 0, 0)),
            pl.BlockSpec((None, vdim, tp), lambda b, i: (b, 0, 0)),
        ] + prefix_specs,
        out_specs=pl.BlockSpec((None, vdim, tq), lambda b, i: (b, 0, i)),
        out_shape=jax.ShapeDtypeStruct((bsz, vdim, tp), F32),
        scratch_shapes=[stat, stat, pltpu.VMEM((MLA_HEADS, V_DIM, tq), F32),
                        pltpu.VMEM((MLA_HEADS, tk, tq), F32),
                        pltpu.VMEM((MLA_HEADS, tk, tq), F32), stat, stat],
        compiler_params=pltpu.CompilerParams(
            dimension_semantics=("arbitrary", "arbitrary"), vmem_limit_bytes=VMEM_LIMIT),
        name="attention",
    )(qt, k, vt, *(prefix or ()))


def _mix_out_kernel(h_ref, ylru_ref, ot_ref, gmla_ref, wout_ref, gpost_ref, o_ref):
    ot = ot_ref[...]
    ms = jnp.mean(ot * ot, axis=0, keepdims=True)
    on = (ot * lax.rsqrt(ms + EPS) * gmla_ref[...]).T.astype(BF16)
    y = jnp.dot(ylru_ref[...], wout_ref[:LRU_WIDTH, :], preferred_element_type=F32)
    y = y + jnp.dot(on, wout_ref[LRU_WIDTH:, :], preferred_element_type=F32)
    o_ref[...] = h_ref[...] + _rms(y, gpost_ref[...])


def _mix_out(h, ylru, ot, layer, p):
    bsz, tp, _ = h.shape
    tt = min(SEQ_TILE, tp)
    row_spec = pl.BlockSpec((None, tt, D_MODEL), lambda b, t: (b, t, 0))
    return pl.pallas_call(
        _mix_out_kernel,
        grid=(bsz, tp // tt),
        in_specs=[
            row_spec,
            pl.BlockSpec((None, tt, LRU_WIDTH), lambda b, t: (b, t, 0)),
            pl.BlockSpec((None, MLA_HEADS * V_DIM, tt), lambda b, t: (b, 0, t)),
            _const_spec((MLA_HEADS * V_DIM, 1), layer),
            _const_spec((D_MODEL, D_MODEL), layer),
            _const_spec((1, D_MODEL), layer),
        ],
        out_specs=row_spec,
        out_shape=jax.ShapeDtypeStruct(h.shape, F32),
        compiler_params=pltpu.CompilerParams(
            dimension_semantics=("arbitrary", "arbitrary"), vmem_limit_bytes=VMEM_LIMIT),
        name="mix_out",
    )(h, ylru, ot, p["mla_out_g"], p["w_out"], p["mix_post_g"])


def _rope_tables(tp):
    pos = jnp.arange(tp, dtype=F32)
    inv_freq = 1.0 / (ROPE_THETA ** (jnp.arange(0, QK_ROPE, 2, dtype=F32) / QK_ROPE))
    ang = pos[:, None] * inv_freq[None, :]
    cos, sin = jnp.cos(ang), jnp.sin(ang)
    half = QK_ROPE // 2
    zeros = lambda n: jnp.zeros((tp, n), F32)
    pad = HEAD_PAD - QK_NOPE - QK_ROPE
    c = jnp.concatenate([jnp.ones((tp, QK_NOPE), F32), cos, cos, zeros(pad)], axis=1)
    sa = jnp.concatenate([zeros(QK_NOPE + half), sin, zeros(pad)], axis=1)
    sb = jnp.concatenate([zeros(QK_NOPE), -sin, zeros(half + pad)], axis=1)
    return c, sa, sb


def _block_diag(w):
    depth = w.shape[0]
    per = LRU_HEADS // 2
    w5 = w.reshape(depth, 2, per, LRU_HEAD_DIM, LRU_HEAD_DIM)
    bd = jnp.einsum("dnhij,hg->dnhigj", w5, jnp.eye(per, dtype=w.dtype))
    return bd.reshape(depth, 2, per * LRU_HEAD_DIM, per * LRU_HEAD_DIM)


def _row(g):
    return g.reshape(g.shape[0], 1, g.shape[1])


@jax.jit
def kernel(x, meta_tokens, ffn1_pre_g, ffn1_w_gate, ffn1_w_up, ffn1_w_down, ffn1_post_g, mix_pre_g, w_in, lru_conv_w, lru_conv_b, lru_w_a, lru_b_a, lru_w_x, lru_b_x, lru_lambda, mla_q_norm_g, mla_w_uq, mla_kv_norm_g, mla_w_ukv, lru_out_g, mla_out_g, w_out, mix_post_g, ffn2_pre_g, ffn2_w_gate, ffn2_w_up, ffn2_w_down, ffn2_post_g):
    bsz, seq, _ = x.shape
    depth = w_in.shape[0]
    assert seq % SEQ_TILE == 0
    h = x
    hm = jnp.concatenate([meta_tokens.astype(x.dtype),
                          jnp.zeros((META_ROWS - N_META, D_MODEL), x.dtype)], axis=0)[None]
    rope_all = _rope_tables(N_META + max(seq, META_ROWS))
    rope_meta = tuple(t[:META_ROWS] for t in rope_all)
    rope_main = tuple(t[N_META:N_META + seq] for t in rope_all)
    zero_hist = jnp.zeros((SUBLANES, LRU_WIDTH), F32)
    zero_state = jnp.zeros((1, LRU_WIDTH), F32)

    split = 2 * LRU_WIDTH + Q_LORA + KV_LORA
    zcols = lambda n: jnp.zeros((depth, D_MODEL, n), w_in.dtype)
    w_in_p = jnp.concatenate(
        [w_in[..., :split], zcols(QK_NOPE), w_in[..., split:],
         zcols(HEAD_PAD - QK_NOPE - QK_ROPE)], axis=-1).astype(BF16)
    w_uq_p = jnp.pad(mla_w_uq.reshape(depth, Q_LORA, MLA_HEADS, QK_NOPE + QK_ROPE),
                     ((0, 0), (0, 0), (0, 0), (0, HEAD_PAD - QK_NOPE - QK_ROPE)))
    w_ukv4 = mla_w_ukv.reshape(depth, KV_LORA, MLA_HEADS, QK_NOPE + V_DIM)
    w_uk_p = jnp.pad(w_ukv4[..., :QK_NOPE], ((0, 0), (0, 0), (0, 0), (0, HEAD_PAD - QK_NOPE)))
    mixp = {
        "mix_pre_g": _row(mix_pre_g),
        "w_in": w_in_p,
        "conv_w": lru_conv_w,
        "conv_b": _row(lru_conv_b),
        "wax": jnp.concatenate([_block_diag(lru_w_a), _block_diag(lru_w_x)], axis=-1).astype(BF16),
        "b_a": _row(lru_b_a),
        "b_x": _row(lru_b_x),
        "lam": _row(lru_lambda),
        "q_g": _row(mla_q_norm_g),
        "w_uq": w_uq_p.reshape(depth, Q_LORA, QK_WIDTH).astype(BF16),
        "kv_g": _row(mla_kv_norm_g),
        "w_uk": w_uk_p.reshape(depth, KV_LORA, QK_WIDTH).astype(BF16),
        "w_uv": w_ukv4[..., QK_NOPE:].reshape(depth, KV_LORA, MLA_HEADS * V_DIM).astype(BF16),
        "lru_out_g": _row(lru_out_g),
        "mla_out_g": mla_out_g.reshape(depth, MLA_HEADS * V_DIM, 1),
        "w_out": w_out.astype(BF16),
        "mix_post_g": _row(mix_post_g),
    }
    ffn1 = (_row(ffn1_pre_g), ffn1_w_gate.astype(BF16), ffn1_w_up.astype(BF16),
            ffn1_w_down.astype(BF16), _row(ffn1_post_g))
    ffn2 = (_row(ffn2_pre_g), ffn2_w_gate.astype(BF16), ffn2_w_up.astype(BF16),
            ffn2_w_down.astype(BF16), _row(ffn2_post_g))

    lane = lax.broadcasted_iota(jnp.int32, (MLA_HEADS * V_DIM, META_ROWS), 1)
    for layer in range(depth):
        hm = _ffn(hm, layer, *ffn1)
        h = _ffn(h, layer, *ffn1)
        ylru_m, qt_m, k_m, vt_m, xr_m, hs_m = _mix_in(
            hm, layer, mixp, rope_meta, zero_hist, zero_state, emit_state=True)
        ylru, qt, k, vt = _mix_in(h, layer, mixp, rope_main,
                                  xr_m[0, N_META - SUBLANES:N_META], hs_m[0, N_META - 1:N_META])
        prefix = (k_m[0, :N_META], jnp.where(lane < N_META, vt_m[0], 0).astype(BF16))
        h = _mix_out(h, ylru, _attention(qt, k, vt, prefix), layer, mixp)
        h = _ffn(h, layer, *ffn2)
        if layer + 1 < depth:
            hm = _mix_out(hm, ylru_m, _attention(qt_m, k_m, vt_m), layer, mixp)
            hm = _ffn(hm, layer, *ffn2)
    return h
```

```python
import functools

import jax
import jax.numpy as jnp
from jax import lax
from jax.experimental import pallas as pl
from jax.experimental.pallas import tpu as pltpu

D_MODEL = 1024
N_META = 16
LRU_WIDTH = 512
LRU_HEADS = 8
LRU_HEAD_DIM = 64
CONV_WIDTH = 4
LRU_C = 8.0
MLA_HEADS = 8
QK_NOPE = 64
QK_ROPE = 32
V_DIM = 64
Q_LORA = 384
KV_LORA = 256
D_FF = 2816
ROPE_THETA = 10000.0
EPS = 1e-6

LANES = 128
SUBLANES = 8
HEAD_PAD = LANES
QK_WIDTH = MLA_HEADS * HEAD_PAD
IN_COLS_PAD = 2 * LRU_WIDTH + Q_LORA + KV_LORA + HEAD_PAD
SEQ_TILE = 256
META_ROWS = LANES
FFN_ROWS = 512
FF_CHUNK = 256
NEG_BIG = -1e30
LOG2_E = 1.4426950408889634
VMEM_LIMIT = 52 * 1024 * 1024

F32 = jnp.float32
BF16 = jnp.bfloat16


def _rms(x, g):
    return x * lax.rsqrt(jnp.mean(x * x, axis=-1, keepdims=True) + EPS) * g


def _const_spec(shape, layer=None):
    if layer is None:
        idx = lambda *_: (0,) * len(shape)
        return pl.BlockSpec(shape, idx, pipeline_mode=pl.Buffered(1))
    idx = lambda *_: (layer,) + (0,) * len(shape)
    return pl.BlockSpec((None,) + shape, idx, pipeline_mode=pl.Buffered(1))


def _ffn_kernel(h_ref, gpre_ref, wg_ref, wu_ref, wd_ref, gpost_ref, o_ref, a_ref):
    x = h_ref[...]
    u = _rms(x, gpre_ref[...]).astype(BF16)
    for c in range(D_FF // FF_CHUNK):
        sl = slice(c * FF_CHUNK, (c + 1) * FF_CHUNK)
        gate = jnp.dot(u, wg_ref[:, sl], preferred_element_type=F32)
        up = jnp.dot(u, wu_ref[:, sl], preferred_element_type=F32)
        a_ref[:, sl] = (gate * jax.nn.sigmoid(gate) * up).astype(BF16)
    f = jnp.dot(a_ref[...], wd_ref[...], preferred_element_type=F32)
    o_ref[...] = x + 0.5 * _rms(f, gpost_ref[...])


def _ffn(h3d, layer, gpre, wg, wu, wd, gpost):
    h2d = h3d.reshape(-1, D_MODEL)
    rows = h2d.shape[0]
    tile = min(FFN_ROWS, rows)
    assert rows % tile == 0
    row_spec = pl.BlockSpec((tile, D_MODEL), lambda i: (i, 0))
    out = pl.pallas_call(
        _ffn_kernel,
        grid=(rows // tile,),
        in_specs=[
            row_spec,
            _const_spec((1, D_MODEL), layer),
            _const_spec((D_MODEL, D_FF), layer),
            _const_spec((D_MODEL, D_FF), layer),
            _const_spec((D_FF, D_MODEL), layer),
            _const_spec((1, D_MODEL), layer),
        ],
        out_specs=row_spec,
        out_shape=jax.ShapeDtypeStruct(h2d.shape, F32),
        scratch_shapes=[pltpu.VMEM((tile, D_FF), BF16)],
        compiler_params=pltpu.CompilerParams(
            dimension_semantics=("arbitrary",), vmem_limit_bytes=VMEM_LIMIT),
        name="ffn",
    )(h2d, gpre, wg, wu, wd, gpost)
    return out.reshape(h3d.shape)


def _rope(x, c, sa, sb):
    n = x.shape[-1]
    half = QK_ROPE // 2
    return x * c + pltpu.roll(x, half, 1) * sa + pltpu.roll(x, n - half, 1) * sb


def _mix_in_kernel(emit_state, h_ref, g_ref, win_ref, cw_ref, cb_ref, wax_ref, ba_ref, bx_ref,
                   lam_ref, gq_ref, wuq_ref, gkv_ref, wuk_ref, wuv_ref, glru_ref,
                   rc_ref, rsa_ref, rsb_ref, xprev0_ref, hcar0_ref,
                   ylru_ref, qt_ref, k_ref, vt_ref, *rest):
    if emit_state:
        xr_out_ref, hs_out_ref, xprev_ref, hcar_ref = rest
    else:
        xprev_ref, hcar_ref = rest
    tt = h_ref.shape[0]

    @pl.when(pl.program_id(1) == 0)
    def _():
        xprev_ref[...] = xprev0_ref[...]
        hcar_ref[...] = hcar0_ref[...]

    u = _rms(h_ref[...], g_ref[...]).astype(BF16)
    z = jnp.dot(u, win_ref[...], preferred_element_type=F32)
    xr = z[:, 0:LRU_WIDTH]
    gr = z[:, LRU_WIDTH:2 * LRU_WIDTH]
    cq = z[:, 2 * LRU_WIDTH:2 * LRU_WIDTH + Q_LORA]
    ckv = z[:, 2 * LRU_WIDTH + Q_LORA:2 * LRU_WIDTH + Q_LORA + KV_LORA]
    krt = z[:, 2 * LRU_WIDTH + Q_LORA + KV_LORA:]

    xe = jnp.concatenate([xprev_ref[...], xr], axis=0)
    xc = xr * cw_ref[CONV_WIDTH - 1:CONV_WIDTH, :] + cb_ref[...]
    for j in range(1, CONV_WIDTH):
        w_j = cw_ref[CONV_WIDTH - 1 - j:CONV_WIDTH - j, :]
        xc = xc + pltpu.roll(xe, j, 0)[SUBLANES:] * w_j
    xprev_ref[...] = xr[tt - SUBLANES:]

    xcb = xc.astype(BF16)
    half = LRU_WIDTH // 2
    ri0 = jnp.dot(xcb[:, :half], wax_ref[0], preferred_element_type=F32)
    ri1 = jnp.dot(xcb[:, half:], wax_ref[1], preferred_element_type=F32)
    r = jax.nn.sigmoid(jnp.concatenate([ri0[:, :half], ri1[:, :half]], axis=1) + ba_ref[...])
    i = jax.nn.sigmoid(jnp.concatenate([ri0[:, half:], ri1[:, half:]], axis=1) + bx_ref[...])
    nl = -lam_ref[...]
    softplus = jnp.maximum(nl, 0.0) + jnp.log1p(jnp.exp(-jnp.abs(nl)))
    log_a = (-LRU_C) * r * softplus
    a = jnp.exp(log_a)
    b = jnp.sqrt(1.0 - a * a) * (i * xc)

    row = lax.broadcasted_iota(jnp.int32, a.shape, 0) % SUBLANES
    s = 1
    while s < SUBLANES:
        valid = row >= s
        b = jnp.where(valid, a * pltpu.roll(b, s, 0) + b, b)
        a = jnp.where(valid, a * pltpu.roll(a, s, 0), a)
        s *= 2
    carry = hcar_ref[...]
    groups = []
    for gidx in range(tt // SUBLANES):
        sl = slice(gidx * SUBLANES, (gidx + 1) * SUBLANES)
        hg = b[sl] + a[sl] * carry
        carry = hg[SUBLANES - 1:SUBLANES]
        groups.append(hg)
    hcar_ref[...] = carry
    hs = jnp.concatenate(groups, axis=0)
    if emit_state:
        xr_out_ref[...] = xr
        hs_out_ref[...] = hs

    cg = 0.7978845608028654
    gelu = 0.5 * gr * (1.0 + jnp.tanh(cg * (gr + 0.044715 * (gr * gr * gr))))
    ylru_ref[...] = _rms(hs * gelu, glru_ref[...]).astype(BF16)

    rc = rc_ref[...]
    rsa = rsa_ref[...]
    rsb = rsb_ref[...]
    tile_heads = lambda t: jnp.concatenate([t] * MLA_HEADS, axis=1)
    cqn = _rms(cq, gq_ref[...]).astype(BF16)
    q = jnp.dot(cqn, wuq_ref[...], preferred_element_type=F32)
    q = _rope(q, tile_heads(rc), tile_heads(rsa), tile_heads(rsb))
    q = q * (LOG2_E * (QK_NOPE + QK_ROPE) ** -0.5)
    qt_ref[...] = q.T.astype(BF16)

    ckvn = _rms(ckv, gkv_ref[...]).astype(BF16)
    kn = jnp.dot(ckvn, wuk_ref[...], preferred_element_type=F32)
    kr = _rope(krt, rc, rsa, rsb)
    k_ref[...] = (kn + tile_heads(kr)).astype(BF16)
    v = jnp.dot(ckvn, wuv_ref[...], preferred_element_type=F32)
    vt_ref[...] = v.T.astype(BF16)


def _mix_in(h, layer, p, rope, xprev0, hcar0, emit_state=False):
    bsz, tp, _ = h.shape
    tt = min(SEQ_TILE, tp)
    nt = tp // tt
    tbl_spec = pl.BlockSpec((tt, HEAD_PAD), lambda b, t: (t, 0))
    seq_spec = pl.BlockSpec((None, tt, LRU_WIDTH), lambda b, t: (b, t, 0))
    state_specs = [seq_spec, seq_spec] if emit_state else []
    state_shapes = [jax.ShapeDtypeStruct((bsz, tp, LRU_WIDTH), F32)] * 2 if emit_state else []
    return pl.pallas_call(
        functools.partial(_mix_in_kernel, emit_state),
        grid=(bsz, nt),
        in_specs=[
            pl.BlockSpec((None, tt, D_MODEL), lambda b, t: (b, t, 0)),
            _const_spec((1, D_MODEL), layer),
            _const_spec((D_MODEL, IN_COLS_PAD), layer),
            _const_spec((CONV_WIDTH, LRU_WIDTH), layer),
            _const_spec((1, LRU_WIDTH), layer),
            _const_spec((2, LRU_WIDTH // 2, LRU_WIDTH), layer),
            _const_spec((1, LRU_WIDTH), layer),
            _const_spec((1, LRU_WIDTH), layer),
            _const_spec((1, LRU_WIDTH), layer),
            _const_spec((1, Q_LORA), layer),
            _const_spec((Q_LORA, QK_WIDTH), layer),
            _const_spec((1, KV_LORA), layer),
            _const_spec((KV_LORA, QK_WIDTH), layer),
            _const_spec((KV_LORA, MLA_HEADS * V_DIM), layer),
            _const_spec((1, LRU_WIDTH), layer),
            tbl_spec, tbl_spec, tbl_spec,
            _const_spec((SUBLANES, LRU_WIDTH)),
            _const_spec((1, LRU_WIDTH)),
        ],
        out_specs=[
            seq_spec,
            pl.BlockSpec((None, QK_WIDTH, tt), lambda b, t: (b, 0, t)),
            pl.BlockSpec((None, tt, QK_WIDTH), lambda b, t: (b, t, 0)),
            pl.BlockSpec((None, MLA_HEADS * V_DIM, tt), lambda b, t: (b, 0, t)),
        ] + state_specs,
        out_shape=[
            jax.ShapeDtypeStruct((bsz, tp, LRU_WIDTH), BF16),
            jax.ShapeDtypeStruct((bsz, QK_WIDTH, tp), BF16),
            jax.ShapeDtypeStruct((bsz, tp, QK_WIDTH), BF16),
            jax.ShapeDtypeStruct((bsz, MLA_HEADS * V_DIM, tp), BF16),
        ] + state_shapes,
        scratch_shapes=[pltpu.VMEM((SUBLANES, LRU_WIDTH), F32),
                        pltpu.VMEM((1, LRU_WIDTH), F32)],
        compiler_params=pltpu.CompilerParams(
            dimension_semantics=("arbitrary", "arbitrary"), vmem_limit_bytes=VMEM_LIMIT),
        name="mix_in",
    )(h, p["mix_pre_g"], p["w_in"], p["conv_w"], p["conv_b"], p["wax"], p["b_a"], p["b_x"],
      p["lam"], p["q_g"], p["w_uq"], p["kv_g"], p["w_uk"], p["w_uv"], p["lru_out_g"], *rope,
      xprev0, hcar0)


def _group_max(s):
    m = s[0:SUBLANES]
    for r in range(1, s.shape[0] // SUBLANES):
        m = jnp.maximum(m, s[r * SUBLANES:(r + 1) * SUBLANES])
    return m


def _sublane_allmax(m):
    for shift in (1, 2, 4):
        m = jnp.maximum(m, pltpu.roll(m, shift, 0))
    return m


def _attn_kernel(has_prefix, qt_ref, k_ref, vt_ref, *rest):
    if has_prefix:
        kpre_ref, vtpre_ref = rest[:2]
        rest = rest[2:]
        spre_ref = rest[-1]
        rest = rest[:-1]
    o_ref, m_ref, l_ref, acc_ref, s0_ref, s1_ref, bm0_ref, bm1_ref = rest
    qi = pl.program_id(1)
    tq = qt_ref.shape[1]
    tk = s0_ref.shape[1]
    s_refs = (s0_ref, s1_ref)
    bm_refs = (bm0_ref, bm1_ref)
    ones_rows = 2 * SUBLANES

    def with_ones(vb):
        return jnp.concatenate([vb, jnp.ones((ones_rows, vb.shape[1]), BF16)], axis=0)

    m_ref[...] = jnp.full(m_ref.shape, NEG_BIG, F32)
    l_ref[...] = jnp.zeros(l_ref.shape, F32)
    acc_ref[...] = jnp.zeros(acc_ref.shape, F32)

    def scores(j, g, slot):
        off = pl.multiple_of(j * tk, tk)
        s = jnp.dot(k_ref[pl.ds(off, tk), g * HEAD_PAD:(g + 1) * HEAD_PAD],
                    qt_ref[g * HEAD_PAD:(g + 1) * HEAD_PAD, :], preferred_element_type=F32)
        s_refs[slot][g] = s
        bm_refs[slot][g] = _group_max(s)

    def softmax_pv(j, g, slot, masked):
        s = s_refs[slot][g]
        if masked:
            key = lax.broadcasted_iota(jnp.int32, s.shape, 0)
            qry = lax.broadcasted_iota(jnp.int32, s.shape, 1)
            s = jnp.where(key <= qry, s, NEG_BIG)
            bm = _group_max(s)
            if has_prefix:
                s_pre = spre_ref[g]
                bm = jnp.maximum(bm, _group_max(s_pre))
        else:
            bm = bm_refs[slot][g]
        m_old = m_ref[g]
        m_new = jnp.maximum(m_old, _sublane_allmax(bm))
        alpha = jnp.exp2(m_old - m_new)
        p = jnp.exp2(s - jnp.concatenate([m_new] * (tk // SUBLANES), axis=0))
        m_ref[g] = m_new
        off = pl.multiple_of(j * tk, tk)
        vb = with_ones(vt_ref[g * V_DIM:(g + 1) * V_DIM, pl.ds(off, tk)])
        pv = jnp.dot(vb, p.astype(BF16), preferred_element_type=F32)
        if masked and has_prefix:
            p_pre = jnp.exp2(s_pre - jnp.concatenate([m_new] * (N_META // SUBLANES), axis=0))
            p_pre = jnp.concatenate(
                [p_pre.astype(BF16), jnp.zeros((LANES - N_META, tq), BF16)], axis=0)
            pv = pv + jnp.dot(with_ones(vtpre_ref[g * V_DIM:(g + 1) * V_DIM, :]), p_pre,
                              preferred_element_type=F32)
        acc_ref[g] = (jnp.concatenate([alpha] * (V_DIM // SUBLANES), axis=0) * acc_ref[g]
                      + pv[:V_DIM])
        l_ref[g] = alpha * l_ref[g] + pv[V_DIM:V_DIM + SUBLANES]

    def step(j_next, slot_next, j, slot, masked=False):
        for g in range(MLA_HEADS):
            if j_next is not None:
                scores(j_next, g, slot_next)
            softmax_pv(j, g, slot, masked)

    for g in range(MLA_HEADS):
        scores(0, g, 0)
        if has_prefix:
            spre_ref[g] = jnp.dot(kpre_ref[:, g * HEAD_PAD:(g + 1) * HEAD_PAD],
                                  qt_ref[g * HEAD_PAD:(g + 1) * HEAD_PAD, :],
                                  preferred_element_type=F32)

    def pair(i, carry):
        j = 2 * i
        step(j + 1, 1, j, 0)
        step(j + 2, 0, j + 1, 1)
        return carry

    lax.fori_loop(0, qi // 2, pair, 0)

    @pl.when(qi % 2 == 0)
    def _():
        step(None, None, qi, 0, True)

    @pl.when(qi % 2 == 1)
    def _():
        step(qi, 1, qi - 1, 0)
        step(None, None, qi, 1, True)

    for g in range(MLA_HEADS):
        inv_l = 1.0 / jnp.concatenate([l_ref[g]] * (V_DIM // SUBLANES), axis=0)
        o_ref[g * V_DIM:(g + 1) * V_DIM, :] = acc_ref[g] * inv_l


def _attention(qt, k, vt, prefix=None):
    bsz, _, tp = qt.shape
    tq = tk = min(SEQ_TILE, tp)
    vdim = MLA_HEADS * V_DIM
    stat = pltpu.VMEM((MLA_HEADS, SUBLANES, tq), F32)
    has_prefix = prefix is not None
    prefix_specs = [_const_spec((N_META, QK_WIDTH)), _const_spec((vdim, LANES))] if has_prefix else []
    return pl.pallas_call(
        functools.partial(_attn_kernel, has_prefix),
        grid=(bsz, tp // tq),
        in_specs=[
            pl.BlockSpec((None, QK_WIDTH, tq), lambda b, i: (b, 0, i)),
            pl.BlockSpec((None, tp, QK_WIDTH), lambda b, i: (b, 0, 0)),
            pl.BlockSpec((None, vdim, tp), lambda b, i: (b, 0, 0)),
        ] + prefix_specs,
        out_specs=pl.BlockSpec((None, vdim, tq), lambda b, i: (b, 0, i)),
        out_shape=jax.ShapeDtypeStruct((bsz, vdim, tp), F32),
        scratch_shapes=[stat, stat, pltpu.VMEM((MLA_HEADS, V_DIM, tq), F32),
                        pltpu.VMEM((MLA_HEADS, tk, tq), F32),
                        pltpu.VMEM((MLA_HEADS, tk, tq), F32), stat, stat]
        + ([pltpu.VMEM((MLA_HEADS, N_META, tq), F32)] if has_prefix else []),
        compiler_params=pltpu.CompilerParams(
            dimension_semantics=("arbitrary", "arbitrary"), vmem_limit_bytes=VMEM_LIMIT),
        name="attention",
    )(qt, k, vt, *(prefix or ()))


def _mix_out_kernel(h_ref, ylru_ref, ot_ref, gmla_ref, wout_ref, gpost_ref, o_ref):
    ot = ot_ref[...]
    ms = jnp.mean(ot * ot, axis=0, keepdims=True)
    on = (ot * lax.rsqrt(ms + EPS) * gmla_ref[...]).T.astype(BF16)
    y = jnp.dot(ylru_ref[...], wout_ref[:LRU_WIDTH, :], preferred_element_type=F32)
    y = y + jnp.dot(on, wout_ref[LRU_WIDTH:, :], preferred_element_type=F32)
    o_ref[...] = h_ref[...] + _rms(y, gpost_ref[...])


def _mix_out(h, ylru, ot, layer, p):
    bsz, tp, _ = h.shape
    tt = min(SEQ_TILE, tp)
    row_spec = pl.BlockSpec((None, tt, D_MODEL), lambda b, t: (b, t, 0))
    return pl.pallas_call(
        _mix_out_kernel,
        grid=(bsz, tp // tt),
        in_specs=[
            row_spec,
            pl.BlockSpec((None, tt, LRU_WIDTH), lambda b, t: (b, t, 0)),
            pl.BlockSpec((None, MLA_HEADS * V_DIM, tt), lambda b, t: (b, 0, t)),
            _const_spec((MLA_HEADS * V_DIM, 1), layer),
            _const_spec((D_MODEL, D_MODEL), layer),
            _const_spec((1, D_MODEL), layer),
        ],
        out_specs=row_spec,
        out_shape=jax.ShapeDtypeStruct(h.shape, F32),
        compiler_params=pltpu.CompilerParams(
            dimension_semantics=("arbitrary", "arbitrary"), vmem_limit_bytes=VMEM_LIMIT),
        name="mix_out",
    )(h, ylru, ot, p["mla_out_g"], p["w_out"], p["mix_post_g"])


def _rope_tables(tp):
    pos = jnp.arange(tp, dtype=F32)
    inv_freq = 1.0 / (ROPE_THETA ** (jnp.arange(0, QK_ROPE, 2, dtype=F32) / QK_ROPE))
    ang = pos[:, None] * inv_freq[None, :]
    cos, sin = jnp.cos(ang), jnp.sin(ang)
    half = QK_ROPE // 2
    zeros = lambda n: jnp.zeros((tp, n), F32)
    pad = HEAD_PAD - QK_NOPE - QK_ROPE
    c = jnp.concatenate([jnp.ones((tp, QK_NOPE), F32), cos, cos, zeros(pad)], axis=1)
    sa = jnp.concatenate([zeros(QK_NOPE + half), sin, zeros(pad)], axis=1)
    sb = jnp.concatenate([zeros(QK_NOPE), -sin, zeros(half + pad)], axis=1)
    return c, sa, sb


def _block_diag(w):
    depth = w.shape[0]
    per = LRU_HEADS // 2
    w5 = w.reshape(depth, 2, per, LRU_HEAD_DIM, LRU_HEAD_DIM)
    bd = jnp.einsum("dnhij,hg->dnhigj", w5, jnp.eye(per, dtype=w.dtype))
    return bd.reshape(depth, 2, per * LRU_HEAD_DIM, per * LRU_HEAD_DIM)


def _row(g):
    return g.reshape(g.shape[0], 1, g.shape[1])


@jax.jit
def kernel(x, meta_tokens, ffn1_pre_g, ffn1_w_gate, ffn1_w_up, ffn1_w_down, ffn1_post_g, mix_pre_g, w_in, lru_conv_w, lru_conv_b, lru_w_a, lru_b_a, lru_w_x, lru_b_x, lru_lambda, mla_q_norm_g, mla_w_uq, mla_kv_norm_g, mla_w_ukv, lru_out_g, mla_out_g, w_out, mix_post_g, ffn2_pre_g, ffn2_w_gate, ffn2_w_up, ffn2_w_down, ffn2_post_g):
    bsz, seq, _ = x.shape
    depth = w_in.shape[0]
    assert seq % SEQ_TILE == 0
    h = x
    hm = jnp.concatenate([meta_tokens.astype(x.dtype),
                          jnp.zeros((META_ROWS - N_META, D_MODEL), x.dtype)], axis=0)[None]
    rope_all = _rope_tables(N_META + max(seq, META_ROWS))
    rope_meta = tuple(t[:META_ROWS] for t in rope_all)
    rope_main = tuple(t[N_META:N_META + seq] for t in rope_all)
    zero_hist = jnp.zeros((SUBLANES, LRU_WIDTH), F32)
    zero_state = jnp.zeros((1, LRU_WIDTH), F32)

    split = 2 * LRU_WIDTH + Q_LORA + KV_LORA
    zcols = lambda n: jnp.zeros((depth, D_MODEL, n), w_in.dtype)
    w_in_p = jnp.concatenate(
        [w_in[..., :split], zcols(QK_NOPE), w_in[..., split:],
         zcols(HEAD_PAD - QK_NOPE - QK_ROPE)], axis=-1).astype(BF16)
    w_uq_p = jnp.pad(mla_w_uq.reshape(depth, Q_LORA, MLA_HEADS, QK_NOPE + QK_ROPE),
                     ((0, 0), (0, 0), (0, 0), (0, HEAD_PAD - QK_NOPE - QK_ROPE)))
    w_ukv4 = mla_w_ukv.reshape(depth, KV_LORA, MLA_HEADS, QK_NOPE + V_DIM)
    w_uk_p = jnp.pad(w_ukv4[..., :QK_NOPE], ((0, 0), (0, 0), (0, 0), (0, HEAD_PAD - QK_NOPE)))
    mixp = {
        "mix_pre_g": _row(mix_pre_g),
        "w_in": w_in_p,
        "conv_w": lru_conv_w,
        "conv_b": _row(lru_conv_b),
        "wax": jnp.concatenate([_block_diag(lru_w_a), _block_diag(lru_w_x)], axis=-1).astype(BF16),
        "b_a": _row(lru_b_a),
        "b_x": _row(lru_b_x),
        "lam": _row(lru_lambda),
        "q_g": _row(mla_q_norm_g),
        "w_uq": w_uq_p.reshape(depth, Q_LORA, QK_WIDTH).astype(BF16),
        "kv_g": _row(mla_kv_norm_g),
        "w_uk": w_uk_p.reshape(depth, KV_LORA, QK_WIDTH).astype(BF16),
        "w_uv": w_ukv4[..., QK_NOPE:].reshape(depth, KV_LORA, MLA_HEADS * V_DIM).astype(BF16),
        "lru_out_g": _row(lru_out_g),
        "mla_out_g": mla_out_g.reshape(depth, MLA_HEADS * V_DIM, 1),
        "w_out": w_out.astype(BF16),
        "mix_post_g": _row(mix_post_g),
    }
    ffn1 = (_row(ffn1_pre_g), ffn1_w_gate.astype(BF16), ffn1_w_up.astype(BF16),
            ffn1_w_down.astype(BF16), _row(ffn1_post_g))
    ffn2 = (_row(ffn2_pre_g), ffn2_w_gate.astype(BF16), ffn2_w_up.astype(BF16),
            ffn2_w_down.astype(BF16), _row(ffn2_post_g))

    lane = lax.broadcasted_iota(jnp.int32, (MLA_HEADS * V_DIM, META_ROWS), 1)
    for layer in range(depth):
        hm = _ffn(hm, layer, *ffn1)
        h = _ffn(h, layer, *ffn1)
        ylru_m, qt_m, k_m, vt_m, xr_m, hs_m = _mix_in(
            hm, layer, mixp, rope_meta, zero_hist, zero_state, emit_state=True)
        ylru, qt, k, vt = _mix_in(h, layer, mixp, rope_main,
                                  xr_m[0, N_META - SUBLANES:N_META], hs_m[0, N_META - 1:N_META])
        prefix = (k_m[0, :N_META], jnp.where(lane < N_META, vt_m[0], 0).astype(BF16))
        h = _mix_out(h, ylru, _attention(qt, k, vt, prefix), layer, mixp)
        h = _ffn(h, layer, *ffn2)
        if layer + 1 < depth:
            hm = _mix_out(hm, ylru_m, _attention(qt_m, k_m, vt_m), layer, mixp)
            hm = _ffn(hm, layer, *ffn2)
    return h
```

```python
import functools

import jax
import jax.numpy as jnp
from jax import lax
from jax.experimental import pallas as pl
from jax.experimental.pallas import tpu as pltpu

D_MODEL = 1024
N_META = 16
LRU_WIDTH = 512
LRU_HEADS = 8
LRU_HEAD_DIM = 64
CONV_WIDTH = 4
LRU_C = 8.0
MLA_HEADS = 8
QK_NOPE = 64
QK_ROPE = 32
V_DIM = 64
Q_LORA = 384
KV_LORA = 256
D_FF = 2816
ROPE_THETA = 10000.0
EPS = 1e-6

LANES = 128
SUBLANES = 8
HEAD_PAD = LANES
QK_WIDTH = MLA_HEADS * HEAD_PAD
IN_COLS_PAD = 2 * LRU_WIDTH + Q_LORA + KV_LORA + HEAD_PAD
SEQ_TILE = 256
META_ROWS = LANES
FFN_ROWS = 512
FF_CHUNK = 256
NEG_BIG = -1e30
LOG2_E = 1.4426950408889634
VMEM_LIMIT = 52 * 1024 * 1024

F32 = jnp.float32
BF16 = jnp.bfloat16


def _rms(x, g):
    return x * lax.rsqrt(jnp.mean(x * x, axis=-1, keepdims=True) + EPS) * g


def _const_spec(shape, layer=None):
    if layer is None:
        idx = lambda *_: (0,) * len(shape)
        return pl.BlockSpec(shape, idx, pipeline_mode=pl.Buffered(1))
    idx = lambda *_: (layer,) + (0,) * len(shape)
    return pl.BlockSpec((None,) + shape, idx, pipeline_mode=pl.Buffered(1))


def _mix_out_rows(h, ylru, ot, gmla, wout_ref, gpost):
    ms = jnp.mean(ot * ot, axis=0, keepdims=True)
    on = (ot * lax.rsqrt(ms + EPS) * gmla).T.astype(BF16)
    y = jnp.dot(ylru, wout_ref[:LRU_WIDTH, :], preferred_element_type=F32)
    y = y + jnp.dot(on, wout_ref[LRU_WIDTH:, :], preferred_element_type=F32)
    return h + _rms(y, gpost)


def _ffn_kernel(after_mixer, h_ref, *refs):
    x = h_ref[...]
    if after_mixer:
        ylru_ref, ot_ref, gmla_ref, wout_ref, gmix_ref = refs[:5]
        refs = refs[5:]
        x = _mix_out_rows(x, ylru_ref[...], ot_ref[...], gmla_ref[...], wout_ref, gmix_ref[...])
    gpre_ref, wg_ref, wu_ref, wd_ref, gpost_ref, o_ref, a_ref = refs
    u = _rms(x, gpre_ref[...]).astype(BF16)
    for c in range(D_FF // FF_CHUNK):
        sl = slice(c * FF_CHUNK, (c + 1) * FF_CHUNK)
        gate = jnp.dot(u, wg_ref[:, sl], preferred_element_type=F32)
        up = jnp.dot(u, wu_ref[:, sl], preferred_element_type=F32)
        a_ref[:, sl] = (gate * jax.nn.sigmoid(gate) * up).astype(BF16)
    f = jnp.dot(a_ref[...], wd_ref[...], preferred_element_type=F32)
    o_ref[...] = x + 0.5 * _rms(f, gpost_ref[...])


def _ffn(h, layer, gpre, wg, wu, wd, gpost, mixer=None):
    bsz, tp, _ = h.shape
    tile = min(FFN_ROWS, tp)
    assert tp % tile == 0
    row_spec = pl.BlockSpec((None, tile, D_MODEL), lambda b, t: (b, t, 0))
    mixer_specs, mixer_args = [], []
    if mixer is not None:
        ylru, ot, p = mixer
        vdim = MLA_HEADS * V_DIM
        mixer_specs = [
            pl.BlockSpec((None, tile, LRU_WIDTH), lambda b, t: (b, t, 0)),
            pl.BlockSpec((None, vdim, tile), lambda b, t: (b, 0, t)),
            _const_spec((vdim, 1), layer),
            _const_spec((D_MODEL, D_MODEL), layer),
            _const_spec((1, D_MODEL), layer),
        ]
        mixer_args = [ylru, ot, p["mla_out_g"], p["w_out"], p["mix_post_g"]]
    return pl.pallas_call(
        functools.partial(_ffn_kernel, mixer is not None),
        grid=(bsz, tp // tile),
        in_specs=[row_spec] + mixer_specs + [
            _const_spec((1, D_MODEL), layer),
            _const_spec((D_MODEL, D_FF), layer),
            _const_spec((D_MODEL, D_FF), layer),
            _const_spec((D_FF, D_MODEL), layer),
            _const_spec((1, D_MODEL), layer),
        ],
        out_specs=row_spec,
        out_shape=jax.ShapeDtypeStruct(h.shape, F32),
        scratch_shapes=[pltpu.VMEM((tile, D_FF), BF16)],
        compiler_params=pltpu.CompilerParams(
            dimension_semantics=("arbitrary", "arbitrary"), vmem_limit_bytes=VMEM_LIMIT),
        name="ffn_mix" if mixer is not None else "ffn",
    )(h, *mixer_args, gpre, wg, wu, wd, gpost)


def _rope(x, c, sa, sb):
    n = x.shape[-1]
    half = QK_ROPE // 2
    return x * c + pltpu.roll(x, half, 1) * sa + pltpu.roll(x, n - half, 1) * sb


def _mix_in_kernel(emit_state, h_ref, g_ref, win_ref, cw_ref, cb_ref, wax_ref, ba_ref, bx_ref,
                   lam_ref, gq_ref, wuq_ref, gkv_ref, wuk_ref, wuv_ref, glru_ref,
                   rc_ref, rsa_ref, rsb_ref, xprev0_ref, hcar0_ref,
                   ylru_ref, qt_ref, k_ref, vt_ref, *rest):
    if emit_state:
        xr_out_ref, hs_out_ref, xprev_ref, hcar_ref = rest
    else:
        xprev_ref, hcar_ref = rest
    tt = h_ref.shape[0]

    @pl.when(pl.program_id(1) == 0)
    def _():
        xprev_ref[...] = xprev0_ref[...]
        hcar_ref[...] = hcar0_ref[...]

    u = _rms(h_ref[...], g_ref[...]).astype(BF16)
    z = jnp.dot(u, win_ref[...], preferred_element_type=F32)
    xr = z[:, 0:LRU_WIDTH]
    gr = z[:, LRU_WIDTH:2 * LRU_WIDTH]
    cq = z[:, 2 * LRU_WIDTH:2 * LRU_WIDTH + Q_LORA]
    ckv = z[:, 2 * LRU_WIDTH + Q_LORA:2 * LRU_WIDTH + Q_LORA + KV_LORA]
    krt = z[:, 2 * LRU_WIDTH + Q_LORA + KV_LORA:]

    xe = jnp.concatenate([xprev_ref[...], xr], axis=0)
    xc = xr * cw_ref[CONV_WIDTH - 1:CONV_WIDTH, :] + cb_ref[...]
    for j in range(1, CONV_WIDTH):
        w_j = cw_ref[CONV_WIDTH - 1 - j:CONV_WIDTH - j, :]
        xc = xc + pltpu.roll(xe, j, 0)[SUBLANES:] * w_j
    xprev_ref[...] = xr[tt - SUBLANES:]

    xcb = xc.astype(BF16)
    half = LRU_WIDTH // 2
    ri0 = jnp.dot(xcb[:, :half], wax_ref[0], preferred_element_type=F32)
    ri1 = jnp.dot(xcb[:, half:], wax_ref[1], preferred_element_type=F32)
    r = jax.nn.sigmoid(jnp.concatenate([ri0[:, :half], ri1[:, :half]], axis=1) + ba_ref[...])
    i = jax.nn.sigmoid(jnp.concatenate([ri0[:, half:], ri1[:, half:]], axis=1) + bx_ref[...])
    nl = -lam_ref[...]
    softplus = jnp.maximum(nl, 0.0) + jnp.log1p(jnp.exp(-jnp.abs(nl)))
    log_a = (-LRU_C) * r * softplus
    a = jnp.exp(log_a)
    b = jnp.sqrt(1.0 - a * a) * (i * xc)

    row = lax.broadcasted_iota(jnp.int32, a.shape, 0) % SUBLANES
    s = 1
    while s < SUBLANES:
        valid = row >= s
        b = jnp.where(valid, a * pltpu.roll(b, s, 0) + b, b)
        a = jnp.where(valid, a * pltpu.roll(a, s, 0), a)
        s *= 2
    carry = hcar_ref[...]
    groups = []
    for gidx in range(tt // SUBLANES):
        sl = slice(gidx * SUBLANES, (gidx + 1) * SUBLANES)
        hg = b[sl] + a[sl] * carry
        carry = hg[SUBLANES - 1:SUBLANES]
        groups.append(hg)
    hcar_ref[...] = carry
    hs = jnp.concatenate(groups, axis=0)
    if emit_state:
        xr_out_ref[...] = xr
        hs_out_ref[...] = hs

    cg = 0.7978845608028654
    gelu = 0.5 * gr * (1.0 + jnp.tanh(cg * (gr + 0.044715 * (gr * gr * gr))))
    ylru_ref[...] = _rms(hs * gelu, glru_ref[...]).astype(BF16)

    rc = rc_ref[...]
    rsa = rsa_ref[...]
    rsb = rsb_ref[...]
    tile_heads = lambda t: jnp.concatenate([t] * MLA_HEADS, axis=1)
    cqn = _rms(cq, gq_ref[...]).astype(BF16)
    q = jnp.dot(cqn, wuq_ref[...], preferred_element_type=F32)
    q = _rope(q, tile_heads(rc), tile_heads(rsa), tile_heads(rsb))
    q = q * (LOG2_E * (QK_NOPE + QK_ROPE) ** -0.5)
    qt_ref[...] = q.T.astype(BF16)

    ckvn = _rms(ckv, gkv_ref[...]).astype(BF16)
    kn = jnp.dot(ckvn, wuk_ref[...], preferred_element_type=F32)
    kr = _rope(krt, rc, rsa, rsb)
    k_ref[...] = (kn + tile_heads(kr)).astype(BF16)
    v = jnp.dot(ckvn, wuv_ref[...], preferred_element_type=F32)
    vt_ref[...] = v.T.astype(BF16)


def _mix_in(h, layer, p, rope, xprev0, hcar0, emit_state=False):
    bsz, tp, _ = h.shape
    tt = min(SEQ_TILE, tp)
    nt = tp // tt
    tbl_spec = pl.BlockSpec((tt, HEAD_PAD), lambda b, t: (t, 0))
    seq_spec = pl.BlockSpec((None, tt, LRU_WIDTH), lambda b, t: (b, t, 0))
    state_specs = [seq_spec, seq_spec] if emit_state else []
    state_shapes = [jax.ShapeDtypeStruct((bsz, tp, LRU_WIDTH), F32)] * 2 if emit_state else []
    return pl.pallas_call(
        functools.partial(_mix_in_kernel, emit_state),
        grid=(bsz, nt),
        in_specs=[
            pl.BlockSpec((None, tt, D_MODEL), lambda b, t: (b, t, 0)),
            _const_spec((1, D_MODEL), layer),
            _const_spec((D_MODEL, IN_COLS_PAD), layer),
            _const_spec((CONV_WIDTH, LRU_WIDTH), layer),
            _const_spec((1, LRU_WIDTH), layer),
            _const_spec((2, LRU_WIDTH // 2, LRU_WIDTH), layer),
            _const_spec((1, LRU_WIDTH), layer),
            _const_spec((1, LRU_WIDTH), layer),
            _const_spec((1, LRU_WIDTH), layer),
            _const_spec((1, Q_LORA), layer),
            _const_spec((Q_LORA, QK_WIDTH), layer),
            _const_spec((1, KV_LORA), layer),
            _const_spec((KV_LORA, QK_WIDTH), layer),
            _const_spec((KV_LORA, MLA_HEADS * V_DIM), layer),
            _const_spec((1, LRU_WIDTH), layer),
            tbl_spec, tbl_spec, tbl_spec,
            _const_spec((SUBLANES, LRU_WIDTH)),
            _const_spec((1, LRU_WIDTH)),
        ],
        out_specs=[
            seq_spec,
            pl.BlockSpec((None, QK_WIDTH, tt), lambda b, t: (b, 0, t)),
            pl.BlockSpec((None, tt, QK_WIDTH), lambda b, t: (b, t, 0)),
            pl.BlockSpec((None, MLA_HEADS * V_DIM, tt), lambda b, t: (b, 0, t)),
        ] + state_specs,
        out_shape=[
            jax.ShapeDtypeStruct((bsz, tp, LRU_WIDTH), BF16),
            jax.ShapeDtypeStruct((bsz, QK_WIDTH, tp), BF16),
            jax.ShapeDtypeStruct((bsz, tp, QK_WIDTH), BF16),
            jax.ShapeDtypeStruct((bsz, MLA_HEADS * V_DIM, tp), BF16),
        ] + state_shapes,
        scratch_shapes=[pltpu.VMEM((SUBLANES, LRU_WIDTH), F32),
                        pltpu.VMEM((1, LRU_WIDTH), F32)],
        compiler_params=pltpu.CompilerParams(
            dimension_semantics=("arbitrary", "arbitrary"), vmem_limit_bytes=VMEM_LIMIT),
        name="mix_in",
    )(h, p["mix_pre_g"], p["w_in"], p["conv_w"], p["conv_b"], p["wax"], p["b_a"], p["b_x"],
      p["lam"], p["q_g"], p["w_uq"], p["kv_g"], p["w_uk"], p["w_uv"], p["lru_out_g"], *rope,
      xprev0, hcar0)


def _group_max(s):
    m = s[0:SUBLANES]
    for r in range(1, s.shape[0] // SUBLANES):
        m = jnp.maximum(m, s[r * SUBLANES:(r + 1) * SUBLANES])
    return m


def _sublane_allmax(m):
    for shift in (1, 2, 4):
        m = jnp.maximum(m, pltpu.roll(m, shift, 0))
    return m


def _attn_kernel(has_prefix, qt_ref, k_ref, vt_ref, *rest):
    if has_prefix:
        kpre_ref, vtpre_ref = rest[:2]
        rest = rest[2:]
        spre_ref = rest[-1]
        rest = rest[:-1]
    o_ref, m_ref, l_ref, acc_ref, s0_ref, s1_ref, bm0_ref, bm1_ref = rest
    qi = pl.program_id(1)
    tq = qt_ref.shape[1]
    tk = s0_ref.shape[1]
    s_refs = (s0_ref, s1_ref)
    bm_refs = (bm0_ref, bm1_ref)
    ones_rows = 2 * SUBLANES

    def with_ones(vb):
        return jnp.concatenate([vb, jnp.ones((ones_rows, vb.shape[1]), BF16)], axis=0)

    m_ref[...] = jnp.full(m_ref.shape, NEG_BIG, F32)
    l_ref[...] = jnp.zeros(l_ref.shape, F32)
    acc_ref[...] = jnp.zeros(acc_ref.shape, F32)

    def scores(j, g, slot):
        off = pl.multiple_of(j * tk, tk)
        s = jnp.dot(k_ref[pl.ds(off, tk), g * HEAD_PAD:(g + 1) * HEAD_PAD],
                    qt_ref[g * HEAD_PAD:(g + 1) * HEAD_PAD, :], preferred_element_type=F32)
        s_refs[slot][g] = s
        bm_refs[slot][g] = _group_max(s)

    def softmax_pv(j, g, slot, masked):
        s = s_refs[slot][g]
        if masked:
            key = lax.broadcasted_iota(jnp.int32, s.shape, 0)
            qry = lax.broadcasted_iota(jnp.int32, s.shape, 1)
            s = jnp.where(key <= qry, s, NEG_BIG)
            bm = _group_max(s)
            if has_prefix:
                s_pre = spre_ref[g]
                bm = jnp.maximum(bm, _group_max(s_pre))
        else:
            bm = bm_refs[slot][g]
        m_old = m_ref[g]
        m_new = jnp.maximum(m_old, _sublane_allmax(bm))
        alpha = jnp.exp2(m_old - m_new)
        p = jnp.exp2(s - jnp.concatenate([m_new] * (tk // SUBLANES), axis=0))
        m_ref[g] = m_new
        off = pl.multiple_of(j * tk, tk)
        vb = with_ones(vt_ref[g * V_DIM:(g + 1) * V_DIM, pl.ds(off, tk)])
        pv = jnp.dot(vb, p.astype(BF16), preferred_element_type=F32)
        if masked and has_prefix:
            p_pre = jnp.exp2(s_pre - jnp.concatenate([m_new] * (N_META // SUBLANES), axis=0))
            p_pre = jnp.concatenate(
                [p_pre.astype(BF16), jnp.zeros((LANES - N_META, tq), BF16)], axis=0)
            pv = pv + jnp.dot(with_ones(vtpre_ref[g * V_DIM:(g + 1) * V_DIM, :]), p_pre,
                              preferred_element_type=F32)
        acc_ref[g] = (jnp.concatenate([alpha] * (V_DIM // SUBLANES), axis=0) * acc_ref[g]
                      + pv[:V_DIM])
        l_ref[g] = alpha * l_ref[g] + pv[V_DIM:V_DIM + SUBLANES]

    def step(j_next, slot_next, j, slot, masked=False):
        for g in range(MLA_HEADS):
            if j_next is not None:
                scores(j_next, g, slot_next)
            softmax_pv(j, g, slot, masked)

    for g in range(MLA_HEADS):
        scores(0, g, 0)
        if has_prefix:
            spre_ref[g] = jnp.dot(kpre_ref[:, g * HEAD_PAD:(g + 1) * HEAD_PAD],
                                  qt_ref[g * HEAD_PAD:(g + 1) * HEAD_PAD, :],
                                  preferred_element_type=F32)

    def pair(i, carry):
        j = 2 * i
        step(j + 1, 1, j, 0)
        step(j + 2, 0, j + 1, 1)
        return carry

    lax.fori_loop(0, qi // 2, pair, 0)

    @pl.when(qi % 2 == 0)
    def _():
        step(None, None, qi, 0, True)

    @pl.when(qi % 2 == 1)
    def _():
        step(qi, 1, qi - 1, 0)
        step(None, None, qi, 1, True)

    for g in range(MLA_HEADS):
        inv_l = 1.0 / jnp.concatenate([l_ref[g]] * (V_DIM // SUBLANES), axis=0)
        o_ref[g * V_DIM:(g + 1) * V_DIM, :] = acc_ref[g] * inv_l


def _attention(qt, k, vt, prefix=None):
    bsz, _, tp = qt.shape
    tq = tk = min(SEQ_TILE, tp)
    vdim = MLA_HEADS * V_DIM
    stat = pltpu.VMEM((MLA_HEADS, SUBLANES, tq), F32)
    has_prefix = prefix is not None
    prefix_specs = [_const_spec((N_META, QK_WIDTH)), _const_spec((vdim, LANES))] if has_prefix else []
    return pl.pallas_call(
        functools.partial(_attn_kernel, has_prefix),
        grid=(bsz, tp // tq),
        in_specs=[
            pl.BlockSpec((None, QK_WIDTH, tq), lambda b, i: (b, 0, i)),
            pl.BlockSpec((None, tp, QK_WIDTH), lambda b, i: (b, 0, 0)),
            pl.BlockSpec((None, vdim, tp), lambda b, i: (b, 0, 0)),
        ] + prefix_specs,
        out_specs=pl.BlockSpec((None, vdim, tq), lambda b, i: (b, 0, i)),
        out_shape=jax.ShapeDtypeStruct((bsz, vdim, tp), F32),
        scratch_shapes=[stat, stat, pltpu.VMEM((MLA_HEADS, V_DIM, tq), F32),
                        pltpu.VMEM((MLA_HEADS, tk, tq), F32),
                        pltpu.VMEM((MLA_HEADS, tk, tq), F32), stat, stat]
        + ([pltpu.VMEM((MLA_HEADS, N_META, tq), F32)] if has_prefix else []),
        compiler_params=pltpu.CompilerParams(
            dimension_semantics=("arbitrary", "arbitrary"), vmem_limit_bytes=VMEM_LIMIT),
        name="attention",
    )(qt, k, vt, *(prefix or ()))


def _rope_tables(tp):
    pos = jnp.arange(tp, dtype=F32)
    inv_freq = 1.0 / (ROPE_THETA ** (jnp.arange(0, QK_ROPE, 2, dtype=F32) / QK_ROPE))
    ang = pos[:, None] * inv_freq[None, :]
    cos, sin = jnp.cos(ang), jnp.sin(ang)
    half = QK_ROPE // 2
    zeros = lambda n: jnp.zeros((tp, n), F32)
    pad = HEAD_PAD - QK_NOPE - QK_ROPE
    c = jnp.concatenate([jnp.ones((tp, QK_NOPE), F32), cos, cos, zeros(pad)], axis=1)
    sa = jnp.concatenate([zeros(QK_NOPE + half), sin, zeros(pad)], axis=1)
    sb = jnp.concatenate([zeros(QK_NOPE), -sin, zeros(half + pad)], axis=1)
    return c, sa, sb


def _block_diag(w):
    depth = w.shape[0]
    per = LRU_HEADS // 2
    w5 = w.reshape(depth, 2, per, LRU_HEAD_DIM, LRU_HEAD_DIM)
    bd = jnp.einsum("dnhij,hg->dnhigj", w5, jnp.eye(per, dtype=w.dtype))
    return bd.reshape(depth, 2, per * LRU_HEAD_DIM, per * LRU_HEAD_DIM)


def _row(g):
    return g.reshape(g.shape[0], 1, g.shape[1])


@jax.jit
def kernel(x, meta_tokens, ffn1_pre_g, ffn1_w_gate, ffn1_w_up, ffn1_w_down, ffn1_post_g, mix_pre_g, w_in, lru_conv_w, lru_conv_b, lru_w_a, lru_b_a, lru_w_x, lru_b_x, lru_lambda, mla_q_norm_g, mla_w_uq, mla_kv_norm_g, mla_w_ukv, lru_out_g, mla_out_g, w_out, mix_post_g, ffn2_pre_g, ffn2_w_gate, ffn2_w_up, ffn2_w_down, ffn2_post_g):
    bsz, seq, _ = x.shape
    depth = w_in.shape[0]
    assert seq % SEQ_TILE == 0
    h = x
    hm = jnp.concatenate([meta_tokens.astype(x.dtype),
                          jnp.zeros((META_ROWS - N_META, D_MODEL), x.dtype)], axis=0)[None]
    rope_all = _rope_tables(N_META + max(seq, META_ROWS))
    rope_meta = tuple(t[:META_ROWS] for t in rope_all)
    rope_main = tuple(t[N_META:N_META + seq] for t in rope_all)
    zero_hist = jnp.zeros((SUBLANES, LRU_WIDTH), F32)
    zero_state = jnp.zeros((1, LRU_WIDTH), F32)

    split = 2 * LRU_WIDTH + Q_LORA + KV_LORA
    zcols = lambda n: jnp.zeros((depth, D_MODEL, n), w_in.dtype)
    w_in_p = jnp.concatenate(
        [w_in[..., :split], zcols(QK_NOPE), w_in[..., split:],
         zcols(HEAD_PAD - QK_NOPE - QK_ROPE)], axis=-1).astype(BF16)
    w_uq_p = jnp.pad(mla_w_uq.reshape(depth, Q_LORA, MLA_HEADS, QK_NOPE + QK_ROPE),
                     ((0, 0), (0, 0), (0, 0), (0, HEAD_PAD - QK_NOPE - QK_ROPE)))
    w_ukv4 = mla_w_ukv.reshape(depth, KV_LORA, MLA_HEADS, QK_NOPE + V_DIM)
    w_uk_p = jnp.pad(w_ukv4[..., :QK_NOPE], ((0, 0), (0, 0), (0, 0), (0, HEAD_PAD - QK_NOPE)))
    mixp = {
        "mix_pre_g": _row(mix_pre_g),
        "w_in": w_in_p,
        "conv_w": lru_conv_w,
        "conv_b": _row(lru_conv_b),
        "wax": jnp.concatenate([_block_diag(lru_w_a), _block_diag(lru_w_x)], axis=-1).astype(BF16),
        "b_a": _row(lru_b_a),
        "b_x": _row(lru_b_x),
        "lam": _row(lru_lambda),
        "q_g": _row(mla_q_norm_g),
        "w_uq": w_uq_p.reshape(depth, Q_LORA, QK_WIDTH).astype(BF16),
        "kv_g": _row(mla_kv_norm_g),
        "w_uk": w_uk_p.reshape(depth, KV_LORA, QK_WIDTH).astype(BF16),
        "w_uv": w_ukv4[..., QK_NOPE:].reshape(depth, KV_LORA, MLA_HEADS * V_DIM).astype(BF16),
        "lru_out_g": _row(lru_out_g),
        "mla_out_g": mla_out_g.reshape(depth, MLA_HEADS * V_DIM, 1),
        "w_out": w_out.astype(BF16),
        "mix_post_g": _row(mix_post_g),
    }
    ffn1 = (_row(ffn1_pre_g), ffn1_w_gate.astype(BF16), ffn1_w_up.astype(BF16),
            ffn1_w_down.astype(BF16), _row(ffn1_post_g))
    ffn2 = (_row(ffn2_pre_g), ffn2_w_gate.astype(BF16), ffn2_w_up.astype(BF16),
            ffn2_w_down.astype(BF16), _row(ffn2_post_g))

    lane = lax.broadcasted_iota(jnp.int32, (MLA_HEADS * V_DIM, META_ROWS), 1)
    for layer in range(depth):
        hm = _ffn(hm, layer, *ffn1)
        h = _ffn(h, layer, *ffn1)
        ylru_m, qt_m, k_m, vt_m, xr_m, hs_m = _mix_in(
            hm, layer, mixp, rope_meta, zero_hist, zero_state, emit_state=True)
        ylru, qt, k, vt = _mix_in(h, layer, mixp, rope_main,
                                  xr_m[0, N_META - SUBLANES:N_META], hs_m[0, N_META - 1:N_META])
        prefix = (k_m[0, :N_META], jnp.where(lane < N_META, vt_m[0], 0).astype(BF16))
        h = _ffn(h, layer, *ffn2, mixer=(ylru, _attention(qt, k, vt, prefix), mixp))
        if layer + 1 < depth:
            hm = _ffn(hm, layer, *ffn2, mixer=(ylru_m, _attention(qt_m, k_m, vt_m), mixp))
    return h
```

```python
import functools

import jax
import jax.numpy as jnp
from jax import lax
from jax.experimental import pallas as pl
from jax.experimental.pallas import tpu as pltpu

D_MODEL = 1024
N_META = 16
LRU_WIDTH = 512
LRU_HEADS = 8
LRU_HEAD_DIM = 64
CONV_WIDTH = 4
LRU_C = 8.0
MLA_HEADS = 8
QK_NOPE = 64
QK_ROPE = 32
V_DIM = 64
Q_LORA = 384
KV_LORA = 256
D_FF = 2816
ROPE_THETA = 10000.0
EPS = 1e-6

LANES = 128
SUBLANES = 8
HEAD_PAD = LANES
QK_WIDTH = MLA_HEADS * HEAD_PAD
Q_NOPE_ALL = MLA_HEADS * QK_NOPE
Q_HALF_ALL = MLA_HEADS * QK_ROPE // 2
QT_ROWS = Q_NOPE_ALL + 2 * Q_HALF_ALL
IN_COLS_PAD = 2 * LRU_WIDTH + Q_LORA + KV_LORA + HEAD_PAD
SEQ_TILE = 256
MIX_TILE = 1024
MIX_SUB = 128
META_ROWS = LANES
FFN_ROWS = 512
FF_CHUNK = 256
NEG_BIG = -1e30
TINY = 1e-30
LOG2_E = 1.4426950408889634
VMEM_LIMIT = 52 * 1024 * 1024

F32 = jnp.float32
BF16 = jnp.bfloat16


def _rms(x, g):
    return x * lax.rsqrt(jnp.mean(x * x, axis=-1, keepdims=True) + EPS) * g


def _const_spec(shape, layer=None):
    if layer is None:
        idx = lambda *_: (0,) * len(shape)
        return pl.BlockSpec(shape, idx, pipeline_mode=pl.Buffered(1))
    idx = lambda *_: (layer,) + (0,) * len(shape)
    return pl.BlockSpec((None,) + shape, idx, pipeline_mode=pl.Buffered(1))


def _mix_out_rows(h, ylru, ot, gmla, wout_ref, gpost):
    ms = jnp.mean(ot * ot, axis=0, keepdims=True)
    on = (ot * lax.rsqrt(ms + EPS) * gmla).T.astype(BF16)
    y = jnp.dot(ylru, wout_ref[:LRU_WIDTH, :], preferred_element_type=F32)
    y = y + jnp.dot(on, wout_ref[LRU_WIDTH:, :], preferred_element_type=F32)
    return h + _rms(y, gpost)


def _ffn_kernel(after_mixer, h_ref, *refs):
    x = h_ref[...]
    if after_mixer:
        ylru_ref, ot_ref, gmla_ref, wout_ref, gmix_ref = refs[:5]
        refs = refs[5:]
        x = _mix_out_rows(x, ylru_ref[...], ot_ref[...], gmla_ref[...], wout_ref, gmix_ref[...])
    gpre_ref, wg_ref, wu_ref, wd_ref, gpost_ref, o_ref, a_ref = refs
    u = _rms(x, gpre_ref[...]).astype(BF16)
    for c in range(D_FF // FF_CHUNK):
        sl = slice(c * FF_CHUNK, (c + 1) * FF_CHUNK)
        gate = jnp.dot(u, wg_ref[:, sl], preferred_element_type=F32)
        up = jnp.dot(u, wu_ref[:, sl], preferred_element_type=F32)
        a_ref[:, sl] = (gate * jax.nn.sigmoid(gate) * up).astype(BF16)
    f = jnp.dot(a_ref[...], wd_ref[...], preferred_element_type=F32)
    o_ref[...] = x + 0.5 * _rms(f, gpost_ref[...])


def _ffn(h, layer, gpre, wg, wu, wd, gpost, mixer=None):
    bsz, tp, _ = h.shape
    tile = min(FFN_ROWS, tp)
    assert tp % tile == 0
    row_spec = pl.BlockSpec((None, tile, D_MODEL), lambda b, t: (b, t, 0))
    mixer_specs, mixer_args = [], []
    if mixer is not None:
        ylru, ot, p = mixer
        vdim = MLA_HEADS * V_DIM
        mixer_specs = [
            pl.BlockSpec((None, tile, LRU_WIDTH), lambda b, t: (b, t, 0)),
            pl.BlockSpec((None, vdim, tile), lambda b, t: (b, 0, t)),
            _const_spec((vdim, 1), layer),
            _const_spec((D_MODEL, D_MODEL), layer),
            _const_spec((1, D_MODEL), layer),
        ]
        mixer_args = [ylru, ot, p["mla_out_g"], p["w_out"], p["mix_post_g"]]
    return pl.pallas_call(
        functools.partial(_ffn_kernel, mixer is not None),
        grid=(bsz, tp // tile),
        in_specs=[row_spec] + mixer_specs + [
            _const_spec((1, D_MODEL), layer),
            _const_spec((D_MODEL, D_FF), layer),
            _const_spec((D_MODEL, D_FF), layer),
            _const_spec((D_FF, D_MODEL), layer),
            _const_spec((1, D_MODEL), layer),
        ],
        out_specs=row_spec,
        out_shape=jax.ShapeDtypeStruct(h.shape, F32),
        scratch_shapes=[pltpu.VMEM((tile, D_FF), BF16)],
        compiler_params=pltpu.CompilerParams(
            dimension_semantics=("arbitrary", "arbitrary"), vmem_limit_bytes=VMEM_LIMIT),
        name="ffn_mix" if mixer is not None else "ffn",
    )(h, *mixer_args, gpre, wg, wu, wd, gpost)


def _rope(x, c, sa, sb):
    n = x.shape[-1]
    half = QK_ROPE // 2
    return x * c + pltpu.roll(x, half, 1) * sa + pltpu.roll(x, n - half, 1) * sb


def _mix_in_kernel(emit_state, h_ref, g_ref, win_ref, cw_ref, cb_ref, wax_ref, ba_ref, bx_ref,
                   lam_ref, gq_ref, wuq_ref, gkv_ref, wuk_ref, wuv_ref, glru_ref,
                   rc_ref, rsa_ref, rsb_ref, cos8_ref, sin8_ref, xprev0_ref, hcar0_ref,
                   ylru_ref, qt_ref, k_ref, vt_ref, *rest):
    if emit_state:
        xr_out_ref, hs_out_ref, xprev_ref, hcar_ref = rest
    else:
        xprev_ref, hcar_ref = rest
    tt = h_ref.shape[0]

    @pl.when(pl.program_id(1) == 0)
    def _():
        xprev_ref[...] = xprev0_ref[...]
        hcar_ref[...] = hcar0_ref[...]

    sub = min(MIX_SUB, tt)
    half = LRU_WIDTH // 2
    nl = -lam_ref[...]
    softplus = jnp.maximum(nl, 0.0) + jnp.log1p(jnp.exp(-jnp.abs(nl)))
    decay = (-LRU_C * LOG2_E) * softplus
    gq_scaled = gq_ref[...] * (LOG2_E * (QK_NOPE + QK_ROPE) ** -0.5)
    tile_heads = lambda t: jnp.concatenate([t] * MLA_HEADS, axis=1)

    def project(i):
        u = _rms(h_ref[i * sub:(i + 1) * sub, :], g_ref[...]).astype(BF16)
        return jnp.dot(u, win_ref[...], preferred_element_type=F32)

    xprev = xprev_ref[...]
    carry = hcar_ref[...]
    z_next = project(0)
    for i in range(tt // sub):
        z = z_next
        if (i + 1) * sub < tt:
            z_next = project(i + 1)
        rows = slice(i * sub, (i + 1) * sub)
        xr = z[:, 0:LRU_WIDTH]
        gr = z[:, LRU_WIDTH:2 * LRU_WIDTH]
        cq = z[:, 2 * LRU_WIDTH:2 * LRU_WIDTH + Q_LORA]
        ckv = z[:, 2 * LRU_WIDTH + Q_LORA:2 * LRU_WIDTH + Q_LORA + KV_LORA]
        krt = z[:, 2 * LRU_WIDTH + Q_LORA + KV_LORA:]

        xe = jnp.concatenate([xprev, xr], axis=0)
        xc = xr * cw_ref[CONV_WIDTH - 1:CONV_WIDTH, :] + cb_ref[...]
        for j in range(1, CONV_WIDTH):
            w_j = cw_ref[CONV_WIDTH - 1 - j:CONV_WIDTH - j, :]
            xc = xc + pltpu.roll(xe, j, 0)[SUBLANES:] * w_j
        xprev = xr[sub - SUBLANES:]

        xcb = xc.astype(BF16)
        ri0 = jnp.dot(xcb[:, :half], wax_ref[0], preferred_element_type=F32)
        ri1 = jnp.dot(xcb[:, half:], wax_ref[1], preferred_element_type=F32)
        r = jax.nn.sigmoid(jnp.concatenate([ri0[:, :half], ri1[:, :half]], axis=1) + ba_ref[...])
        g_in = jax.nn.sigmoid(
            jnp.concatenate([ri0[:, half:], ri1[:, half:]], axis=1) + bx_ref[...])
        a = jnp.exp2(r * decay)
        gap = 1.0 - a * a
        b = (gap * lax.rsqrt(jnp.maximum(gap, TINY))) * (g_in * xc)

        row = lax.broadcasted_iota(jnp.int32, a.shape, 0) % SUBLANES
        s = 1
        while s < SUBLANES:
            valid = row >= s
            b = jnp.where(valid, a * pltpu.roll(b, s, 0) + b, b)
            a = jnp.where(valid, a * pltpu.roll(a, s, 0), a)
            s *= 2
        groups = []
        for gidx in range(sub // SUBLANES):
            sl = slice(gidx * SUBLANES, (gidx + 1) * SUBLANES)
            hg = b[sl] + a[sl] * carry
            carry = hg[SUBLANES - 1:SUBLANES]
            groups.append(hg)
        hs = jnp.concatenate(groups, axis=0)
        if emit_state:
            xr_out_ref[rows, :] = xr
            hs_out_ref[rows, :] = hs

        cg = 0.7978845608028654
        half_gr = 0.5 * gr
        gelu = half_gr + half_gr * jnp.tanh(gr * (cg + (cg * 0.044715) * (gr * gr)))
        ylru_ref[rows, :] = _rms(hs * gelu, glru_ref[...]).astype(BF16)

        rc = rc_ref[rows, :]
        rsa = rsa_ref[rows, :]
        rsb = rsb_ref[rows, :]
        cqn = _rms(cq, gq_scaled).astype(BF16)
        q = jnp.dot(cqn, wuq_ref[...], preferred_element_type=F32)
        q1 = q[:, Q_NOPE_ALL:Q_NOPE_ALL + Q_HALF_ALL]
        q2 = q[:, Q_NOPE_ALL + Q_HALF_ALL:]
        cos8 = cos8_ref[rows, :]
        sin8 = sin8_ref[rows, :]
        q = jnp.concatenate([q[:, :Q_NOPE_ALL], q1 * cos8 - q2 * sin8, q2 * cos8 + q1 * sin8],
                            axis=1)
        qt_ref[:, rows] = q.T.astype(BF16)

        ckvn = _rms(ckv, gkv_ref[...]).astype(BF16)
        kn = jnp.dot(ckvn, wuk_ref[...], preferred_element_type=F32)
        kr = _rope(krt, rc, rsa, rsb)
        k_ref[rows, :] = (kn + tile_heads(kr)).astype(BF16)
        v = jnp.dot(ckvn, wuv_ref[...], preferred_element_type=F32)
        vt_ref[:, rows] = v.T.astype(BF16)
    xprev_ref[...] = xprev
    hcar_ref[...] = carry


def _mix_in(h, layer, p, rope, xprev0, hcar0, emit_state=False):
    bsz, tp, _ = h.shape
    tt = min(MIX_TILE, tp)
    nt = tp // tt
    tbl_spec = pl.BlockSpec((tt, HEAD_PAD), lambda b, t: (t, 0))
    seq_spec = pl.BlockSpec((None, tt, LRU_WIDTH), lambda b, t: (b, t, 0))
    state_specs = [seq_spec, seq_spec] if emit_state else []
    state_shapes = [jax.ShapeDtypeStruct((bsz, tp, LRU_WIDTH), F32)] * 2 if emit_state else []
    return pl.pallas_call(
        functools.partial(_mix_in_kernel, emit_state),
        grid=(bsz, nt),
        in_specs=[
            pl.BlockSpec((None, tt, D_MODEL), lambda b, t: (b, t, 0)),
            _const_spec((1, D_MODEL), layer),
            _const_spec((D_MODEL, IN_COLS_PAD), layer),
            _const_spec((CONV_WIDTH, LRU_WIDTH), layer),
            _const_spec((1, LRU_WIDTH), layer),
            _const_spec((2, LRU_WIDTH // 2, LRU_WIDTH), layer),
            _const_spec((1, LRU_WIDTH), layer),
            _const_spec((1, LRU_WIDTH), layer),
            _const_spec((1, LRU_WIDTH), layer),
            _const_spec((1, Q_LORA), layer),
            _const_spec((Q_LORA, QT_ROWS), layer),
            _const_spec((1, KV_LORA), layer),
            _const_spec((KV_LORA, QK_WIDTH), layer),
            _const_spec((KV_LORA, MLA_HEADS * V_DIM), layer),
            _const_spec((1, LRU_WIDTH), layer),
            tbl_spec, tbl_spec, tbl_spec, tbl_spec, tbl_spec,
            _const_spec((SUBLANES, LRU_WIDTH)),
            _const_spec((1, LRU_WIDTH)),
        ],
        out_specs=[
            seq_spec,
            pl.BlockSpec((None, QT_ROWS, tt), lambda b, t: (b, 0, t)),
            pl.BlockSpec((None, tt, QK_WIDTH), lambda b, t: (b, t, 0)),
            pl.BlockSpec((None, MLA_HEADS * V_DIM, tt), lambda b, t: (b, 0, t)),
        ] + state_specs,
        out_shape=[
            jax.ShapeDtypeStruct((bsz, tp, LRU_WIDTH), BF16),
            jax.ShapeDtypeStruct((bsz, QT_ROWS, tp), BF16),
            jax.ShapeDtypeStruct((bsz, tp, QK_WIDTH), BF16),
            jax.ShapeDtypeStruct((bsz, MLA_HEADS * V_DIM, tp), BF16),
        ] + state_shapes,
        scratch_shapes=[pltpu.VMEM((SUBLANES, LRU_WIDTH), F32),
                        pltpu.VMEM((1, LRU_WIDTH), F32)],
        compiler_params=pltpu.CompilerParams(
            dimension_semantics=("arbitrary", "arbitrary"), vmem_limit_bytes=VMEM_LIMIT),
        name="mix_in",
    )(h, p["mix_pre_g"], p["w_in"], p["conv_w"], p["conv_b"], p["wax"], p["b_a"], p["b_x"],
      p["lam"], p["q_g"], p["w_uq"], p["kv_g"], p["w_uk"], p["w_uv"], p["lru_out_g"], *rope,
      xprev0, hcar0)


def _group_max(s):
    m = s[0:SUBLANES]
    for r in range(1, s.shape[0] // SUBLANES):
        m = jnp.maximum(m, s[r * SUBLANES:(r + 1) * SUBLANES])
    return m


def _sublane_allmax(m):
    for shift in (1, 2, 4):
        m = jnp.maximum(m, pltpu.roll(m, shift, 0))
    return m


def _attn_kernel(has_prefix, qt_ref, k_ref, vt_ref, *rest):
    if has_prefix:
        kpre_ref, vtpre_ref = rest[:2]
        rest = rest[2:]
        spre_ref = rest[-1]
        rest = rest[:-1]
    o_ref, m_ref, l_ref, acc_ref, s0_ref, s1_ref, bm0_ref, bm1_ref = rest
    qi = pl.program_id(1)
    tq = qt_ref.shape[1]
    tk = s0_ref.shape[1]
    s_refs = (s0_ref, s1_ref)
    bm_refs = (bm0_ref, bm1_ref)
    ones_rows = 2 * SUBLANES

    def with_ones(vb):
        return jnp.concatenate([vb, jnp.ones((ones_rows, vb.shape[1]), BF16)], axis=0)

    m_ref[...] = jnp.full(m_ref.shape, NEG_BIG, F32)
    l_ref[...] = jnp.zeros(l_ref.shape, F32)
    acc_ref[...] = jnp.zeros(acc_ref.shape, F32)

    def q_head(g):
        half = QK_ROPE // 2
        r1 = Q_NOPE_ALL + g * half
        r2 = r1 + Q_HALF_ALL
        return jnp.concatenate(
            [qt_ref[g * QK_NOPE:(g + 1) * QK_NOPE, :], qt_ref[r1:r1 + half, :],
             qt_ref[r2:r2 + half, :], jnp.zeros((HEAD_PAD - QK_NOPE - QK_ROPE, tq), BF16)], axis=0)

    def scores(j, g, slot):
        off = pl.multiple_of(j * tk, tk)
        s = jnp.dot(k_ref[pl.ds(off, tk), g * HEAD_PAD:(g + 1) * HEAD_PAD], q_head(g),
                    preferred_element_type=F32)
        s_refs[slot][g] = s
        bm_refs[slot][g] = _group_max(s)

    def softmax_pv(j, g, slot, masked):
        s = s_refs[slot][g]
        if masked:
            key = lax.broadcasted_iota(jnp.int32, s.shape, 0)
            qry = lax.broadcasted_iota(jnp.int32, s.shape, 1)
            s = jnp.where(key <= qry, s, NEG_BIG)
            bm = _group_max(s)
            if has_prefix:
                s_pre = spre_ref[g]
                bm = jnp.maximum(bm, _group_max(s_pre))
        else:
            bm = bm_refs[slot][g]
        m_old = m_ref[g]
        m_new = jnp.maximum(m_old, _sublane_allmax(bm))
        alpha = jnp.exp2(m_old - m_new)
        p = jnp.exp2(s - jnp.concatenate([m_new] * (tk // SUBLANES), axis=0))
        m_ref[g] = m_new
        off = pl.multiple_of(j * tk, tk)
        vb = with_ones(vt_ref[g * V_DIM:(g + 1) * V_DIM, pl.ds(off, tk)])
        pv = jnp.dot(vb, p.astype(BF16), preferred_element_type=F32)
        if masked and has_prefix:
            p_pre = jnp.exp2(s_pre - jnp.concatenate([m_new] * (N_META // SUBLANES), axis=0))
            p_pre = jnp.concatenate(
                [p_pre.astype(BF16), jnp.zeros((LANES - N_META, tq), BF16)], axis=0)
            pv = pv + jnp.dot(with_ones(vtpre_ref[g * V_DIM:(g + 1) * V_DIM, :]), p_pre,
                              preferred_element_type=F32)
        acc_ref[g] = (jnp.concatenate([alpha] * (V_DIM // SUBLANES), axis=0) * acc_ref[g]
                      + pv[:V_DIM])
        l_ref[g] = alpha * l_ref[g] + pv[V_DIM:V_DIM + SUBLANES]

    def step(j_next, slot_next, j, slot, masked=False):
        for g in range(MLA_HEADS):
            if j_next is not None:
                scores(j_next, g, slot_next)
            softmax_pv(j, g, slot, masked)

    for g in range(MLA_HEADS):
        scores(0, g, 0)
        if has_prefix:
            spre_ref[g] = jnp.dot(kpre_ref[:, g * HEAD_PAD:(g + 1) * HEAD_PAD], q_head(g),
                                  preferred_element_type=F32)

    def pair(i, carry):
        j = 2 * i
        step(j + 1, 1, j, 0)
        step(j + 2, 0, j + 1, 1)
        return carry

    lax.fori_loop(0, qi // 2, pair, 0)

    @pl.when(qi % 2 == 0)
    def _():
        step(None, None, qi, 0, True)

    @pl.when(qi % 2 == 1)
    def _():
        step(qi, 1, qi - 1, 0)
        step(None, None, qi, 1, True)

    for g in range(MLA_HEADS):
        inv_l = 1.0 / jnp.concatenate([l_ref[g]] * (V_DIM // SUBLANES), axis=0)
        o_ref[g * V_DIM:(g + 1) * V_DIM, :] = acc_ref[g] * inv_l


def _attention(qt, k, vt, prefix=None):
    bsz, _, tp = qt.shape
    tq = tk = min(SEQ_TILE, tp)
    vdim = MLA_HEADS * V_DIM
    stat = pltpu.VMEM((MLA_HEADS, SUBLANES, tq), F32)
    has_prefix = prefix is not None
    prefix_specs = ([_const_spec((N_META, QK_WIDTH)), _const_spec((vdim, LANES))]
                    if has_prefix else [])
    return pl.pallas_call(
        functools.partial(_attn_kernel, has_prefix),
        grid=(bsz, tp // tq),
        in_specs=[
            pl.BlockSpec((None, QT_ROWS, tq), lambda b, i: (b, 0, i)),
            pl.BlockSpec((None, tp, QK_WIDTH), lambda b, i: (b, 0, 0)),
            pl.BlockSpec((None, vdim, tp), lambda b, i: (b, 0, 0)),
        ] + prefix_specs,
        out_specs=pl.BlockSpec((None, vdim, tq), lambda b, i: (b, 0, i)),
        out_shape=jax.ShapeDtypeStruct((bsz, vdim, tp), F32),
        scratch_shapes=[stat, stat, pltpu.VMEM((MLA_HEADS, V_DIM, tq), F32),
                        pltpu.VMEM((MLA_HEADS, tk, tq), F32),
                        pltpu.VMEM((MLA_HEADS, tk, tq), F32), stat, stat]
        + ([pltpu.VMEM((MLA_HEADS, N_META, tq), F32)] if has_prefix else []),
        compiler_params=pltpu.CompilerParams(
            dimension_semantics=("arbitrary", "arbitrary"), vmem_limit_bytes=VMEM_LIMIT),
        name="attention",
    )(qt, k, vt, *(prefix or ()))


def _rope_tables(tp):
    pos = jnp.arange(tp, dtype=F32)
    inv_freq = 1.0 / (ROPE_THETA ** (jnp.arange(0, QK_ROPE, 2, dtype=F32) / QK_ROPE))
    ang = pos[:, None] * inv_freq[None, :]
    cos, sin = jnp.cos(ang), jnp.sin(ang)
    half = QK_ROPE // 2
    zeros = lambda n: jnp.zeros((tp, n), F32)
    pad = HEAD_PAD - QK_NOPE - QK_ROPE
    c = jnp.concatenate([jnp.ones((tp, QK_NOPE), F32), cos, cos, zeros(pad)], axis=1)
    sa = jnp.concatenate([zeros(QK_NOPE + half), sin, zeros(pad)], axis=1)
    sb = jnp.concatenate([zeros(QK_NOPE), -sin, zeros(half + pad)], axis=1)
    return c, sa, sb, jnp.tile(cos, (1, MLA_HEADS)), jnp.tile(sin, (1, MLA_HEADS))


def _block_diag(w):
    depth = w.shape[0]
    per = LRU_HEADS // 2
    w5 = w.reshape(depth, 2, per, LRU_HEAD_DIM, LRU_HEAD_DIM)
    bd = jnp.einsum("dnhij,hg->dnhigj", w5, jnp.eye(per, dtype=w.dtype))
    return bd.reshape(depth, 2, per * LRU_HEAD_DIM, per * LRU_HEAD_DIM)


def _row(g):
    return g.reshape(g.shape[0], 1, g.shape[1])


@jax.jit
def kernel(x, meta_tokens, ffn1_pre_g, ffn1_w_gate, ffn1_w_up, ffn1_w_down, ffn1_post_g, mix_pre_g, w_in, lru_conv_w, lru_conv_b, lru_w_a, lru_b_a, lru_w_x, lru_b_x, lru_lambda, mla_q_norm_g, mla_w_uq, mla_kv_norm_g, mla_w_ukv, lru_out_g, mla_out_g, w_out, mix_post_g, ffn2_pre_g, ffn2_w_gate, ffn2_w_up, ffn2_w_down, ffn2_post_g):
    bsz, seq, _ = x.shape
    depth = w_in.shape[0]
    assert seq % SEQ_TILE == 0 and seq % MIX_TILE == 0
    h = x
    hm = jnp.concatenate([meta_tokens.astype(x.dtype),
                          jnp.zeros((META_ROWS - N_META, D_MODEL), x.dtype)], axis=0)[None]
    rope_all = _rope_tables(N_META + max(seq, META_ROWS))
    rope_meta = tuple(t[:META_ROWS] for t in rope_all)
    rope_main = tuple(t[N_META:N_META + seq] for t in rope_all)
    zero_hist = jnp.zeros((SUBLANES, LRU_WIDTH), F32)
    zero_state = jnp.zeros((1, LRU_WIDTH), F32)

    split = 2 * LRU_WIDTH + Q_LORA + KV_LORA
    zcols = lambda n: jnp.zeros((depth, D_MODEL, n), w_in.dtype)
    w_in_p = jnp.concatenate(
        [w_in[..., :split], zcols(QK_NOPE), w_in[..., split:],
         zcols(HEAD_PAD - QK_NOPE - QK_ROPE)], axis=-1).astype(BF16)
    w_uq4 = mla_w_uq.reshape(depth, Q_LORA, MLA_HEADS, QK_NOPE + QK_ROPE)
    half = QK_ROPE // 2
    w_uq_p = jnp.concatenate(
        [w_uq4[..., :QK_NOPE].reshape(depth, Q_LORA, Q_NOPE_ALL),
         w_uq4[..., QK_NOPE:QK_NOPE + half].reshape(depth, Q_LORA, Q_HALF_ALL),
         w_uq4[..., QK_NOPE + half:].reshape(depth, Q_LORA, Q_HALF_ALL)], axis=-1)
    w_ukv4 = mla_w_ukv.reshape(depth, KV_LORA, MLA_HEADS, QK_NOPE + V_DIM)
    w_uk_p = jnp.pad(w_ukv4[..., :QK_NOPE], ((0, 0), (0, 0), (0, 0), (0, HEAD_PAD - QK_NOPE)))
    mixp = {
        "mix_pre_g": _row(mix_pre_g),
        "w_in": w_in_p,
        "conv_w": lru_conv_w,
        "conv_b": _row(lru_conv_b),
        "wax": jnp.concatenate([_block_diag(lru_w_a), _block_diag(lru_w_x)], axis=-1).astype(BF16),
        "b_a": _row(lru_b_a),
        "b_x": _row(lru_b_x),
        "lam": _row(lru_lambda),
        "q_g": _row(mla_q_norm_g),
        "w_uq": w_uq_p.astype(BF16),
        "kv_g": _row(mla_kv_norm_g),
        "w_uk": w_uk_p.reshape(depth, KV_LORA, QK_WIDTH).astype(BF16),
        "w_uv": w_ukv4[..., QK_NOPE:].reshape(depth, KV_LORA, MLA_HEADS * V_DIM).astype(BF16),
        "lru_out_g": _row(lru_out_g),
        "mla_out_g": mla_out_g.reshape(depth, MLA_HEADS * V_DIM, 1),
        "w_out": w_out.astype(BF16),
        "mix_post_g": _row(mix_post_g),
    }
    ffn1 = (_row(ffn1_pre_g), ffn1_w_gate.astype(BF16), ffn1_w_up.astype(BF16),
            ffn1_w_down.astype(BF16), _row(ffn1_post_g))
    ffn2 = (_row(ffn2_pre_g), ffn2_w_gate.astype(BF16), ffn2_w_up.astype(BF16),
            ffn2_w_down.astype(BF16), _row(ffn2_post_g))

    lane = lax.broadcasted_iota(jnp.int32, (MLA_HEADS * V_DIM, META_ROWS), 1)
    for layer in range(depth):
        hm = _ffn(hm, layer, *ffn1)
        h = _ffn(h, layer, *ffn1)
        ylru_m, qt_m, k_m, vt_m, xr_m, hs_m = _mix_in(
            hm, layer, mixp, rope_meta, zero_hist, zero_state, emit_state=True)
        ylru, qt, k, vt = _mix_in(h, layer, mixp, rope_main,
                                  xr_m[0, N_META - SUBLANES:N_META], hs_m[0, N_META - 1:N_META])
        prefix = (k_m[0, :N_META], jnp.where(lane < N_META, vt_m[0], 0).astype(BF16))
        h = _ffn(h, layer, *ffn2, mixer=(ylru, _attention(qt, k, vt, prefix), mixp))
        if layer + 1 < depth:
            hm = _ffn(hm, layer, *ffn2, mixer=(ylru_m, _attention(qt_m, k_m, vt_m), mixp))
    return h
```

```python
import functools

import jax
import jax.numpy as jnp
from jax import lax
from jax.experimental import pallas as pl
from jax.experimental.pallas import tpu as pltpu

D_MODEL = 1024
N_META = 16
LRU_WIDTH = 512
LRU_HEADS = 8
LRU_HEAD_DIM = 64
CONV_WIDTH = 4
LRU_C = 8.0
MLA_HEADS = 8
QK_NOPE = 64
QK_ROPE = 32
V_DIM = 64
Q_LORA = 384
KV_LORA = 256
D_FF = 2816
ROPE_THETA = 10000.0
EPS = 1e-6

LANES = 128
SUBLANES = 8
HEAD_PAD = LANES
QK_WIDTH = MLA_HEADS * HEAD_PAD
Q_NOPE_ALL = MLA_HEADS * QK_NOPE
Q_HALF_ALL = MLA_HEADS * QK_ROPE // 2
QT_ROWS = Q_NOPE_ALL + 2 * Q_HALF_ALL
IN_COLS_PAD = 2 * LRU_WIDTH + Q_LORA + KV_LORA + HEAD_PAD
SEQ_TILE = 256
MIX_TILE = 1024
MIX_SUB = 128
META_ROWS = LANES
FFN_ROWS = 512
FFN_SUBTILES = 2
FF_CHUNK = 256
NEG_BIG = -1e30
TINY = 1e-30
LOG2_E = 1.4426950408889634
VMEM_LIMIT = 52 * 1024 * 1024

F32 = jnp.float32
BF16 = jnp.bfloat16


def _rms(x, g):
    return x * lax.rsqrt(jnp.mean(x * x, axis=-1, keepdims=True) + EPS) * g


def _const_spec(shape, layer=None):
    if layer is None:
        idx = lambda *_: (0,) * len(shape)
        return pl.BlockSpec(shape, idx, pipeline_mode=pl.Buffered(1))
    idx = lambda *_: (layer,) + (0,) * len(shape)
    return pl.BlockSpec((None,) + shape, idx, pipeline_mode=pl.Buffered(1))


def _mix_out_rows(h, ylru, ot, gmla, wout_ref, gpost):
    ms = jnp.mean(ot * ot, axis=0, keepdims=True)
    on = (ot * lax.rsqrt(ms + EPS) * gmla).T.astype(BF16)
    y = jnp.dot(ylru, wout_ref[:LRU_WIDTH, :], preferred_element_type=F32)
    y = y + jnp.dot(on, wout_ref[LRU_WIDTH:, :], preferred_element_type=F32)
    return h + _rms(y, gpost)


def _ffn_kernel(after_mixer, h_ref, *refs):
    if after_mixer:
        ylru_ref, ot_ref, gmla_ref, wout_ref, gmix_ref = refs[:5]
        refs = refs[5:]
    gpre_ref, wg_ref, wu_ref, wd_ref, gpost_ref, o_ref, a_ref = refs
    rows = h_ref.shape[0]
    sub = rows // FFN_SUBTILES if rows % (FFN_SUBTILES * LANES) == 0 else rows
    parts = [slice(i * sub, (i + 1) * sub) for i in range(rows // sub)]
    xs = []
    for r in parts:
        x = h_ref[r, :]
        if after_mixer:
            x = _mix_out_rows(x, ylru_ref[r, :], ot_ref[:, r], gmla_ref[...], wout_ref,
                              gmix_ref[...])
        xs.append(x)
    for r, x in zip(parts, xs):
        u = _rms(x, gpre_ref[...]).astype(BF16)
        for c in range(D_FF // FF_CHUNK):
            sl = slice(c * FF_CHUNK, (c + 1) * FF_CHUNK)
            gate = jnp.dot(u, wg_ref[:, sl], preferred_element_type=F32)
            up = jnp.dot(u, wu_ref[:, sl], preferred_element_type=F32)
            a_ref[r, sl] = (gate * jax.nn.sigmoid(gate) * up).astype(BF16)
    for r, x in zip(parts, xs):
        f = jnp.dot(a_ref[r, :], wd_ref[...], preferred_element_type=F32)
        o_ref[r, :] = x + 0.5 * _rms(f, gpost_ref[...])


def _ffn(h, layer, gpre, wg, wu, wd, gpost, mixer=None):
    bsz, tp, _ = h.shape
    tile = min(FFN_ROWS, tp)
    assert tp % tile == 0
    row_spec = pl.BlockSpec((None, tile, D_MODEL), lambda b, t: (b, t, 0))
    mixer_specs, mixer_args = [], []
    if mixer is not None:
        ylru, ot, p = mixer
        vdim = MLA_HEADS * V_DIM
        mixer_specs = [
            pl.BlockSpec((None, tile, LRU_WIDTH), lambda b, t: (b, t, 0)),
            pl.BlockSpec((None, vdim, tile), lambda b, t: (b, 0, t)),
            _const_spec((vdim, 1), layer),
            _const_spec((D_MODEL, D_MODEL), layer),
            _const_spec((1, D_MODEL), layer),
        ]
        mixer_args = [ylru, ot, p["mla_out_g"], p["w_out"], p["mix_post_g"]]
    return pl.pallas_call(
        functools.partial(_ffn_kernel, mixer is not None),
        grid=(bsz, tp // tile),
        in_specs=[row_spec] + mixer_specs + [
            _const_spec((1, D_MODEL), layer),
            _const_spec((D_MODEL, D_FF), layer),
            _const_spec((D_MODEL, D_FF), layer),
            _const_spec((D_FF, D_MODEL), layer),
            _const_spec((1, D_MODEL), layer),
        ],
        out_specs=row_spec,
        out_shape=jax.ShapeDtypeStruct(h.shape, F32),
        scratch_shapes=[pltpu.VMEM((tile, D_FF), BF16)],
        compiler_params=pltpu.CompilerParams(
            dimension_semantics=("arbitrary", "arbitrary"), vmem_limit_bytes=VMEM_LIMIT),
        name="ffn_mix" if mixer is not None else "ffn",
    )(h, *mixer_args, gpre, wg, wu, wd, gpost)


def _rope(x, c, sa, sb):
    n = x.shape[-1]
    half = QK_ROPE // 2
    return x * c + pltpu.roll(x, half, 1) * sa + pltpu.roll(x, n - half, 1) * sb


def _mix_in_kernel(emit_state, h_ref, g_ref, win_ref, cw_ref, cb_ref, wax_ref, ba_ref, bx_ref,
                   lam_ref, gq_ref, wuq_ref, gkv_ref, wuk_ref, wuv_ref, glru_ref,
                   rc_ref, rsa_ref, rsb_ref, cos8_ref, sin8_ref, xprev0_ref, hcar0_ref,
                   ylru_ref, qt_ref, k_ref, vt_ref, *rest):
    if emit_state:
        xr_out_ref, hs_out_ref, xprev_ref, hcar_ref = rest
    else:
        xprev_ref, hcar_ref = rest
    tt = h_ref.shape[0]

    @pl.when(pl.program_id(1) == 0)
    def _():
        xprev_ref[...] = xprev0_ref[...]
        hcar_ref[...] = hcar0_ref[...]

    sub = min(MIX_SUB, tt)
    half = LRU_WIDTH // 2
    nl = -lam_ref[...]
    softplus = jnp.maximum(nl, 0.0) + jnp.log1p(jnp.exp(-jnp.abs(nl)))
    decay = (-LRU_C * LOG2_E) * softplus
    gq_scaled = gq_ref[...] * (LOG2_E * (QK_NOPE + QK_ROPE) ** -0.5)
    tile_heads = lambda t: jnp.concatenate([t] * MLA_HEADS, axis=1)

    def project(i):
        u = _rms(h_ref[i * sub:(i + 1) * sub, :], g_ref[...]).astype(BF16)
        return jnp.dot(u, win_ref[...], preferred_element_type=F32)

    xprev = xprev_ref[...]
    carry = hcar_ref[...]
    z_next = project(0)
    for i in range(tt // sub):
        z = z_next
        if (i + 1) * sub < tt:
            z_next = project(i + 1)
        rows = slice(i * sub, (i + 1) * sub)
        xr = z[:, 0:LRU_WIDTH]
        gr = z[:, LRU_WIDTH:2 * LRU_WIDTH]
        cq = z[:, 2 * LRU_WIDTH:2 * LRU_WIDTH + Q_LORA]
        ckv = z[:, 2 * LRU_WIDTH + Q_LORA:2 * LRU_WIDTH + Q_LORA + KV_LORA]
        krt = z[:, 2 * LRU_WIDTH + Q_LORA + KV_LORA:]

        xe = jnp.concatenate([xprev, xr], axis=0)
        xc = xr * cw_ref[CONV_WIDTH - 1:CONV_WIDTH, :] + cb_ref[...]
        for j in range(1, CONV_WIDTH):
            w_j = cw_ref[CONV_WIDTH - 1 - j:CONV_WIDTH - j, :]
            xc = xc + pltpu.roll(xe, j, 0)[SUBLANES:] * w_j
        xprev = xr[sub - SUBLANES:]

        xcb = xc.astype(BF16)
        ri0 = jnp.dot(xcb[:, :half], wax_ref[0], preferred_element_type=F32)
        ri1 = jnp.dot(xcb[:, half:], wax_ref[1], preferred_element_type=F32)
        r = jax.nn.sigmoid(jnp.concatenate([ri0[:, :half], ri1[:, :half]], axis=1) + ba_ref[...])
        g_in = jax.nn.sigmoid(
            jnp.concatenate([ri0[:, half:], ri1[:, half:]], axis=1) + bx_ref[...])
        a = jnp.exp2(r * decay)
        gap = 1.0 - a * a
        b = (gap * lax.rsqrt(jnp.maximum(gap, TINY))) * (g_in * xc)

        grouped = (sub // SUBLANES, SUBLANES, LRU_WIDTH)
        a = a.reshape(grouped)
        b = b.reshape(grouped)
        row = lax.broadcasted_iota(jnp.int32, grouped, 1)
        s = 1
        while s < SUBLANES:
            valid = row >= s
            b = a * jnp.where(valid, pltpu.roll(b, s, 1), 0.0) + b
            a = a * jnp.where(valid, pltpu.roll(a, s, 1), 1.0)
            s *= 2
        a = a.reshape(sub, LRU_WIDTH)
        b = b.reshape(sub, LRU_WIDTH)
        groups = []
        for gidx in range(sub // SUBLANES):
            sl = slice(gidx * SUBLANES, (gidx + 1) * SUBLANES)
            hg = b[sl] + a[sl] * carry
            carry = hg[SUBLANES - 1:SUBLANES]
            groups.append(hg)
        hs = jnp.concatenate(groups, axis=0)
        if emit_state:
            xr_out_ref[rows, :] = xr
            hs_out_ref[rows, :] = hs

        cg = 0.7978845608028654
        half_gr = 0.5 * gr
        gelu = half_gr + half_gr * jnp.tanh(gr * (cg + (cg * 0.044715) * (gr * gr)))
        ylru_ref[rows, :] = _rms(hs * gelu, glru_ref[...]).astype(BF16)

        rc = rc_ref[rows, :]
        rsa = rsa_ref[rows, :]
        rsb = rsb_ref[rows, :]
        cqn = _rms(cq, gq_scaled).astype(BF16)
        q = jnp.dot(cqn, wuq_ref[...], preferred_element_type=F32)
        q1 = q[:, Q_NOPE_ALL:Q_NOPE_ALL + Q_HALF_ALL]
        q2 = q[:, Q_NOPE_ALL + Q_HALF_ALL:]
        cos8 = cos8_ref[rows, :]
        sin8 = sin8_ref[rows, :]
        q = jnp.concatenate([q[:, :Q_NOPE_ALL], q1 * cos8 - q2 * sin8, q2 * cos8 + q1 * sin8],
                            axis=1)
        qt_ref[:, rows] = q.T.astype(BF16)

        ckvn = _rms(ckv, gkv_ref[...]).astype(BF16)
        kn = jnp.dot(ckvn, wuk_ref[...], preferred_element_type=F32)
        kr = _rope(krt, rc, rsa, rsb)
        k_ref[rows, :] = (kn + tile_heads(kr)).astype(BF16)
        v = jnp.dot(ckvn, wuv_ref[...], preferred_element_type=F32)
        vt_ref[:, rows] = v.T.astype(BF16)
    xprev_ref[...] = xprev
    hcar_ref[...] = carry


def _mix_in(h, layer, p, rope, xprev0, hcar0, emit_state=False):
    bsz, tp, _ = h.shape
    tt = min(MIX_TILE, tp)
    nt = tp // tt
    tbl_spec = pl.BlockSpec((tt, HEAD_PAD), lambda b, t: (t, 0))
    seq_spec = pl.BlockSpec((None, tt, LRU_WIDTH), lambda b, t: (b, t, 0))
    state_specs = [seq_spec, seq_spec] if emit_state else []
    state_shapes = [jax.ShapeDtypeStruct((bsz, tp, LRU_WIDTH), F32)] * 2 if emit_state else []
    return pl.pallas_call(
        functools.partial(_mix_in_kernel, emit_state),
        grid=(bsz, nt),
        in_specs=[
            pl.BlockSpec((None, tt, D_MODEL), lambda b, t: (b, t, 0)),
            _const_spec((1, D_MODEL), layer),
            _const_spec((D_MODEL, IN_COLS_PAD), layer),
            _const_spec((CONV_WIDTH, LRU_WIDTH), layer),
            _const_spec((1, LRU_WIDTH), layer),
            _const_spec((2, LRU_WIDTH // 2, LRU_WIDTH), layer),
            _const_spec((1, LRU_WIDTH), layer),
            _const_spec((1, LRU_WIDTH), layer),
            _const_spec((1, LRU_WIDTH), layer),
            _const_spec((1, Q_LORA), layer),
            _const_spec((Q_LORA, QT_ROWS), layer),
            _const_spec((1, KV_LORA), layer),
            _const_spec((KV_LORA, QK_WIDTH), layer),
            _const_spec((KV_LORA, MLA_HEADS * V_DIM), layer),
            _const_spec((1, LRU_WIDTH), layer),
            tbl_spec, tbl_spec, tbl_spec, tbl_spec, tbl_spec,
            _const_spec((SUBLANES, LRU_WIDTH)),
            _const_spec((1, LRU_WIDTH)),
        ],
        out_specs=[
            seq_spec,
            pl.BlockSpec((None, QT_ROWS, tt), lambda b, t: (b, 0, t)),
            pl.BlockSpec((None, tt, QK_WIDTH), lambda b, t: (b, t, 0)),
            pl.BlockSpec((None, MLA_HEADS * V_DIM, tt), lambda b, t: (b, 0, t)),
        ] + state_specs,
        out_shape=[
            jax.ShapeDtypeStruct((bsz, tp, LRU_WIDTH), BF16),
            jax.ShapeDtypeStruct((bsz, QT_ROWS, tp), BF16),
            jax.ShapeDtypeStruct((bsz, tp, QK_WIDTH), BF16),
            jax.ShapeDtypeStruct((bsz, MLA_HEADS * V_DIM, tp), BF16),
        ] + state_shapes,
        scratch_shapes=[pltpu.VMEM((SUBLANES, LRU_WIDTH), F32),
                        pltpu.VMEM((1, LRU_WIDTH), F32)],
        compiler_params=pltpu.CompilerParams(
            dimension_semantics=("arbitrary", "arbitrary"), vmem_limit_bytes=VMEM_LIMIT),
        name="mix_in",
    )(h, p["mix_pre_g"], p["w_in"], p["conv_w"], p["conv_b"], p["wax"], p["b_a"], p["b_x"],
      p["lam"], p["q_g"], p["w_uq"], p["kv_g"], p["w_uk"], p["w_uv"], p["lru_out_g"], *rope,
      xprev0, hcar0)


def _group_max(s):
    m = s[0:SUBLANES]
    for r in range(1, s.shape[0] // SUBLANES):
        m = jnp.maximum(m, s[r * SUBLANES:(r + 1) * SUBLANES])
    return m


def _sublane_allmax(m):
    for shift in (1, 2, 4):
        m = jnp.maximum(m, pltpu.roll(m, shift, 0))
    return m


def _attn_kernel(has_prefix, qt_ref, k_ref, vt_ref, *rest):
    if has_prefix:
        kpre_ref, vtpre_ref = rest[:2]
        rest = rest[2:]
        spre_ref = rest[-1]
        rest = rest[:-1]
    o_ref, m_ref, l_ref, acc_ref, s0_ref, s1_ref, bm0_ref, bm1_ref = rest
    qi = pl.program_id(1)
    nq = pl.num_programs(1)
    tq = o_ref.shape[1]
    tk = s0_ref.shape[1]
    s_refs = (s0_ref, s1_ref)
    bm_refs = (bm0_ref, bm1_ref)
    ones_rows = 2 * SUBLANES

    def with_ones(vb):
        return jnp.concatenate([vb, jnp.ones((ones_rows, vb.shape[1]), BF16)], axis=0)

    m_ref[...] = jnp.full(m_ref.shape, NEG_BIG, F32)
    l_ref[...] = jnp.zeros(l_ref.shape, F32)
    acc_ref[...] = jnp.zeros(acc_ref.shape, F32)

    def q_head(g, tile):
        half = QK_ROPE // 2
        r1 = Q_NOPE_ALL + g * half
        r2 = r1 + Q_HALF_ALL
        cols = pl.ds(pl.multiple_of(tile * tq, tq), tq)
        return jnp.concatenate(
            [qt_ref[g * QK_NOPE:(g + 1) * QK_NOPE, cols], qt_ref[r1:r1 + half, cols],
             qt_ref[r2:r2 + half, cols], jnp.zeros((HEAD_PAD - QK_NOPE - QK_ROPE, tq), BF16)],
            axis=0)

    def scores(j, g, slot, tile):
        off = pl.multiple_of(j * tk, tk)
        s = jnp.dot(k_ref[pl.ds(off, tk), g * HEAD_PAD:(g + 1) * HEAD_PAD], q_head(g, tile),
                    preferred_element_type=F32)
        s_refs[slot][g] = s
        bm_refs[slot][g] = _group_max(s)

    def first_scores(g, tile):
        scores(0, g, 0, tile)
        if has_prefix:
            spre_ref[g] = jnp.dot(kpre_ref[:, g * HEAD_PAD:(g + 1) * HEAD_PAD], q_head(g, tile),
                                  preferred_element_type=F32)

    def softmax_pv(j, g, slot, masked):
        s = s_refs[slot][g]
        if masked:
            key = lax.broadcasted_iota(jnp.int32, s.shape, 0)
            qry = lax.broadcasted_iota(jnp.int32, s.shape, 1)
            s = jnp.where(key <= qry, s, NEG_BIG)
            bm = _group_max(s)
            if has_prefix:
                s_pre = spre_ref[g]
                bm = jnp.maximum(bm, _group_max(s_pre))
        else:
            bm = bm_refs[slot][g]
        m_old = m_ref[g]
        m_new = jnp.maximum(m_old, _sublane_allmax(bm))
        alpha = jnp.exp2(m_old - m_new)
        p = jnp.exp2(s - jnp.concatenate([m_new] * (tk // SUBLANES), axis=0))
        m_ref[g] = m_new
        off = pl.multiple_of(j * tk, tk)
        vb = with_ones(vt_ref[g * V_DIM:(g + 1) * V_DIM, pl.ds(off, tk)])
        pv = jnp.dot(vb, p.astype(BF16), preferred_element_type=F32)
        if masked and has_prefix:
            p_pre = jnp.exp2(s_pre - jnp.concatenate([m_new] * (N_META // SUBLANES), axis=0))
            p_pre = jnp.concatenate(
                [p_pre.astype(BF16), jnp.zeros((LANES - N_META, tq), BF16)], axis=0)
            pv = pv + jnp.dot(with_ones(vtpre_ref[g * V_DIM:(g + 1) * V_DIM, :]), p_pre,
                              preferred_element_type=F32)
        acc_ref[g] = (jnp.concatenate([alpha] * (V_DIM // SUBLANES), axis=0) * acc_ref[g]
                      + pv[:V_DIM])
        l_ref[g] = alpha * l_ref[g] + pv[V_DIM:V_DIM + SUBLANES]

    def step(j_next, slot_next, j, slot):
        for g in range(MLA_HEADS):
            scores(j_next, g, slot_next, qi)
            softmax_pv(j, g, slot, False)

    def last_step(slot):
        nxt = jnp.minimum(qi + 1, nq - 1)
        for g in range(MLA_HEADS):
            softmax_pv(qi, g, slot, True)
            first_scores(g, nxt)

    @pl.when(qi == 0)
    def _():
        for g in range(MLA_HEADS):
            first_scores(g, 0)

    def pair(i, carry):
        j = 2 * i
        step(j + 1, 1, j, 0)
        step(j + 2, 0, j + 1, 1)
        return carry

    lax.fori_loop(0, qi // 2, pair, 0)

    @pl.when(qi % 2 == 0)
    def _():
        last_step(0)

    @pl.when(qi % 2 == 1)
    def _():
        step(qi, 1, qi - 1, 0)
        last_step(1)

    for g in range(MLA_HEADS):
        inv_l = 1.0 / jnp.concatenate([l_ref[g]] * (V_DIM // SUBLANES), axis=0)
        o_ref[g * V_DIM:(g + 1) * V_DIM, :] = acc_ref[g] * inv_l


def _attention(qt, k, vt, prefix=None):
    bsz, _, tp = qt.shape
    tq = tk = min(SEQ_TILE, tp)
    vdim = MLA_HEADS * V_DIM
    stat = pltpu.VMEM((MLA_HEADS, SUBLANES, tq), F32)
    has_prefix = prefix is not None
    prefix_specs = ([_const_spec((N_META, QK_WIDTH)), _const_spec((vdim, LANES))]
                    if has_prefix else [])
    return pl.pallas_call(
        functools.partial(_attn_kernel, has_prefix),
        grid=(bsz, tp // tq),
        in_specs=[
            pl.BlockSpec((None, QT_ROWS, tp), lambda b, i: (b, 0, 0)),
            pl.BlockSpec((None, tp, QK_WIDTH), lambda b, i: (b, 0, 0)),
            pl.BlockSpec((None, vdim, tp), lambda b, i: (b, 0, 0)),
        ] + prefix_specs,
        out_specs=pl.BlockSpec((None, vdim, tq), lambda b, i: (b, 0, i)),
        out_shape=jax.ShapeDtypeStruct((bsz, vdim, tp), F32),
        scratch_shapes=[stat, stat, pltpu.VMEM((MLA_HEADS, V_DIM, tq), F32),
                        pltpu.VMEM((MLA_HEADS, tk, tq), F32),
                        pltpu.VMEM((MLA_HEADS, tk, tq), F32), stat, stat]
        + ([pltpu.VMEM((MLA_HEADS, N_META, tq), F32)] if has_prefix else []),
        compiler_params=pltpu.CompilerParams(
            dimension_semantics=("arbitrary", "arbitrary"), vmem_limit_bytes=VMEM_LIMIT),
        name="attention",
    )(qt, k, vt, *(prefix or ()))


def _rope_tables(tp):
    pos = jnp.arange(tp, dtype=F32)
    inv_freq = 1.0 / (ROPE_THETA ** (jnp.arange(0, QK_ROPE, 2, dtype=F32) / QK_ROPE))
    ang = pos[:, None] * inv_freq[None, :]
    cos, sin = jnp.cos(ang), jnp.sin(ang)
    half = QK_ROPE // 2
    zeros = lambda n: jnp.zeros((tp, n), F32)
    pad = HEAD_PAD - QK_NOPE - QK_ROPE
    c = jnp.concatenate([jnp.ones((tp, QK_NOPE), F32), cos, cos, zeros(pad)], axis=1)
    sa = jnp.concatenate([zeros(QK_NOPE + half), sin, zeros(pad)], axis=1)
    sb = jnp.concatenate([zeros(QK_NOPE), -sin, zeros(half + pad)], axis=1)
    return c, sa, sb, jnp.tile(cos, (1, MLA_HEADS)), jnp.tile(sin, (1, MLA_HEADS))


def _block_diag(w):
    depth = w.shape[0]
    per = LRU_HEADS // 2
    w5 = w.reshape(depth, 2, per, LRU_HEAD_DIM, LRU_HEAD_DIM)
    bd = jnp.einsum("dnhij,hg->dnhigj", w5, jnp.eye(per, dtype=w.dtype))
    return bd.reshape(depth, 2, per * LRU_HEAD_DIM, per * LRU_HEAD_DIM)


def _row(g):
    return g.reshape(g.shape[0], 1, g.shape[1])


@jax.jit
def kernel(x, meta_tokens, ffn1_pre_g, ffn1_w_gate, ffn1_w_up, ffn1_w_down, ffn1_post_g, mix_pre_g, w_in, lru_conv_w, lru_conv_b, lru_w_a, lru_b_a, lru_w_x, lru_b_x, lru_lambda, mla_q_norm_g, mla_w_uq, mla_kv_norm_g, mla_w_ukv, lru_out_g, mla_out_g, w_out, mix_post_g, ffn2_pre_g, ffn2_w_gate, ffn2_w_up, ffn2_w_down, ffn2_post_g):
    bsz, seq, _ = x.shape
    depth = w_in.shape[0]
    assert seq % SEQ_TILE == 0 and seq % MIX_TILE == 0
    h = x
    hm = jnp.concatenate([meta_tokens.astype(x.dtype),
                          jnp.zeros((META_ROWS - N_META, D_MODEL), x.dtype)], axis=0)[None]
    rope_all = _rope_tables(N_META + max(seq, META_ROWS))
    rope_meta = tuple(t[:META_ROWS] for t in rope_all)
    rope_main = tuple(t[N_META:N_META + seq] for t in rope_all)
    zero_hist = jnp.zeros((SUBLANES, LRU_WIDTH), F32)
    zero_state = jnp.zeros((1, LRU_WIDTH), F32)

    split = 2 * LRU_WIDTH + Q_LORA + KV_LORA
    zcols = lambda n: jnp.zeros((depth, D_MODEL, n), w_in.dtype)
    w_in_p = jnp.concatenate(
        [w_in[..., :split], zcols(QK_NOPE), w_in[..., split:],
         zcols(HEAD_PAD - QK_NOPE - QK_ROPE)], axis=-1).astype(BF16)
    w_uq4 = mla_w_uq.reshape(depth, Q_LORA, MLA_HEADS, QK_NOPE + QK_ROPE)
    half = QK_ROPE // 2
    w_uq_p = jnp.concatenate(
        [w_uq4[..., :QK_NOPE].reshape(depth, Q_LORA, Q_NOPE_ALL),
         w_uq4[..., QK_NOPE:QK_NOPE + half].reshape(depth, Q_LORA, Q_HALF_ALL),
         w_uq4[..., QK_NOPE + half:].reshape(depth, Q_LORA, Q_HALF_ALL)], axis=-1)
    w_ukv4 = mla_w_ukv.reshape(depth, KV_LORA, MLA_HEADS, QK_NOPE + V_DIM)
    w_uk_p = jnp.pad(w_ukv4[..., :QK_NOPE], ((0, 0), (0, 0), (0, 0), (0, HEAD_PAD - QK_NOPE)))
    mixp = {
        "mix_pre_g": _row(mix_pre_g),
        "w_in": w_in_p,
        "conv_w": lru_conv_w,
        "conv_b": _row(lru_conv_b),
        "wax": jnp.concatenate([_block_diag(lru_w_a), _block_diag(lru_w_x)], axis=-1).astype(BF16),
        "b_a": _row(lru_b_a),
        "b_x": _row(lru_b_x),
        "lam": _row(lru_lambda),
        "q_g": _row(mla_q_norm_g),
        "w_uq": w_uq_p.astype(BF16),
        "kv_g": _row(mla_kv_norm_g),
        "w_uk": w_uk_p.reshape(depth, KV_LORA, QK_WIDTH).astype(BF16),
        "w_uv": w_ukv4[..., QK_NOPE:].reshape(depth, KV_LORA, MLA_HEADS * V_DIM).astype(BF16),
        "lru_out_g": _row(lru_out_g),
        "mla_out_g": mla_out_g.reshape(depth, MLA_HEADS * V_DIM, 1),
        "w_out": w_out.astype(BF16),
        "mix_post_g": _row(mix_post_g),
    }
    ffn1 = (_row(ffn1_pre_g), ffn1_w_gate.astype(BF16), ffn1_w_up.astype(BF16),
            ffn1_w_down.astype(BF16), _row(ffn1_post_g))
    ffn2 = (_row(ffn2_pre_g), ffn2_w_gate.astype(BF16), ffn2_w_up.astype(BF16),
            ffn2_w_down.astype(BF16), _row(ffn2_post_g))

    lane = lax.broadcasted_iota(jnp.int32, (MLA_HEADS * V_DIM, META_ROWS), 1)
    for layer in range(depth):
        hm = _ffn(hm, layer, *ffn1)
        h = _ffn(h, layer, *ffn1)
        ylru_m, qt_m, k_m, vt_m, xr_m, hs_m = _mix_in(
            hm, layer, mixp, rope_meta, zero_hist, zero_state, emit_state=True)
        ylru, qt, k, vt = _mix_in(h, layer, mixp, rope_main,
                                  xr_m[0, N_META - SUBLANES:N_META], hs_m[0, N_META - 1:N_META])
        prefix = (k_m[0, :N_META], jnp.where(lane < N_META, vt_m[0], 0).astype(BF16))
        h = _ffn(h, layer, *ffn2, mixer=(ylru, _attention(qt, k, vt, prefix), mixp))
        if layer + 1 < depth:
            hm = _ffn(hm, layer, *ffn2, mixer=(ylru_m, _attention(qt_m, k_m, vt_m), mixp))
    return h
```

```python
import functools

import jax
import jax.numpy as jnp
from jax import lax
from jax.experimental import pallas as pl
from jax.experimental.pallas import tpu as pltpu

D_MODEL = 1024
N_META = 16
LRU_WIDTH = 512
LRU_HEADS = 8
LRU_HEAD_DIM = 64
CONV_WIDTH = 4
LRU_C = 8.0
MLA_HEADS = 8
QK_NOPE = 64
QK_ROPE = 32
V_DIM = 64
Q_LORA = 384
KV_LORA = 256
D_FF = 2816
ROPE_THETA = 10000.0
EPS = 1e-6

LANES = 128
SUBLANES = 8
HEAD_PAD = LANES
QK_WIDTH = MLA_HEADS * HEAD_PAD
Q_NOPE_ALL = MLA_HEADS * QK_NOPE
Q_HALF_ALL = MLA_HEADS * QK_ROPE // 2
QT_ROWS = Q_NOPE_ALL + 2 * Q_HALF_ALL
IN_COLS_PAD = 2 * LRU_WIDTH + Q_LORA + KV_LORA + HEAD_PAD
SEQ_TILE = 256
MIX_TILE = 1024
MIX_SUB = 256
META_ROWS = LANES
FFN_ROWS = 512
FFN_SUBTILES = 2
FF_CHUNK = 256
NEG_BIG = -1e30
TINY = 1e-30
LOG2_E = 1.4426950408889634
VMEM_LIMIT = 52 * 1024 * 1024

F32 = jnp.float32
BF16 = jnp.bfloat16


def _rms(x, g):
    return x * lax.rsqrt(jnp.mean(x * x, axis=-1, keepdims=True) + EPS) * g


def _const_spec(shape, layer=None):
    if layer is None:
        idx = lambda *_: (0,) * len(shape)
        return pl.BlockSpec(shape, idx, pipeline_mode=pl.Buffered(1))
    idx = lambda *_: (layer,) + (0,) * len(shape)
    return pl.BlockSpec((None,) + shape, idx, pipeline_mode=pl.Buffered(1))


def _mix_out_rows(h, ylru, ot, gmla, wout_ref, gpost):
    ms = jnp.mean(ot * ot, axis=0, keepdims=True)
    on = (ot * lax.rsqrt(ms + EPS) * gmla).T.astype(BF16)
    y = jnp.dot(ylru, wout_ref[:LRU_WIDTH, :], preferred_element_type=F32)
    y = y + jnp.dot(on, wout_ref[LRU_WIDTH:, :], preferred_element_type=F32)
    return h + _rms(y, gpost)


def _ffn_kernel(after_mixer, h_ref, *refs):
    if after_mixer:
        ylru_ref, ot_ref, gmla_ref, wout_ref, gmix_ref = refs[:5]
        refs = refs[5:]
    gpre_ref, wg_ref, wu_ref, wd_ref, gpost_ref, o_ref, a_ref = refs
    rows = h_ref.shape[0]
    sub = rows // FFN_SUBTILES if after_mixer and rows % (FFN_SUBTILES * LANES) == 0 else rows
    parts = [slice(i * sub, (i + 1) * sub) for i in range(rows // sub)]
    xs = []
    for r in parts:
        x = h_ref[r, :]
        if after_mixer:
            x = _mix_out_rows(x, ylru_ref[r, :], ot_ref[:, r], gmla_ref[...], wout_ref,
                              gmix_ref[...])
        xs.append(x)
    for r, x in zip(parts, xs):
        u = _rms(x, gpre_ref[...]).astype(BF16)
        for c in range(D_FF // FF_CHUNK):
            sl = slice(c * FF_CHUNK, (c + 1) * FF_CHUNK)
            gate = jnp.dot(u, wg_ref[:, sl], preferred_element_type=F32)
            up = jnp.dot(u, wu_ref[:, sl], preferred_element_type=F32)
            a_ref[r, sl] = (gate * jax.nn.sigmoid(gate) * up).astype(BF16)
    for r, x in zip(parts, xs):
        f = jnp.dot(a_ref[r, :], wd_ref[...], preferred_element_type=F32)
        o_ref[r, :] = x + 0.5 * _rms(f, gpost_ref[...])


def _ffn(h, layer, gpre, wg, wu, wd, gpost, mixer=None):
    bsz, tp, _ = h.shape
    tile = min(FFN_ROWS, tp)
    assert tp % tile == 0
    row_spec = pl.BlockSpec((None, tile, D_MODEL), lambda b, t: (b, t, 0))
    mixer_specs, mixer_args = [], []
    if mixer is not None:
        ylru, ot, p = mixer
        vdim = MLA_HEADS * V_DIM
        mixer_specs = [
            pl.BlockSpec((None, tile, LRU_WIDTH), lambda b, t: (b, t, 0)),
            pl.BlockSpec((None, vdim, tile), lambda b, t: (b, 0, t)),
            _const_spec((vdim, 1), layer),
            _const_spec((D_MODEL, D_MODEL), layer),
            _const_spec((1, D_MODEL), layer),
        ]
        mixer_args = [ylru, ot, p["mla_out_g"], p["w_out"], p["mix_post_g"]]
    return pl.pallas_call(
        functools.partial(_ffn_kernel, mixer is not None),
        grid=(bsz, tp // tile),
        in_specs=[row_spec] + mixer_specs + [
            _const_spec((1, D_MODEL), layer),
            _const_spec((D_MODEL, D_FF), layer),
            _const_spec((D_MODEL, D_FF), layer),
            _const_spec((D_FF, D_MODEL), layer),
            _const_spec((1, D_MODEL), layer),
        ],
        out_specs=row_spec,
        out_shape=jax.ShapeDtypeStruct(h.shape, F32),
        scratch_shapes=[pltpu.VMEM((tile, D_FF), BF16)],
        compiler_params=pltpu.CompilerParams(
            dimension_semantics=("arbitrary", "arbitrary"), vmem_limit_bytes=VMEM_LIMIT),
        name="ffn_mix" if mixer is not None else "ffn",
    )(h, *mixer_args, gpre, wg, wu, wd, gpost)


def _rope(x, c, sa, sb):
    n = x.shape[-1]
    half = QK_ROPE // 2
    return x * c + pltpu.roll(x, half, 1) * sa + pltpu.roll(x, n - half, 1) * sb


def _mix_in_kernel(emit_state, h_ref, g_ref, win_ref, cw_ref, cb_ref, wax_ref, ba_ref, bx_ref,
                   lam_ref, gq_ref, wuq_ref, gkv_ref, wuk_ref, wuv_ref, glru_ref,
                   rc_ref, rsa_ref, rsb_ref, cos8_ref, sin8_ref, xprev0_ref, hcar0_ref,
                   ylru_ref, qt_ref, k_ref, vt_ref, *rest):
    if emit_state:
        xr_out_ref, hs_out_ref, xprev_ref, hcar_ref = rest
    else:
        xprev_ref, hcar_ref = rest
    tt = h_ref.shape[0]

    @pl.when(pl.program_id(1) == 0)
    def _():
        xprev_ref[...] = xprev0_ref[...]
        hcar_ref[...] = hcar0_ref[...]

    sub = min(MIX_SUB, tt)
    half = LRU_WIDTH // 2
    nl = -lam_ref[...]
    softplus = jnp.maximum(nl, 0.0) + jnp.log1p(jnp.exp(-jnp.abs(nl)))
    decay = (-LRU_C * LOG2_E) * softplus
    gq_scaled = gq_ref[...] * (LOG2_E * (QK_NOPE + QK_ROPE) ** -0.5)
    tile_heads = lambda t: jnp.concatenate([t] * MLA_HEADS, axis=1)

    def project(i):
        u = _rms(h_ref[i * sub:(i + 1) * sub, :], g_ref[...]).astype(BF16)
        return jnp.dot(u, win_ref[...], preferred_element_type=F32)

    xprev = xprev_ref[...]
    carry = hcar_ref[...]
    z_next = project(0)
    for i in range(tt // sub):
        z = z_next
        if (i + 1) * sub < tt:
            z_next = project(i + 1)
        rows = slice(i * sub, (i + 1) * sub)
        xr = z[:, 0:LRU_WIDTH]
        gr = z[:, LRU_WIDTH:2 * LRU_WIDTH]
        cq = z[:, 2 * LRU_WIDTH:2 * LRU_WIDTH + Q_LORA]
        ckv = z[:, 2 * LRU_WIDTH + Q_LORA:2 * LRU_WIDTH + Q_LORA + KV_LORA]
        krt = z[:, 2 * LRU_WIDTH + Q_LORA + KV_LORA:]

        xe = jnp.concatenate([xprev, xr], axis=0)
        xc = xr * cw_ref[CONV_WIDTH - 1:CONV_WIDTH, :] + cb_ref[...]
        for j in range(1, CONV_WIDTH):
            w_j = cw_ref[CONV_WIDTH - 1 - j:CONV_WIDTH - j, :]
            xc = xc + pltpu.roll(xe, j, 0)[SUBLANES:] * w_j
        xprev = xr[sub - SUBLANES:]

        xcb = xc.astype(BF16)
        ri0 = jnp.dot(xcb[:, :half], wax_ref[0], preferred_element_type=F32)
        ri1 = jnp.dot(xcb[:, half:], wax_ref[1], preferred_element_type=F32)
        r = jax.nn.sigmoid(jnp.concatenate([ri0[:, :half], ri1[:, :half]], axis=1) + ba_ref[...])
        g_in = jax.nn.sigmoid(
            jnp.concatenate([ri0[:, half:], ri1[:, half:]], axis=1) + bx_ref[...])
        a = jnp.exp2(r * decay)
        gap = 1.0 - a * a
        b = (gap * lax.rsqrt(jnp.maximum(gap, TINY))) * (g_in * xc)

        grouped = (sub // SUBLANES, SUBLANES, LRU_WIDTH)
        a = a.reshape(grouped)
        b = b.reshape(grouped)
        row = lax.broadcasted_iota(jnp.int32, grouped, 1)
        s = 1
        while s < SUBLANES:
            valid = row >= s
            b = a * jnp.where(valid, pltpu.roll(b, s, 1), 0.0) + b
            a = a * jnp.where(valid, pltpu.roll(a, s, 1), 1.0)
            s *= 2
        a = a.reshape(sub, LRU_WIDTH)
        b = b.reshape(sub, LRU_WIDTH)
        groups = []
        for gidx in range(sub // SUBLANES):
            sl = slice(gidx * SUBLANES, (gidx + 1) * SUBLANES)
            hg = b[sl] + a[sl] * carry
            carry = hg[SUBLANES - 1:SUBLANES]
            groups.append(hg)
        hs = jnp.concatenate(groups, axis=0)
        if emit_state:
            xr_out_ref[rows, :] = xr
            hs_out_ref[rows, :] = hs

        cg = 0.7978845608028654
        half_gr = 0.5 * gr
        gelu = half_gr + half_gr * jnp.tanh(gr * (cg + (cg * 0.044715) * (gr * gr)))
        ylru_ref[rows, :] = _rms(hs * gelu, glru_ref[...]).astype(BF16)

        rc = rc_ref[rows, :]
        rsa = rsa_ref[rows, :]
        rsb = rsb_ref[rows, :]
        cqn = _rms(cq, gq_scaled).astype(BF16)
        q = jnp.dot(cqn, wuq_ref[...], preferred_element_type=F32)
        q1 = q[:, Q_NOPE_ALL:Q_NOPE_ALL + Q_HALF_ALL]
        q2 = q[:, Q_NOPE_ALL + Q_HALF_ALL:]
        cos8 = cos8_ref[rows, :]
        sin8 = sin8_ref[rows, :]
        q = jnp.concatenate([q[:, :Q_NOPE_ALL], q1 * cos8 - q2 * sin8, q2 * cos8 + q1 * sin8],
                            axis=1)
        qt_ref[:, rows] = q.T.astype(BF16)

        ckvn = _rms(ckv, gkv_ref[...]).astype(BF16)
        kn = jnp.dot(ckvn, wuk_ref[...], preferred_element_type=F32)
        kr = _rope(krt, rc, rsa, rsb)
        k_ref[rows, :] = (kn + tile_heads(kr)).astype(BF16)
        v = jnp.dot(ckvn, wuv_ref[...], preferred_element_type=F32)
        vt_ref[:, rows] = v.T.astype(BF16)
    xprev_ref[...] = xprev
    hcar_ref[...] = carry


def _mix_in(h, layer, p, rope, xprev0, hcar0, emit_state=False):
    bsz, tp, _ = h.shape
    tt = min(MIX_TILE, tp)
    nt = tp // tt
    tbl_spec = pl.BlockSpec((tt, HEAD_PAD), lambda b, t: (t, 0))
    seq_spec = pl.BlockSpec((None, tt, LRU_WIDTH), lambda b, t: (b, t, 0))
    state_specs = [seq_spec, seq_spec] if emit_state else []
    state_shapes = [jax.ShapeDtypeStruct((bsz, tp, LRU_WIDTH), F32)] * 2 if emit_state else []
    return pl.pallas_call(
        functools.partial(_mix_in_kernel, emit_state),
        grid=(bsz, nt),
        in_specs=[
            pl.BlockSpec((None, tt, D_MODEL), lambda b, t: (b, t, 0)),
            _const_spec((1, D_MODEL), layer),
            _const_spec((D_MODEL, IN_COLS_PAD), layer),
            _const_spec((CONV_WIDTH, LRU_WIDTH), layer),
            _const_spec((1, LRU_WIDTH), layer),
            _const_spec((2, LRU_WIDTH // 2, LRU_WIDTH), layer),
            _const_spec((1, LRU_WIDTH), layer),
            _const_spec((1, LRU_WIDTH), layer),
            _const_spec((1, LRU_WIDTH), layer),
            _const_spec((1, Q_LORA), layer),
            _const_spec((Q_LORA, QT_ROWS), layer),
            _const_spec((1, KV_LORA), layer),
            _const_spec((KV_LORA, QK_WIDTH), layer),
            _const_spec((KV_LORA, MLA_HEADS * V_DIM), layer),
            _const_spec((1, LRU_WIDTH), layer),
            tbl_spec, tbl_spec, tbl_spec, tbl_spec, tbl_spec,
            _const_spec((SUBLANES, LRU_WIDTH)),
            _const_spec((1, LRU_WIDTH)),
        ],
        out_specs=[
            seq_spec,
            pl.BlockSpec((None, QT_ROWS, tt), lambda b, t: (b, 0, t)),
            pl.BlockSpec((None, tt, QK_WIDTH), lambda b, t: (b, t, 0)),
            pl.BlockSpec((None, MLA_HEADS * V_DIM, tt), lambda b, t: (b, 0, t)),
        ] + state_specs,
        out_shape=[
            jax.ShapeDtypeStruct((bsz, tp, LRU_WIDTH), BF16),
            jax.ShapeDtypeStruct((bsz, QT_ROWS, tp), BF16),
            jax.ShapeDtypeStruct((bsz, tp, QK_WIDTH), BF16),
            jax.ShapeDtypeStruct((bsz, MLA_HEADS * V_DIM, tp), BF16),
        ] + state_shapes,
        scratch_shapes=[pltpu.VMEM((SUBLANES, LRU_WIDTH), F32),
                        pltpu.VMEM((1, LRU_WIDTH), F32)],
        compiler_params=pltpu.CompilerParams(
            dimension_semantics=("arbitrary", "arbitrary"), vmem_limit_bytes=VMEM_LIMIT),
        name="mix_in",
    )(h, p["mix_pre_g"], p["w_in"], p["conv_w"], p["conv_b"], p["wax"], p["b_a"], p["b_x"],
      p["lam"], p["q_g"], p["w_uq"], p["kv_g"], p["w_uk"], p["w_uv"], p["lru_out_g"], *rope,
      xprev0, hcar0)


def _group_max(s):
    m = s[0:SUBLANES]
    for r in range(1, s.shape[0] // SUBLANES):
        m = jnp.maximum(m, s[r * SUBLANES:(r + 1) * SUBLANES])
    return m


def _sublane_allmax(m):
    for shift in (1, 2, 4):
        m = jnp.maximum(m, pltpu.roll(m, shift, 0))
    return m


def _attn_kernel(has_prefix, qt_ref, k_ref, vt_ref, *rest):
    if has_prefix:
        kpre_ref, vtpre_ref = rest[:2]
        rest = rest[2:]
        spre_ref = rest[-1]
        rest = rest[:-1]
    o_ref, m_ref, l_ref, acc_ref, s0_ref, s1_ref, bm0_ref, bm1_ref = rest
    qi = pl.program_id(1)
    nq = pl.num_programs(1)
    tq = o_ref.shape[1]
    tk = s0_ref.shape[1]
    s_refs = (s0_ref, s1_ref)
    bm_refs = (bm0_ref, bm1_ref)
    ones_rows = 2 * SUBLANES

    def with_ones(vb):
        return jnp.concatenate([vb, jnp.ones((ones_rows, vb.shape[1]), BF16)], axis=0)

    m_ref[...] = jnp.full(m_ref.shape, NEG_BIG, F32)
    l_ref[...] = jnp.zeros(l_ref.shape, F32)
    acc_ref[...] = jnp.zeros(acc_ref.shape, F32)

    def q_head(g, tile):
        half = QK_ROPE // 2
        r1 = Q_NOPE_ALL + g * half
        r2 = r1 + Q_HALF_ALL
        cols = pl.ds(pl.multiple_of(tile * tq, tq), tq)
        return jnp.concatenate(
            [qt_ref[g * QK_NOPE:(g + 1) * QK_NOPE, cols], qt_ref[r1:r1 + half, cols],
             qt_ref[r2:r2 + half, cols], jnp.zeros((HEAD_PAD - QK_NOPE - QK_ROPE, tq), BF16)],
            axis=0)

    def scores(j, g, slot, tile):
        off = pl.multiple_of(j * tk, tk)
        s = jnp.dot(k_ref[pl.ds(off, tk), g * HEAD_PAD:(g + 1) * HEAD_PAD], q_head(g, tile),
                    preferred_element_type=F32)
        s_refs[slot][g] = s
        bm_refs[slot][g] = _group_max(s)

    def first_scores(g, tile):
        scores(0, g, 0, tile)
        if has_prefix:
            spre_ref[g] = jnp.dot(kpre_ref[:, g * HEAD_PAD:(g + 1) * HEAD_PAD], q_head(g, tile),
                                  preferred_element_type=F32)

    def softmax_pv(j, g, slot, masked):
        s = s_refs[slot][g]
        if masked:
            key = lax.broadcasted_iota(jnp.int32, s.shape, 0)
            qry = lax.broadcasted_iota(jnp.int32, s.shape, 1)
            s = jnp.where(key <= qry, s, NEG_BIG)
            bm = _group_max(s)
            if has_prefix:
                s_pre = spre_ref[g]
                bm = jnp.maximum(bm, _group_max(s_pre))
        else:
            bm = bm_refs[slot][g]
        m_old = m_ref[g]
        m_new = jnp.maximum(m_old, _sublane_allmax(bm))
        alpha = jnp.exp2(m_old - m_new)
        p = jnp.exp2(s - jnp.concatenate([m_new] * (tk // SUBLANES), axis=0))
        m_ref[g] = m_new
        off = pl.multiple_of(j * tk, tk)
        vb = with_ones(vt_ref[g * V_DIM:(g + 1) * V_DIM, pl.ds(off, tk)])
        pv = jnp.dot(vb, p.astype(BF16), preferred_element_type=F32)
        if masked and has_prefix:
            p_pre = jnp.exp2(s_pre - jnp.concatenate([m_new] * (N_META // SUBLANES), axis=0))
            p_pre = jnp.concatenate(
                [p_pre.astype(BF16), jnp.zeros((LANES - N_META, tq), BF16)], axis=0)
            pv = pv + jnp.dot(with_ones(vtpre_ref[g * V_DIM:(g + 1) * V_DIM, :]), p_pre,
                              preferred_element_type=F32)
        acc_ref[g] = (jnp.concatenate([alpha] * (V_DIM // SUBLANES), axis=0) * acc_ref[g]
                      + pv[:V_DIM])
        l_ref[g] = alpha * l_ref[g] + pv[V_DIM:V_DIM + SUBLANES]

    def step(j_next, slot_next, j, slot):
        for g in range(MLA_HEADS):
            scores(j_next, g, slot_next, qi)
            softmax_pv(j, g, slot, False)

    def last_step(slot):
        nxt = jnp.minimum(qi + 1, nq - 1)
        for g in range(MLA_HEADS):
            softmax_pv(qi, g, slot, True)
            first_scores(g, nxt)

    @pl.when(qi == 0)
    def _():
        for g in range(MLA_HEADS):
            first_scores(g, 0)

    def pair(i, carry):
        j = 2 * i
        step(j + 1, 1, j, 0)
        step(j + 2, 0, j + 1, 1)
        return carry

    lax.fori_loop(0, qi // 2, pair, 0)

    @pl.when(qi % 2 == 0)
    def _():
        last_step(0)

    @pl.when(qi % 2 == 1)
    def _():
        step(qi, 1, qi - 1, 0)
        last_step(1)

    for g in range(MLA_HEADS):
        inv_l = 1.0 / jnp.concatenate([l_ref[g]] * (V_DIM // SUBLANES), axis=0)
        o_ref[g * V_DIM:(g + 1) * V_DIM, :] = acc_ref[g] * inv_l


def _attention(qt, k, vt, prefix=None):
    bsz, _, tp = qt.shape
    tq = tk = min(SEQ_TILE, tp)
    vdim = MLA_HEADS * V_DIM
    stat = pltpu.VMEM((MLA_HEADS, SUBLANES, tq), F32)
    has_prefix = prefix is not None
    prefix_specs = ([_const_spec((N_META, QK_WIDTH)), _const_spec((vdim, LANES))]
                    if has_prefix else [])
    return pl.pallas_call(
        functools.partial(_attn_kernel, has_prefix),
        grid=(bsz, tp // tq),
        in_specs=[
            pl.BlockSpec((None, QT_ROWS, tp), lambda b, i: (b, 0, 0)),
            pl.BlockSpec((None, tp, QK_WIDTH), lambda b, i: (b, 0, 0)),
            pl.BlockSpec((None, vdim, tp), lambda b, i: (b, 0, 0)),
        ] + prefix_specs,
        out_specs=pl.BlockSpec((None, vdim, tq), lambda b, i: (b, 0, i)),
        out_shape=jax.ShapeDtypeStruct((bsz, vdim, tp), F32),
        scratch_shapes=[stat, stat, pltpu.VMEM((MLA_HEADS, V_DIM, tq), F32),
                        pltpu.VMEM((MLA_HEADS, tk, tq), F32),
                        pltpu.VMEM((MLA_HEADS, tk, tq), F32), stat, stat]
        + ([pltpu.VMEM((MLA_HEADS, N_META, tq), F32)] if has_prefix else []),
        compiler_params=pltpu.CompilerParams(
            dimension_semantics=("arbitrary", "arbitrary"), vmem_limit_bytes=VMEM_LIMIT),
        name="attention",
    )(qt, k, vt, *(prefix or ()))


def _rope_tables(tp):
    pos = jnp.arange(tp, dtype=F32)
    inv_freq = 1.0 / (ROPE_THETA ** (jnp.arange(0, QK_ROPE, 2, dtype=F32) / QK_ROPE))
    ang = pos[:, None] * inv_freq[None, :]
    cos, sin = jnp.cos(ang), jnp.sin(ang)
    half = QK_ROPE // 2
    zeros = lambda n: jnp.zeros((tp, n), F32)
    pad = HEAD_PAD - QK_NOPE - QK_ROPE
    c = jnp.concatenate([jnp.ones((tp, QK_NOPE), F32), cos, cos, zeros(pad)], axis=1)
    sa = jnp.concatenate([zeros(QK_NOPE + half), sin, zeros(pad)], axis=1)
    sb = jnp.concatenate([zeros(QK_NOPE), -sin, zeros(half + pad)], axis=1)
    return c, sa, sb, jnp.tile(cos, (1, MLA_HEADS)), jnp.tile(sin, (1, MLA_HEADS))


def _block_diag(w):
    depth = w.shape[0]
    per = LRU_HEADS // 2
    w5 = w.reshape(depth, 2, per, LRU_HEAD_DIM, LRU_HEAD_DIM)
    bd = jnp.einsum("dnhij,hg->dnhigj", w5, jnp.eye(per, dtype=w.dtype))
    return bd.reshape(depth, 2, per * LRU_HEAD_DIM, per * LRU_HEAD_DIM)


def _row(g):
    return g.reshape(g.shape[0], 1, g.shape[1])


@jax.jit
def kernel(x, meta_tokens, ffn1_pre_g, ffn1_w_gate, ffn1_w_up, ffn1_w_down, ffn1_post_g, mix_pre_g, w_in, lru_conv_w, lru_conv_b, lru_w_a, lru_b_a, lru_w_x, lru_b_x, lru_lambda, mla_q_norm_g, mla_w_uq, mla_kv_norm_g, mla_w_ukv, lru_out_g, mla_out_g, w_out, mix_post_g, ffn2_pre_g, ffn2_w_gate, ffn2_w_up, ffn2_w_down, ffn2_post_g):
    bsz, seq, _ = x.shape
    depth = w_in.shape[0]
    assert seq % SEQ_TILE == 0 and seq % MIX_TILE == 0
    h = x
    hm = jnp.concatenate([meta_tokens.astype(x.dtype),
                          jnp.zeros((META_ROWS - N_META, D_MODEL), x.dtype)], axis=0)[None]
    rope_all = _rope_tables(N_META + max(seq, META_ROWS))
    rope_meta = tuple(t[:META_ROWS] for t in rope_all)
    rope_main = tuple(t[N_META:N_META + seq] for t in rope_all)
    zero_hist = jnp.zeros((SUBLANES, LRU_WIDTH), F32)
    zero_state = jnp.zeros((1, LRU_WIDTH), F32)

    split = 2 * LRU_WIDTH + Q_LORA + KV_LORA
    zcols = lambda n: jnp.zeros((depth, D_MODEL, n), w_in.dtype)
    w_in_p = jnp.concatenate(
        [w_in[..., :split], zcols(QK_NOPE), w_in[..., split:],
         zcols(HEAD_PAD - QK_NOPE - QK_ROPE)], axis=-1).astype(BF16)
    w_uq4 = mla_w_uq.reshape(depth, Q_LORA, MLA_HEADS, QK_NOPE + QK_ROPE)
    half = QK_ROPE // 2
    w_uq_p = jnp.concatenate(
        [w_uq4[..., :QK_NOPE].reshape(depth, Q_LORA, Q_NOPE_ALL),
         w_uq4[..., QK_NOPE:QK_NOPE + half].reshape(depth, Q_LORA, Q_HALF_ALL),
         w_uq4[..., QK_NOPE + half:].reshape(depth, Q_LORA, Q_HALF_ALL)], axis=-1)
    w_ukv4 = mla_w_ukv.reshape(depth, KV_LORA, MLA_HEADS, QK_NOPE + V_DIM)
    w_uk_p = jnp.pad(w_ukv4[..., :QK_NOPE], ((0, 0), (0, 0), (0, 0), (0, HEAD_PAD - QK_NOPE)))
    mixp = {
        "mix_pre_g": _row(mix_pre_g),
        "w_in": w_in_p,
        "conv_w": lru_conv_w,
        "conv_b": _row(lru_conv_b),
        "wax": jnp.concatenate([_block_diag(lru_w_a), _block_diag(lru_w_x)], axis=-1).astype(BF16),
        "b_a": _row(lru_b_a),
        "b_x": _row(lru_b_x),
        "lam": _row(lru_lambda),
        "q_g": _row(mla_q_norm_g),
        "w_uq": w_uq_p.astype(BF16),
        "kv_g": _row(mla_kv_norm_g),
        "w_uk": w_uk_p.reshape(depth, KV_LORA, QK_WIDTH).astype(BF16),
        "w_uv": w_ukv4[..., QK_NOPE:].reshape(depth, KV_LORA, MLA_HEADS * V_DIM).astype(BF16),
        "lru_out_g": _row(lru_out_g),
        "mla_out_g": mla_out_g.reshape(depth, MLA_HEADS * V_DIM, 1),
        "w_out": w_out.astype(BF16),
        "mix_post_g": _row(mix_post_g),
    }
    ffn1 = (_row(ffn1_pre_g), ffn1_w_gate.astype(BF16), ffn1_w_up.astype(BF16),
            ffn1_w_down.astype(BF16), _row(ffn1_post_g))
    ffn2 = (_row(ffn2_pre_g), ffn2_w_gate.astype(BF16), ffn2_w_up.astype(BF16),
            ffn2_w_down.astype(BF16), _row(ffn2_post_g))

    lane = lax.broadcasted_iota(jnp.int32, (MLA_HEADS * V_DIM, META_ROWS), 1)
    for layer in range(depth):
        hm = _ffn(hm, layer, *ffn1)
        h = _ffn(h, layer, *ffn1)
        ylru_m, qt_m, k_m, vt_m, xr_m, hs_m = _mix_in(
            hm, layer, mixp, rope_meta, zero_hist, zero_state, emit_state=True)
        ylru, qt, k, vt = _mix_in(h, layer, mixp, rope_main,
                                  xr_m[0, N_META - SUBLANES:N_META], hs_m[0, N_META - 1:N_META])
        prefix = (k_m[0, :N_META], jnp.where(lane < N_META, vt_m[0], 0).astype(BF16))
        h = _ffn(h, layer, *ffn2, mixer=(ylru, _attention(qt, k, vt, prefix), mixp))
        if layer + 1 < depth:
            hm = _ffn(hm, layer, *ffn2, mixer=(ylru_m, _attention(qt_m, k_m, vt_m), mixp))
    return h
```

```python
import functools

import jax
import jax.numpy as jnp
from jax import lax
from jax.experimental import pallas as pl
from jax.experimental.pallas import tpu as pltpu

D_MODEL = 1024
N_META = 16
LRU_WIDTH = 512
LRU_HEADS = 8
LRU_HEAD_DIM = 64
CONV_WIDTH = 4
LRU_C = 8.0
MLA_HEADS = 8
QK_NOPE = 64
QK_ROPE = 32
V_DIM = 64
Q_LORA = 384
KV_LORA = 256
D_FF = 2816
ROPE_THETA = 10000.0
EPS = 1e-6

LANES = 128
SUBLANES = 8
HEAD_PAD = LANES
QK_WIDTH = MLA_HEADS * HEAD_PAD
Q_NOPE_ALL = MLA_HEADS * QK_NOPE
Q_HALF_ALL = MLA_HEADS * QK_ROPE // 2
QT_ROWS = Q_NOPE_ALL + 2 * Q_HALF_ALL
IN_COLS_PAD = 2 * LRU_WIDTH + Q_LORA + KV_LORA + HEAD_PAD
SEQ_TILE = 256
MIX_TILE = 1024
MIX_SUB = 256
META_ROWS = LANES
FFN_ROWS = 512
FFN_SUBTILES = 2
FF_CHUNK = 256
NEG_BIG = -1e30
TINY = 1e-30
LOG2_E = 1.4426950408889634
VMEM_LIMIT = 52 * 1024 * 1024

F32 = jnp.float32
BF16 = jnp.bfloat16


def _rms(x, g):
    return x * lax.rsqrt(jnp.mean(x * x, axis=-1, keepdims=True) + EPS) * g


def _const_spec(shape, layer=None):
    if layer is None:
        idx = lambda *_: (0,) * len(shape)
        return pl.BlockSpec(shape, idx, pipeline_mode=pl.Buffered(1))
    idx = lambda *_: (layer,) + (0,) * len(shape)
    return pl.BlockSpec((None,) + shape, idx, pipeline_mode=pl.Buffered(1))


def _mix_out_rows(h, ylru, ot, gmla, wout_ref, gpost):
    ms = jnp.mean(ot * ot, axis=0, keepdims=True)
    on = (ot * lax.rsqrt(ms + EPS) * gmla).T.astype(BF16)
    y = jnp.dot(ylru, wout_ref[:LRU_WIDTH, :], preferred_element_type=F32)
    y = y + jnp.dot(on, wout_ref[LRU_WIDTH:, :], preferred_element_type=F32)
    return h + _rms(y, gpost)


def _ffn_kernel(after_mixer, emit_norm, h_ref, *refs):
    if after_mixer:
        ylru_ref, ot_ref, gmla_ref, wout_ref, gmix_ref = refs[:5]
        refs = refs[5:]
    if emit_norm:
        gpre_ref, wg_ref, wu_ref, wd_ref, gpost_ref, gnext_ref, o_ref, u_ref, a_ref = refs
    else:
        gpre_ref, wg_ref, wu_ref, wd_ref, gpost_ref, o_ref, a_ref = refs
    rows = h_ref.shape[0]
    sub = rows // FFN_SUBTILES if after_mixer and rows % (FFN_SUBTILES * LANES) == 0 else rows
    parts = [slice(i * sub, (i + 1) * sub) for i in range(rows // sub)]
    xs = []
    for r in parts:
        x = h_ref[r, :]
        if after_mixer:
            x = _mix_out_rows(x, ylru_ref[r, :], ot_ref[:, r], gmla_ref[...], wout_ref,
                              gmix_ref[...])
        xs.append(x)
    for r, x in zip(parts, xs):
        u = _rms(x, gpre_ref[...]).astype(BF16)
        for c in range(D_FF // FF_CHUNK):
            sl = slice(c * FF_CHUNK, (c + 1) * FF_CHUNK)
            gate = jnp.dot(u, wg_ref[:, sl], preferred_element_type=F32)
            up = jnp.dot(u, wu_ref[:, sl], preferred_element_type=F32)
            a_ref[r, sl] = (gate * jax.nn.sigmoid(gate) * up).astype(BF16)
    for r, x in zip(parts, xs):
        f = jnp.dot(a_ref[r, :], wd_ref[...], preferred_element_type=F32)
        out = x + 0.5 * _rms(f, gpost_ref[...])
        o_ref[r, :] = out
        if emit_norm:
            u_ref[r, :] = _rms(out, gnext_ref[...]).astype(BF16)


def _ffn(h, layer, gpre, wg, wu, wd, gpost, mixer=None, next_gain=None):
    bsz, tp, _ = h.shape
    tile = min(FFN_ROWS, tp)
    assert tp % tile == 0
    row_spec = pl.BlockSpec((None, tile, D_MODEL), lambda b, t: (b, t, 0))
    mixer_specs, mixer_args = [], []
    if mixer is not None:
        ylru, ot, p = mixer
        vdim = MLA_HEADS * V_DIM
        mixer_specs = [
            pl.BlockSpec((None, tile, LRU_WIDTH), lambda b, t: (b, t, 0)),
            pl.BlockSpec((None, vdim, tile), lambda b, t: (b, 0, t)),
            _const_spec((vdim, 1), layer),
            _const_spec((D_MODEL, D_MODEL), layer),
            _const_spec((1, D_MODEL), layer),
        ]
        mixer_args = [ylru, ot, p["mla_out_g"], p["w_out"], p["mix_post_g"]]
    emit_norm = next_gain is not None
    norm_specs = [_const_spec((1, D_MODEL), layer)] if emit_norm else []
    norm_args = [next_gain] if emit_norm else []
    out_specs = [row_spec, row_spec] if emit_norm else row_spec
    out_shape = jax.ShapeDtypeStruct(h.shape, F32)
    if emit_norm:
        out_shape = [out_shape, jax.ShapeDtypeStruct(h.shape, BF16)]
    return pl.pallas_call(
        functools.partial(_ffn_kernel, mixer is not None, emit_norm),
        grid=(bsz, tp // tile),
        in_specs=[row_spec] + mixer_specs + [
            _const_spec((1, D_MODEL), layer),
            _const_spec((D_MODEL, D_FF), layer),
            _const_spec((D_MODEL, D_FF), layer),
            _const_spec((D_FF, D_MODEL), layer),
            _const_spec((1, D_MODEL), layer),
        ] + norm_specs,
        out_specs=out_specs,
        out_shape=out_shape,
        scratch_shapes=[pltpu.VMEM((tile, D_FF), BF16)],
        compiler_params=pltpu.CompilerParams(
            dimension_semantics=("arbitrary", "arbitrary"), vmem_limit_bytes=VMEM_LIMIT),
        name="ffn_mix" if mixer is not None else "ffn",
    )(h, *mixer_args, gpre, wg, wu, wd, gpost, *norm_args)


def _rope(x, c, sa, sb):
    n = x.shape[-1]
    half = QK_ROPE // 2
    return x * c + pltpu.roll(x, half, 1) * sa + pltpu.roll(x, n - half, 1) * sb


def _mix_in_kernel(emit_state, u_ref, win_ref, cw_ref, cb_ref, wax_ref, ba_ref, bx_ref,
                   lam_ref, gq_ref, wuq_ref, gkv_ref, wuk_ref, wuv_ref, glru_ref,
                   rc_ref, rsa_ref, rsb_ref, cos8_ref, sin8_ref, xprev0_ref, hcar0_ref,
                   ylru_ref, qt_ref, k_ref, vt_ref, *rest):
    if emit_state:
        xr_out_ref, hs_out_ref, xprev_ref, hcar_ref = rest
    else:
        xprev_ref, hcar_ref = rest
    tt = u_ref.shape[0]

    @pl.when(pl.program_id(1) == 0)
    def _():
        xprev_ref[...] = xprev0_ref[...]
        hcar_ref[...] = hcar0_ref[...]

    sub = min(MIX_SUB, tt)
    half = LRU_WIDTH // 2
    nl = -lam_ref[...]
    softplus = jnp.maximum(nl, 0.0) + jnp.log1p(jnp.exp(-jnp.abs(nl)))
    decay = (-LRU_C * LOG2_E) * softplus
    gq_scaled = gq_ref[...] * (LOG2_E * (QK_NOPE + QK_ROPE) ** -0.5)
    tile_heads = lambda t: jnp.concatenate([t] * MLA_HEADS, axis=1)

    def project(i):
        return jnp.dot(u_ref[i * sub:(i + 1) * sub, :], win_ref[...], preferred_element_type=F32)

    xprev = xprev_ref[...]
    carry = hcar_ref[...]
    z_next = project(0)
    for i in range(tt // sub):
        z = z_next
        if (i + 1) * sub < tt:
            z_next = project(i + 1)
        rows = slice(i * sub, (i + 1) * sub)
        xr = z[:, 0:LRU_WIDTH]
        gr = z[:, LRU_WIDTH:2 * LRU_WIDTH]
        cq = z[:, 2 * LRU_WIDTH:2 * LRU_WIDTH + Q_LORA]
        ckv = z[:, 2 * LRU_WIDTH + Q_LORA:2 * LRU_WIDTH + Q_LORA + KV_LORA]
        krt = z[:, 2 * LRU_WIDTH + Q_LORA + KV_LORA:]

        xe = jnp.concatenate([xprev, xr], axis=0)
        xc = xr * cw_ref[CONV_WIDTH - 1:CONV_WIDTH, :] + cb_ref[...]
        for j in range(1, CONV_WIDTH):
            w_j = cw_ref[CONV_WIDTH - 1 - j:CONV_WIDTH - j, :]
            xc = xc + pltpu.roll(xe, j, 0)[SUBLANES:] * w_j
        xprev = xr[sub - SUBLANES:]

        xcb = xc.astype(BF16)
        ri0 = jnp.dot(xcb[:, :half], wax_ref[0], preferred_element_type=F32)
        ri1 = jnp.dot(xcb[:, half:], wax_ref[1], preferred_element_type=F32)
        r = jax.nn.sigmoid(jnp.concatenate([ri0[:, :half], ri1[:, :half]], axis=1) + ba_ref[...])
        g_in = jax.nn.sigmoid(
            jnp.concatenate([ri0[:, half:], ri1[:, half:]], axis=1) + bx_ref[...])
        a = jnp.exp2(r * decay)
        gap = 1.0 - a * a
        b = (gap * lax.rsqrt(jnp.maximum(gap, TINY))) * (g_in * xc)

        grouped = (sub // SUBLANES, SUBLANES, LRU_WIDTH)
        a = a.reshape(grouped)
        b = b.reshape(grouped)
        row = lax.broadcasted_iota(jnp.int32, grouped, 1)
        s = 1
        while s < SUBLANES:
            valid = row >= s
            b = a * jnp.where(valid, pltpu.roll(b, s, 1), 0.0) + b
            a = a * jnp.where(valid, pltpu.roll(a, s, 1), 1.0)
            s *= 2
        a = a.reshape(sub, LRU_WIDTH)
        b = b.reshape(sub, LRU_WIDTH)
        groups = []
        for gidx in range(sub // SUBLANES):
            sl = slice(gidx * SUBLANES, (gidx + 1) * SUBLANES)
            hg = b[sl] + a[sl] * carry
            carry = hg[SUBLANES - 1:SUBLANES]
            groups.append(hg)
        hs = jnp.concatenate(groups, axis=0)
        if emit_state:
            xr_out_ref[rows, :] = xr
            hs_out_ref[rows, :] = hs

        cg = 0.7978845608028654
        half_gr = 0.5 * gr
        gelu = half_gr + half_gr * jnp.tanh(gr * (cg + (cg * 0.044715) * (gr * gr)))
        ylru_ref[rows, :] = _rms(hs * gelu, glru_ref[...]).astype(BF16)

        rc = rc_ref[rows, :]
        rsa = rsa_ref[rows, :]
        rsb = rsb_ref[rows, :]
        cqn = _rms(cq, gq_scaled).astype(BF16)
        q = jnp.dot(cqn, wuq_ref[...], preferred_element_type=F32)
        q1 = q[:, Q_NOPE_ALL:Q_NOPE_ALL + Q_HALF_ALL]
        q2 = q[:, Q_NOPE_ALL + Q_HALF_ALL:]
        cos8 = cos8_ref[rows, :]
        sin8 = sin8_ref[rows, :]
        q = jnp.concatenate([q[:, :Q_NOPE_ALL], q1 * cos8 - q2 * sin8, q2 * cos8 + q1 * sin8],
                            axis=1)
        qt_ref[:, rows] = q.T.astype(BF16)

        ckvn = _rms(ckv, gkv_ref[...]).astype(BF16)
        kn = jnp.dot(ckvn, wuk_ref[...], preferred_element_type=F32)
        kr = _rope(krt, rc, rsa, rsb)
        k_ref[rows, :] = (kn + tile_heads(kr)).astype(BF16)
        v = jnp.dot(ckvn, wuv_ref[...], preferred_element_type=F32)
        vt_ref[:, rows] = v.T.astype(BF16)
    xprev_ref[...] = xprev
    hcar_ref[...] = carry


def _mix_in(u, layer, p, rope, xprev0, hcar0, emit_state=False):
    bsz, tp, _ = u.shape
    tt = min(MIX_TILE, tp)
    nt = tp // tt
    tbl_spec = pl.BlockSpec((tt, HEAD_PAD), lambda b, t: (t, 0))
    seq_spec = pl.BlockSpec((None, tt, LRU_WIDTH), lambda b, t: (b, t, 0))
    state_specs = [seq_spec, seq_spec] if emit_state else []
    state_shapes = [jax.ShapeDtypeStruct((bsz, tp, LRU_WIDTH), F32)] * 2 if emit_state else []
    return pl.pallas_call(
        functools.partial(_mix_in_kernel, emit_state),
        grid=(bsz, nt),
        in_specs=[
            pl.BlockSpec((None, tt, D_MODEL), lambda b, t: (b, t, 0)),
            _const_spec((D_MODEL, IN_COLS_PAD), layer),
            _const_spec((CONV_WIDTH, LRU_WIDTH), layer),
            _const_spec((1, LRU_WIDTH), layer),
            _const_spec((2, LRU_WIDTH // 2, LRU_WIDTH), layer),
            _const_spec((1, LRU_WIDTH), layer),
            _const_spec((1, LRU_WIDTH), layer),
            _const_spec((1, LRU_WIDTH), layer),
            _const_spec((1, Q_LORA), layer),
            _const_spec((Q_LORA, QT_ROWS), layer),
            _const_spec((1, KV_LORA), layer),
            _const_spec((KV_LORA, QK_WIDTH), layer),
            _const_spec((KV_LORA, MLA_HEADS * V_DIM), layer),
            _const_spec((1, LRU_WIDTH), layer),
            tbl_spec, tbl_spec, tbl_spec, tbl_spec, tbl_spec,
            _const_spec((SUBLANES, LRU_WIDTH)),
            _const_spec((1, LRU_WIDTH)),
        ],
        out_specs=[
            seq_spec,
            pl.BlockSpec((None, QT_ROWS, tt), lambda b, t: (b, 0, t)),
            pl.BlockSpec((None, tt, QK_WIDTH), lambda b, t: (b, t, 0)),
            pl.BlockSpec((None, MLA_HEADS * V_DIM, tt), lambda b, t: (b, 0, t)),
        ] + state_specs,
        out_shape=[
            jax.ShapeDtypeStruct((bsz, tp, LRU_WIDTH), BF16),
            jax.ShapeDtypeStruct((bsz, QT_ROWS, tp), BF16),
            jax.ShapeDtypeStruct((bsz, tp, QK_WIDTH), BF16),
            jax.ShapeDtypeStruct((bsz, MLA_HEADS * V_DIM, tp), BF16),
        ] + state_shapes,
        scratch_shapes=[pltpu.VMEM((SUBLANES, LRU_WIDTH), F32),
                        pltpu.VMEM((1, LRU_WIDTH), F32)],
        compiler_params=pltpu.CompilerParams(
            dimension_semantics=("arbitrary", "arbitrary"), vmem_limit_bytes=VMEM_LIMIT),
        name="mix_in",
    )(u, p["w_in"], p["conv_w"], p["conv_b"], p["wax"], p["b_a"], p["b_x"],
      p["lam"], p["q_g"], p["w_uq"], p["kv_g"], p["w_uk"], p["w_uv"], p["lru_out_g"], *rope,
      xprev0, hcar0)


def _group_max(s):
    m = s[0:SUBLANES]
    for r in range(1, s.shape[0] // SUBLANES):
        m = jnp.maximum(m, s[r * SUBLANES:(r + 1) * SUBLANES])
    return m


def _sublane_allmax(m):
    for shift in (1, 2, 4):
        m = jnp.maximum(m, pltpu.roll(m, shift, 0))
    return m


def _attn_kernel(has_prefix, qt_ref, k_ref, vt_ref, *rest):
    if has_prefix:
        kpre_ref, vtpre_ref = rest[:2]
        rest = rest[2:]
        spre_ref = rest[-1]
        rest = rest[:-1]
    o_ref, m_ref, l_ref, acc_ref, s0_ref, s1_ref, bm0_ref, bm1_ref = rest
    qi = pl.program_id(1)
    nq = pl.num_programs(1)
    tq = o_ref.shape[1]
    tk = s0_ref.shape[1]
    s_refs = (s0_ref, s1_ref)
    bm_refs = (bm0_ref, bm1_ref)
    ones_rows = 2 * SUBLANES

    def with_ones(vb):
        return jnp.concatenate([vb, jnp.ones((ones_rows, vb.shape[1]), BF16)], axis=0)

    m_ref[...] = jnp.full(m_ref.shape, NEG_BIG, F32)
    l_ref[...] = jnp.zeros(l_ref.shape, F32)
    acc_ref[...] = jnp.zeros(acc_ref.shape, F32)

    def q_head(g, tile):
        half = QK_ROPE // 2
        r1 = Q_NOPE_ALL + g * half
        r2 = r1 + Q_HALF_ALL
        cols = pl.ds(pl.multiple_of(tile * tq, tq), tq)
        return jnp.concatenate(
            [qt_ref[g * QK_NOPE:(g + 1) * QK_NOPE, cols], qt_ref[r1:r1 + half, cols],
             qt_ref[r2:r2 + half, cols], jnp.zeros((HEAD_PAD - QK_NOPE - QK_ROPE, tq), BF16)],
            axis=0)

    def scores(j, g, slot, tile):
        off = pl.multiple_of(j * tk, tk)
        s = jnp.dot(k_ref[pl.ds(off, tk), g * HEAD_PAD:(g + 1) * HEAD_PAD], q_head(g, tile),
                    preferred_element_type=F32)
        s_refs[slot][g] = s
        bm_refs[slot][g] = _group_max(s)

    def first_scores(g, tile):
        scores(0, g, 0, tile)
        if has_prefix:
            spre_ref[g] = jnp.dot(kpre_ref[:, g * HEAD_PAD:(g + 1) * HEAD_PAD], q_head(g, tile),
                                  preferred_element_type=F32)

    def softmax_pv(j, g, slot, masked):
        s = s_refs[slot][g]
        if masked:
            key = lax.broadcasted_iota(jnp.int32, s.shape, 0)
            qry = lax.broadcasted_iota(jnp.int32, s.shape, 1)
            s = jnp.where(key <= qry, s, NEG_BIG)
            bm = _group_max(s)
            if has_prefix:
                s_pre = spre_ref[g]
                bm = jnp.maximum(bm, _group_max(s_pre))
        else:
            bm = bm_refs[slot][g]
        m_old = m_ref[g]
        m_new = jnp.maximum(m_old, _sublane_allmax(bm))
        alpha = jnp.exp2(m_old - m_new)
        p = jnp.exp2(s - jnp.concatenate([m_new] * (tk // SUBLANES), axis=0))
        m_ref[g] = m_new
        off = pl.multiple_of(j * tk, tk)
        vb = with_ones(vt_ref[g * V_DIM:(g + 1) * V_DIM, pl.ds(off, tk)])
        pv = jnp.dot(vb, p.astype(BF16), preferred_element_type=F32)
        if masked and has_prefix:
            p_pre = jnp.exp2(s_pre - jnp.concatenate([m_new] * (N_META // SUBLANES), axis=0))
            p_pre = jnp.concatenate(
                [p_pre.astype(BF16), jnp.zeros((LANES - N_META, tq), BF16)], axis=0)
            pv = pv + jnp.dot(with_ones(vtpre_ref[g * V_DIM:(g + 1) * V_DIM, :]), p_pre,
                              preferred_element_type=F32)
        acc_ref[g] = (jnp.concatenate([alpha] * (V_DIM // SUBLANES), axis=0) * acc_ref[g]
                      + pv[:V_DIM])
        l_ref[g] = alpha * l_ref[g] + pv[V_DIM:V_DIM + SUBLANES]

    def step(j_next, slot_next, j, slot):
        for g in range(MLA_HEADS):
            scores(j_next, g, slot_next, qi)
            softmax_pv(j, g, slot, False)

    def last_step(slot):
        nxt = jnp.minimum(qi + 1, nq - 1)
        for g in range(MLA_HEADS):
            softmax_pv(qi, g, slot, True)
            first_scores(g, nxt)

    @pl.when(qi == 0)
    def _():
        for g in range(MLA_HEADS):
            first_scores(g, 0)

    def pair(i, carry):
        j = 2 * i
        step(j + 1, 1, j, 0)
        step(j + 2, 0, j + 1, 1)
        return carry

    lax.fori_loop(0, qi // 2, pair, 0)

    @pl.when(qi % 2 == 0)
    def _():
        last_step(0)

    @pl.when(qi % 2 == 1)
    def _():
        step(qi, 1, qi - 1, 0)
        last_step(1)

    for g in range(MLA_HEADS):
        inv_l = 1.0 / jnp.concatenate([l_ref[g]] * (V_DIM // SUBLANES), axis=0)
        o_ref[g * V_DIM:(g + 1) * V_DIM, :] = acc_ref[g] * inv_l


def _attention(qt, k, vt, prefix=None):
    bsz, _, tp = qt.shape
    tq = tk = min(SEQ_TILE, tp)
    vdim = MLA_HEADS * V_DIM
    stat = pltpu.VMEM((MLA_HEADS, SUBLANES, tq), F32)
    has_prefix = prefix is not None
    prefix_specs = ([_const_spec((N_META, QK_WIDTH)), _const_spec((vdim, LANES))]
                    if has_prefix else [])
    return pl.pallas_call(
        functools.partial(_attn_kernel, has_prefix),
        grid=(bsz, tp // tq),
        in_specs=[
            pl.BlockSpec((None, QT_ROWS, tp), lambda b, i: (b, 0, 0)),
            pl.BlockSpec((None, tp, QK_WIDTH), lambda b, i: (b, 0, 0)),
            pl.BlockSpec((None, vdim, tp), lambda b, i: (b, 0, 0)),
        ] + prefix_specs,
        out_specs=pl.BlockSpec((None, vdim, tq), lambda b, i: (b, 0, i)),
        out_shape=jax.ShapeDtypeStruct((bsz, vdim, tp), F32),
        scratch_shapes=[stat, stat, pltpu.VMEM((MLA_HEADS, V_DIM, tq), F32),
                        pltpu.VMEM((MLA_HEADS, tk, tq), F32),
                        pltpu.VMEM((MLA_HEADS, tk, tq), F32), stat, stat]
        + ([pltpu.VMEM((MLA_HEADS, N_META, tq), F32)] if has_prefix else []),
        compiler_params=pltpu.CompilerParams(
            dimension_semantics=("arbitrary", "arbitrary"), vmem_limit_bytes=VMEM_LIMIT),
        name="attention",
    )(qt, k, vt, *(prefix or ()))


def _rope_tables(tp):
    pos = jnp.arange(tp, dtype=F32)
    inv_freq = 1.0 / (ROPE_THETA ** (jnp.arange(0, QK_ROPE, 2, dtype=F32) / QK_ROPE))
    ang = pos[:, None] * inv_freq[None, :]
    cos, sin = jnp.cos(ang), jnp.sin(ang)
    half = QK_ROPE // 2
    zeros = lambda n: jnp.zeros((tp, n), F32)
    pad = HEAD_PAD - QK_NOPE - QK_ROPE
    c = jnp.concatenate([jnp.ones((tp, QK_NOPE), F32), cos, cos, zeros(pad)], axis=1)
    sa = jnp.concatenate([zeros(QK_NOPE + half), sin, zeros(pad)], axis=1)
    sb = jnp.concatenate([zeros(QK_NOPE), -sin, zeros(half + pad)], axis=1)
    return c, sa, sb, jnp.tile(cos, (1, MLA_HEADS)), jnp.tile(sin, (1, MLA_HEADS))


def _block_diag(w):
    depth = w.shape[0]
    per = LRU_HEADS // 2
    w5 = w.reshape(depth, 2, per, LRU_HEAD_DIM, LRU_HEAD_DIM)
    bd = jnp.einsum("dnhij,hg->dnhigj", w5, jnp.eye(per, dtype=w.dtype))
    return bd.reshape(depth, 2, per * LRU_HEAD_DIM, per * LRU_HEAD_DIM)


def _row(g):
    return g.reshape(g.shape[0], 1, g.shape[1])


@jax.jit
def kernel(x, meta_tokens, ffn1_pre_g, ffn1_w_gate, ffn1_w_up, ffn1_w_down, ffn1_post_g, mix_pre_g, w_in, lru_conv_w, lru_conv_b, lru_w_a, lru_b_a, lru_w_x, lru_b_x, lru_lambda, mla_q_norm_g, mla_w_uq, mla_kv_norm_g, mla_w_ukv, lru_out_g, mla_out_g, w_out, mix_post_g, ffn2_pre_g, ffn2_w_gate, ffn2_w_up, ffn2_w_down, ffn2_post_g):
    bsz, seq, _ = x.shape
    depth = w_in.shape[0]
    assert seq % SEQ_TILE == 0 and seq % MIX_TILE == 0
    h = x
    hm = jnp.concatenate([meta_tokens.astype(x.dtype),
                          jnp.zeros((META_ROWS - N_META, D_MODEL), x.dtype)], axis=0)[None]
    rope_all = _rope_tables(N_META + max(seq, META_ROWS))
    rope_meta = tuple(t[:META_ROWS] for t in rope_all)
    rope_main = tuple(t[N_META:N_META + seq] for t in rope_all)
    zero_hist = jnp.zeros((SUBLANES, LRU_WIDTH), F32)
    zero_state = jnp.zeros((1, LRU_WIDTH), F32)

    split = 2 * LRU_WIDTH + Q_LORA + KV_LORA
    zcols = lambda n: jnp.zeros((depth, D_MODEL, n), w_in.dtype)
    w_in_p = jnp.concatenate(
        [w_in[..., :split], zcols(QK_NOPE), w_in[..., split:],
         zcols(HEAD_PAD - QK_NOPE - QK_ROPE)], axis=-1).astype(BF16)
    w_uq4 = mla_w_uq.reshape(depth, Q_LORA, MLA_HEADS, QK_NOPE + QK_ROPE)
    half = QK_ROPE // 2
    w_uq_p = jnp.concatenate(
        [w_uq4[..., :QK_NOPE].reshape(depth, Q_LORA, Q_NOPE_ALL),
         w_uq4[..., QK_NOPE:QK_NOPE + half].reshape(depth, Q_LORA, Q_HALF_ALL),
         w_uq4[..., QK_NOPE + half:].reshape(depth, Q_LORA, Q_HALF_ALL)], axis=-1)
    w_ukv4 = mla_w_ukv.reshape(depth, KV_LORA, MLA_HEADS, QK_NOPE + V_DIM)
    w_uk_p = jnp.pad(w_ukv4[..., :QK_NOPE], ((0, 0), (0, 0), (0, 0), (0, HEAD_PAD - QK_NOPE)))
    mixp = {
        "mix_pre_g": _row(mix_pre_g),
        "w_in": w_in_p,
        "conv_w": lru_conv_w,
        "conv_b": _row(lru_conv_b),
        "wax": jnp.concatenate([_block_diag(lru_w_a), _block_diag(lru_w_x)], axis=-1).astype(BF16),
        "b_a": _row(lru_b_a),
        "b_x": _row(lru_b_x),
        "lam": _row(lru_lambda),
        "q_g": _row(mla_q_norm_g),
        "w_uq": w_uq_p.astype(BF16),
        "kv_g": _row(mla_kv_norm_g),
        "w_uk": w_uk_p.reshape(depth, KV_LORA, QK_WIDTH).astype(BF16),
        "w_uv": w_ukv4[..., QK_NOPE:].reshape(depth, KV_LORA, MLA_HEADS * V_DIM).astype(BF16),
        "lru_out_g": _row(lru_out_g),
        "mla_out_g": mla_out_g.reshape(depth, MLA_HEADS * V_DIM, 1),
        "w_out": w_out.astype(BF16),
        "mix_post_g": _row(mix_post_g),
    }
    ffn1 = (_row(ffn1_pre_g), ffn1_w_gate.astype(BF16), ffn1_w_up.astype(BF16),
            ffn1_w_down.astype(BF16), _row(ffn1_post_g))
    ffn2 = (_row(ffn2_pre_g), ffn2_w_gate.astype(BF16), ffn2_w_up.astype(BF16),
            ffn2_w_down.astype(BF16), _row(ffn2_post_g))

    lane = lax.broadcasted_iota(jnp.int32, (MLA_HEADS * V_DIM, META_ROWS), 1)
    for layer in range(depth):
        hm, um = _ffn(hm, layer, *ffn1, next_gain=mixp["mix_pre_g"])
        h, u = _ffn(h, layer, *ffn1, next_gain=mixp["mix_pre_g"])
        ylru_m, qt_m, k_m, vt_m, xr_m, hs_m = _mix_in(
            um, layer, mixp, rope_meta, zero_hist, zero_state, emit_state=True)
        ylru, qt, k, vt = _mix_in(u, layer, mixp, rope_main,
                                  xr_m[0, N_META - SUBLANES:N_META], hs_m[0, N_META - 1:N_META])
        prefix = (k_m[0, :N_META], jnp.where(lane < N_META, vt_m[0], 0).astype(BF16))
        h = _ffn(h, layer, *ffn2, mixer=(ylru, _attention(qt, k, vt, prefix), mixp))
        if layer + 1 < depth:
            hm = _ffn(hm, layer, *ffn2, mixer=(ylru_m, _attention(qt_m, k_m, vt_m), mixp))
    return h
```

```python
import functools

import jax
import jax.numpy as jnp
from jax import lax
from jax.experimental import pallas as pl
from jax.experimental.pallas import tpu as pltpu

D_MODEL = 1024
N_META = 16
LRU_WIDTH = 512
LRU_HEADS = 8
LRU_HEAD_DIM = 64
CONV_WIDTH = 4
LRU_C = 8.0
MLA_HEADS = 8
QK_NOPE = 64
QK_ROPE = 32
V_DIM = 64
Q_LORA = 384
KV_LORA = 256
D_FF = 2816
ROPE_THETA = 10000.0
EPS = 1e-6

LANES = 128
SUBLANES = 8
HEAD_PAD = LANES
QK_WIDTH = MLA_HEADS * HEAD_PAD
Q_NOPE_ALL = MLA_HEADS * QK_NOPE
Q_HALF_ALL = MLA_HEADS * QK_ROPE // 2
QT_ROWS = Q_NOPE_ALL + 2 * Q_HALF_ALL
IN_COLS_PAD = 2 * LRU_WIDTH + Q_LORA + KV_LORA + HEAD_PAD
SEQ_TILE = 256
MIX_TILE = 1024
MIX_SUB = 256
META_ROWS = LANES
FFN_ROWS = 512
FFN_SUBTILES = 2
FF_CHUNK = 256
NEG_BIG = -1e30
TINY = 1e-30
LOG2_E = 1.4426950408889634
VMEM_LIMIT = 52 * 1024 * 1024

F32 = jnp.float32
BF16 = jnp.bfloat16


def _rms(x, g):
    return x * lax.rsqrt(jnp.mean(x * x, axis=-1, keepdims=True) + EPS) * g


def _const_spec(shape, layer=None):
    if layer is None:
        idx = lambda *_: (0,) * len(shape)
        return pl.BlockSpec(shape, idx, pipeline_mode=pl.Buffered(1))
    idx = lambda *_: (layer,) + (0,) * len(shape)
    return pl.BlockSpec((None,) + shape, idx, pipeline_mode=pl.Buffered(1))


def _mix_out_rows(h, ylru, ot, gmla, wout_ref, gpost):
    ms = jnp.mean(ot * ot, axis=0, keepdims=True)
    on = (ot * lax.rsqrt(ms + EPS) * gmla).T.astype(BF16)
    y = jnp.dot(ylru, wout_ref[:LRU_WIDTH, :], preferred_element_type=F32)
    y = y + jnp.dot(on, wout_ref[LRU_WIDTH:, :], preferred_element_type=F32)
    return h + _rms(y, gpost)


def _ffn_kernel(after_mixer, h_ref, *refs):
    if after_mixer:
        ylru_ref, ot_ref, gmla_ref, wout_ref, gmix_ref = refs[:5]
        refs = refs[5:]
    gpre_ref, wg_ref, wu_ref, wd_ref, gpost_ref, o_ref, a_ref = refs
    rows = h_ref.shape[0]
    sub = rows // FFN_SUBTILES if after_mixer and rows % (FFN_SUBTILES * LANES) == 0 else rows
    parts = [slice(i * sub, (i + 1) * sub) for i in range(rows // sub)]
    xs = []
    for r in parts:
        x = h_ref[r, :]
        if after_mixer:
            x = _mix_out_rows(x, ylru_ref[r, :], ot_ref[:, r], gmla_ref[...], wout_ref,
                              gmix_ref[...])
        xs.append(x)
    for r, x in zip(parts, xs):
        u = _rms(x, gpre_ref[...]).astype(BF16)
        for c in range(D_FF // FF_CHUNK):
            sl = slice(c * FF_CHUNK, (c + 1) * FF_CHUNK)
            gate = jnp.dot(u, wg_ref[:, sl], preferred_element_type=F32)
            up = jnp.dot(u, wu_ref[:, sl], preferred_element_type=F32)
            a_ref[r, sl] = (gate * jax.nn.sigmoid(gate) * up).astype(BF16)
    for r, x in zip(parts, xs):
        f = jnp.dot(a_ref[r, :], wd_ref[...], preferred_element_type=F32)
        o_ref[r, :] = x + 0.5 * _rms(f, gpost_ref[...])


def _ffn(h, layer, gpre, wg, wu, wd, gpost, mixer=None):
    bsz, tp, _ = h.shape
    tile = min(FFN_ROWS if mixer is not None else 2 * FFN_ROWS, tp)
    assert tp % tile == 0
    row_spec = pl.BlockSpec((None, tile, D_MODEL), lambda b, t: (b, t, 0))
    mixer_specs, mixer_args = [], []
    if mixer is not None:
        ylru, ot, p = mixer
        vdim = MLA_HEADS * V_DIM
        mixer_specs = [
            pl.BlockSpec((None, tile, LRU_WIDTH), lambda b, t: (b, t, 0)),
            pl.BlockSpec((None, vdim, tile), lambda b, t: (b, 0, t)),
            _const_spec((vdim, 1), layer),
            _const_spec((D_MODEL, D_MODEL), layer),
            _const_spec((1, D_MODEL), layer),
        ]
        mixer_args = [ylru, ot, p["mla_out_g"], p["w_out"], p["mix_post_g"]]
    return pl.pallas_call(
        functools.partial(_ffn_kernel, mixer is not None),
        grid=(bsz, tp // tile),
        in_specs=[row_spec] + mixer_specs + [
            _const_spec((1, D_MODEL), layer),
            _const_spec((D_MODEL, D_FF), layer),
            _const_spec((D_MODEL, D_FF), layer),
            _const_spec((D_FF, D_MODEL), layer),
            _const_spec((1, D_MODEL), layer),
        ],
        out_specs=row_spec,
        out_shape=jax.ShapeDtypeStruct(h.shape, F32),
        scratch_shapes=[pltpu.VMEM((tile, D_FF), BF16)],
        compiler_params=pltpu.CompilerParams(
            dimension_semantics=("arbitrary", "arbitrary"), vmem_limit_bytes=VMEM_LIMIT),
        name="ffn_mix" if mixer is not None else "ffn",
    )(h, *mixer_args, gpre, wg, wu, wd, gpost)


def _rope(x, c, sa, sb):
    n = x.shape[-1]
    half = QK_ROPE // 2
    return x * c + pltpu.roll(x, half, 1) * sa + pltpu.roll(x, n - half, 1) * sb


def _mix_in_kernel(emit_state, h_ref, g_ref, win_ref, cw_ref, cb_ref, wax_ref, ba_ref, bx_ref,
                   lam_ref, gq_ref, wuq_ref, gkv_ref, wuk_ref, wuv_ref, glru_ref,
                   rc_ref, rsa_ref, rsb_ref, cos8_ref, sin8_ref, xprev0_ref, hcar0_ref,
                   ylru_ref, qt_ref, k_ref, vt_ref, *rest):
    if emit_state:
        xr_out_ref, hs_out_ref, xprev_ref, hcar_ref = rest
    else:
        xprev_ref, hcar_ref = rest
    tt = h_ref.shape[0]

    @pl.when(pl.program_id(1) == 0)
    def _():
        xprev_ref[...] = xprev0_ref[...]
        hcar_ref[...] = hcar0_ref[...]

    sub = min(MIX_SUB, tt)
    half = LRU_WIDTH // 2
    nl = -lam_ref[...]
    softplus = jnp.maximum(nl, 0.0) + jnp.log1p(jnp.exp(-jnp.abs(nl)))
    decay = (-LRU_C * LOG2_E) * softplus
    gq_scaled = gq_ref[...] * (LOG2_E * (QK_NOPE + QK_ROPE) ** -0.5)
    tile_heads = lambda t: jnp.concatenate([t] * MLA_HEADS, axis=1)

    def project(i):
        u = _rms(h_ref[i * sub:(i + 1) * sub, :], g_ref[...]).astype(BF16)
        return jnp.dot(u, win_ref[...], preferred_element_type=F32)

    xprev = xprev_ref[...]
    carry = hcar_ref[...]
    z_next = project(0)
    for i in range(tt // sub):
        z = z_next
        if (i + 1) * sub < tt:
            z_next = project(i + 1)
        rows = slice(i * sub, (i + 1) * sub)
        xr = z[:, 0:LRU_WIDTH]
        gr = z[:, LRU_WIDTH:2 * LRU_WIDTH]
        cq = z[:, 2 * LRU_WIDTH:2 * LRU_WIDTH + Q_LORA]
        ckv = z[:, 2 * LRU_WIDTH + Q_LORA:2 * LRU_WIDTH + Q_LORA + KV_LORA]
        krt = z[:, 2 * LRU_WIDTH + Q_LORA + KV_LORA:]

        xe = jnp.concatenate([xprev, xr], axis=0)
        xc = xr * cw_ref[CONV_WIDTH - 1:CONV_WIDTH, :] + cb_ref[...]
        for j in range(1, CONV_WIDTH):
            w_j = cw_ref[CONV_WIDTH - 1 - j:CONV_WIDTH - j, :]
            xc = xc + pltpu.roll(xe, j, 0)[SUBLANES:] * w_j
        xprev = xr[sub - SUBLANES:]

        xcb = xc.astype(BF16)
        ri0 = jnp.dot(xcb[:, :half], wax_ref[0], preferred_element_type=F32)
        ri1 = jnp.dot(xcb[:, half:], wax_ref[1], preferred_element_type=F32)
        r = jax.nn.sigmoid(jnp.concatenate([ri0[:, :half], ri1[:, :half]], axis=1) + ba_ref[...])
        g_in = jax.nn.sigmoid(
            jnp.concatenate([ri0[:, half:], ri1[:, half:]], axis=1) + bx_ref[...])
        a = jnp.exp2(r * decay)
        gap = 1.0 - a * a
        b = (gap * lax.rsqrt(jnp.maximum(gap, TINY))) * (g_in * xc)

        grouped = (sub // SUBLANES, SUBLANES, LRU_WIDTH)
        a = a.reshape(grouped)
        b = b.reshape(grouped)
        row = lax.broadcasted_iota(jnp.int32, grouped, 1)
        s = 1
        while s < SUBLANES:
            valid = row >= s
            b = a * jnp.where(valid, pltpu.roll(b, s, 1), 0.0) + b
            a = a * jnp.where(valid, pltpu.roll(a, s, 1), 1.0)
            s *= 2
        a = a.reshape(sub, LRU_WIDTH)
        b = b.reshape(sub, LRU_WIDTH)
        groups = []
        for gidx in range(sub // SUBLANES):
            sl = slice(gidx * SUBLANES, (gidx + 1) * SUBLANES)
            hg = b[sl] + a[sl] * carry
            carry = hg[SUBLANES - 1:SUBLANES]
            groups.append(hg)
        hs = jnp.concatenate(groups, axis=0)
        if emit_state:
            xr_out_ref[rows, :] = xr
            hs_out_ref[rows, :] = hs

        cg = 0.7978845608028654
        half_gr = 0.5 * gr
        gelu = half_gr + half_gr * jnp.tanh(gr * (cg + (cg * 0.044715) * (gr * gr)))
        ylru_ref[rows, :] = _rms(hs * gelu, glru_ref[...]).astype(BF16)

        rc = rc_ref[rows, :]
        rsa = rsa_ref[rows, :]
        rsb = rsb_ref[rows, :]
        cqn = _rms(cq, gq_scaled).astype(BF16)
        q = jnp.dot(cqn, wuq_ref[...], preferred_element_type=F32)
        q1 = q[:, Q_NOPE_ALL:Q_NOPE_ALL + Q_HALF_ALL]
        q2 = q[:, Q_NOPE_ALL + Q_HALF_ALL:]
        cos8 = cos8_ref[rows, :]
        sin8 = sin8_ref[rows, :]
        q = jnp.concatenate([q[:, :Q_NOPE_ALL], q1 * cos8 - q2 * sin8, q2 * cos8 + q1 * sin8],
                            axis=1)
        qt_ref[:, rows] = q.T.astype(BF16)

        ckvn = _rms(ckv, gkv_ref[...]).astype(BF16)
        kn = jnp.dot(ckvn, wuk_ref[...], preferred_element_type=F32)
        kr = _rope(krt, rc, rsa, rsb)
        k_ref[rows, :] = (kn + tile_heads(kr)).astype(BF16)
        v = jnp.dot(ckvn, wuv_ref[...], preferred_element_type=F32)
        vt_ref[:, rows] = v.T.astype(BF16)
    xprev_ref[...] = xprev
    hcar_ref[...] = carry


def _mix_in(h, layer, p, rope, xprev0, hcar0, emit_state=False):
    bsz, tp, _ = h.shape
    tt = min(MIX_TILE, tp)
    nt = tp // tt
    tbl_spec = pl.BlockSpec((tt, HEAD_PAD), lambda b, t: (t, 0))
    seq_spec = pl.BlockSpec((None, tt, LRU_WIDTH), lambda b, t: (b, t, 0))
    state_specs = [seq_spec, seq_spec] if emit_state else []
    state_shapes = [jax.ShapeDtypeStruct((bsz, tp, LRU_WIDTH), F32)] * 2 if emit_state else []
    return pl.pallas_call(
        functools.partial(_mix_in_kernel, emit_state),
        grid=(bsz, nt),
        in_specs=[
            pl.BlockSpec((None, tt, D_MODEL), lambda b, t: (b, t, 0)),
            _const_spec((1, D_MODEL), layer),
            _const_spec((D_MODEL, IN_COLS_PAD), layer),
            _const_spec((CONV_WIDTH, LRU_WIDTH), layer),
            _const_spec((1, LRU_WIDTH), layer),
            _const_spec((2, LRU_WIDTH // 2, LRU_WIDTH), layer),
            _const_spec((1, LRU_WIDTH), layer),
            _const_spec((1, LRU_WIDTH), layer),
            _const_spec((1, LRU_WIDTH), layer),
            _const_spec((1, Q_LORA), layer),
            _const_spec((Q_LORA, QT_ROWS), layer),
            _const_spec((1, KV_LORA), layer),
            _const_spec((KV_LORA, QK_WIDTH), layer),
            _const_spec((KV_LORA, MLA_HEADS * V_DIM), layer),
            _const_spec((1, LRU_WIDTH), layer),
            tbl_spec, tbl_spec, tbl_spec, tbl_spec, tbl_spec,
            _const_spec((SUBLANES, LRU_WIDTH)),
            _const_spec((1, LRU_WIDTH)),
        ],
        out_specs=[
            seq_spec,
            pl.BlockSpec((None, QT_ROWS, tt), lambda b, t: (b, 0, t)),
            pl.BlockSpec((None, tt, QK_WIDTH), lambda b, t: (b, t, 0)),
            pl.BlockSpec((None, MLA_HEADS * V_DIM, tt), lambda b, t: (b, 0, t)),
        ] + state_specs,
        out_shape=[
            jax.ShapeDtypeStruct((bsz, tp, LRU_WIDTH), BF16),
            jax.ShapeDtypeStruct((bsz, QT_ROWS, tp), BF16),
            jax.ShapeDtypeStruct((bsz, tp, QK_WIDTH), BF16),
            jax.ShapeDtypeStruct((bsz, MLA_HEADS * V_DIM, tp), BF16),
        ] + state_shapes,
        scratch_shapes=[pltpu.VMEM((SUBLANES, LRU_WIDTH), F32),
                        pltpu.VMEM((1, LRU_WIDTH), F32)],
        compiler_params=pltpu.CompilerParams(
            dimension_semantics=("arbitrary", "arbitrary"), vmem_limit_bytes=VMEM_LIMIT),
        name="mix_in",
    )(h, p["mix_pre_g"], p["w_in"], p["conv_w"], p["conv_b"], p["wax"], p["b_a"], p["b_x"],
      p["lam"], p["q_g"], p["w_uq"], p["kv_g"], p["w_uk"], p["w_uv"], p["lru_out_g"], *rope,
      xprev0, hcar0)


def _group_max(s):
    m = s[0:SUBLANES]
    for r in range(1, s.shape[0] // SUBLANES):
        m = jnp.maximum(m, s[r * SUBLANES:(r + 1) * SUBLANES])
    return m


def _sublane_allmax(m):
    for shift in (1, 2, 4):
        m = jnp.maximum(m, pltpu.roll(m, shift, 0))
    return m


def _attn_kernel(has_prefix, qt_ref, k_ref, vt_ref, *rest):
    if has_prefix:
        kpre_ref, vtpre_ref = rest[:2]
        rest = rest[2:]
        spre_ref = rest[-1]
        rest = rest[:-1]
    o_ref, m_ref, l_ref, acc_ref, s0_ref, s1_ref, bm0_ref, bm1_ref = rest
    qi = pl.program_id(1)
    nq = pl.num_programs(1)
    tq = o_ref.shape[1]
    tk = s0_ref.shape[1]
    s_refs = (s0_ref, s1_ref)
    bm_refs = (bm0_ref, bm1_ref)
    ones_rows = 2 * SUBLANES

    def with_ones(vb):
        return jnp.concatenate([vb, jnp.ones((ones_rows, vb.shape[1]), BF16)], axis=0)

    m_ref[...] = jnp.full(m_ref.shape, NEG_BIG, F32)
    l_ref[...] = jnp.zeros(l_ref.shape, F32)
    acc_ref[...] = jnp.zeros(acc_ref.shape, F32)

    def q_head(g, tile):
        half = QK_ROPE // 2
        r1 = Q_NOPE_ALL + g * half
        r2 = r1 + Q_HALF_ALL
        cols = pl.ds(pl.multiple_of(tile * tq, tq), tq)
        return jnp.concatenate(
            [qt_ref[g * QK_NOPE:(g + 1) * QK_NOPE, cols], qt_ref[r1:r1 + half, cols],
             qt_ref[r2:r2 + half, cols], jnp.zeros((HEAD_PAD - QK_NOPE - QK_ROPE, tq), BF16)],
            axis=0)

    def scores(j, g, slot, tile):
        off = pl.multiple_of(j * tk, tk)
        s = jnp.dot(k_ref[pl.ds(off, tk), g * HEAD_PAD:(g + 1) * HEAD_PAD], q_head(g, tile),
                    preferred_element_type=F32)
        s_refs[slot][g] = s
        bm_refs[slot][g] = _group_max(s)

    def first_scores(g, tile):
        scores(0, g, 0, tile)
        if has_prefix:
            spre_ref[g] = jnp.dot(kpre_ref[:, g * HEAD_PAD:(g + 1) * HEAD_PAD], q_head(g, tile),
                                  preferred_element_type=F32)

    def softmax_pv(j, g, slot, masked):
        s = s_refs[slot][g]
        if masked:
            key = lax.broadcasted_iota(jnp.int32, s.shape, 0)
            qry = lax.broadcasted_iota(jnp.int32, s.shape, 1)
            s = jnp.where(key <= qry, s, NEG_BIG)
            bm = _group_max(s)
            if has_prefix:
                s_pre = spre_ref[g]
                bm = jnp.maximum(bm, _group_max(s_pre))
        else:
            bm = bm_refs[slot][g]
        m_old = m_ref[g]
        m_new = jnp.maximum(m_old, _sublane_allmax(bm))
        alpha = jnp.exp2(m_old - m_new)
        p = jnp.exp2(s - jnp.concatenate([m_new] * (tk // SUBLANES), axis=0))
        m_ref[g] = m_new
        off = pl.multiple_of(j * tk, tk)
        vb = with_ones(vt_ref[g * V_DIM:(g + 1) * V_DIM, pl.ds(off, tk)])
        pv = jnp.dot(vb, p.astype(BF16), preferred_element_type=F32)
        if masked and has_prefix:
            p_pre = jnp.exp2(s_pre - jnp.concatenate([m_new] * (N_META // SUBLANES), axis=0))
            p_pre = jnp.concatenate(
                [p_pre.astype(BF16), jnp.zeros((LANES - N_META, tq), BF16)], axis=0)
            pv = pv + jnp.dot(with_ones(vtpre_ref[g * V_DIM:(g + 1) * V_DIM, :]), p_pre,
                              preferred_element_type=F32)
        acc_ref[g] = (jnp.concatenate([alpha] * (V_DIM // SUBLANES), axis=0) * acc_ref[g]
                      + pv[:V_DIM])
        l_ref[g] = alpha * l_ref[g] + pv[V_DIM:V_DIM + SUBLANES]

    def step(j_next, slot_next, j, slot):
        for g in range(MLA_HEADS):
            scores(j_next, g, slot_next, qi)
            softmax_pv(j, g, slot, False)

    def last_step(slot):
        nxt = jnp.minimum(qi + 1, nq - 1)
        for g in range(MLA_HEADS):
            softmax_pv(qi, g, slot, True)
            first_scores(g, nxt)

    @pl.when(qi == 0)
    def _():
        for g in range(MLA_HEADS):
            first_scores(g, 0)

    def pair(i, carry):
        j = 2 * i
        step(j + 1, 1, j, 0)
        step(j + 2, 0, j + 1, 1)
        return carry

    lax.fori_loop(0, qi // 2, pair, 0)

    @pl.when(qi % 2 == 0)
    def _():
        last_step(0)

    @pl.when(qi % 2 == 1)
    def _():
        step(qi, 1, qi - 1, 0)
        last_step(1)

    for g in range(MLA_HEADS):
        inv_l = 1.0 / jnp.concatenate([l_ref[g]] * (V_DIM // SUBLANES), axis=0)
        o_ref[g * V_DIM:(g + 1) * V_DIM, :] = acc_ref[g] * inv_l


def _attention(qt, k, vt, prefix=None):
    bsz, _, tp = qt.shape
    tq = tk = min(SEQ_TILE, tp)
    vdim = MLA_HEADS * V_DIM
    stat = pltpu.VMEM((MLA_HEADS, SUBLANES, tq), F32)
    has_prefix = prefix is not None
    prefix_specs = ([_const_spec((N_META, QK_WIDTH)), _const_spec((vdim, LANES))]
                    if has_prefix else [])
    return pl.pallas_call(
        functools.partial(_attn_kernel, has_prefix),
        grid=(bsz, tp // tq),
        in_specs=[
            pl.BlockSpec((None, QT_ROWS, tp), lambda b, i: (b, 0, 0)),
            pl.BlockSpec((None, tp, QK_WIDTH), lambda b, i: (b, 0, 0)),
            pl.BlockSpec((None, vdim, tp), lambda b, i: (b, 0, 0)),
        ] + prefix_specs,
        out_specs=pl.BlockSpec((None, vdim, tq), lambda b, i: (b, 0, i)),
        out_shape=jax.ShapeDtypeStruct((bsz, vdim, tp), F32),
        scratch_shapes=[stat, stat, pltpu.VMEM((MLA_HEADS, V_DIM, tq), F32),
                        pltpu.VMEM((MLA_HEADS, tk, tq), F32),
                        pltpu.VMEM((MLA_HEADS, tk, tq), F32), stat, stat]
        + ([pltpu.VMEM((MLA_HEADS, N_META, tq), F32)] if has_prefix else []),
        compiler_params=pltpu.CompilerParams(
            dimension_semantics=("arbitrary", "arbitrary"), vmem_limit_bytes=VMEM_LIMIT),
        name="attention",
    )(qt, k, vt, *(prefix or ()))


def _rope_tables(tp):
    pos = jnp.arange(tp, dtype=F32)
    inv_freq = 1.0 / (ROPE_THETA ** (jnp.arange(0, QK_ROPE, 2, dtype=F32) / QK_ROPE))
    ang = pos[:, None] * inv_freq[None, :]
    cos, sin = jnp.cos(ang), jnp.sin(ang)
    half = QK_ROPE // 2
    zeros = lambda n: jnp.zeros((tp, n), F32)
    pad = HEAD_PAD - QK_NOPE - QK_ROPE
    c = jnp.concatenate([jnp.ones((tp, QK_NOPE), F32), cos, cos, zeros(pad)], axis=1)
    sa = jnp.concatenate([zeros(QK_NOPE + half), sin, zeros(pad)], axis=1)
    sb = jnp.concatenate([zeros(QK_NOPE), -sin, zeros(half + pad)], axis=1)
    return c, sa, sb, jnp.tile(cos, (1, MLA_HEADS)), jnp.tile(sin, (1, MLA_HEADS))


def _block_diag(w):
    depth = w.shape[0]
    per = LRU_HEADS // 2
    w5 = w.reshape(depth, 2, per, LRU_HEAD_DIM, LRU_HEAD_DIM)
    bd = jnp.einsum("dnhij,hg->dnhigj", w5, jnp.eye(per, dtype=w.dtype))
    return bd.reshape(depth, 2, per * LRU_HEAD_DIM, per * LRU_HEAD_DIM)


def _row(g):
    return g.reshape(g.shape[0], 1, g.shape[1])


@jax.jit
def kernel(x, meta_tokens, ffn1_pre_g, ffn1_w_gate, ffn1_w_up, ffn1_w_down, ffn1_post_g, mix_pre_g, w_in, lru_conv_w, lru_conv_b, lru_w_a, lru_b_a, lru_w_x, lru_b_x, lru_lambda, mla_q_norm_g, mla_w_uq, mla_kv_norm_g, mla_w_ukv, lru_out_g, mla_out_g, w_out, mix_post_g, ffn2_pre_g, ffn2_w_gate, ffn2_w_up, ffn2_w_down, ffn2_post_g):
    bsz, seq, _ = x.shape
    depth = w_in.shape[0]
    assert seq % SEQ_TILE == 0 and seq % MIX_TILE == 0
    h = x
    hm = jnp.concatenate([meta_tokens.astype(x.dtype),
                          jnp.zeros((META_ROWS - N_META, D_MODEL), x.dtype)], axis=0)[None]
    rope_all = _rope_tables(N_META + max(seq, META_ROWS))
    rope_meta = tuple(t[:META_ROWS] for t in rope_all)
    rope_main = tuple(t[N_META:N_META + seq] for t in rope_all)
    zero_hist = jnp.zeros((SUBLANES, LRU_WIDTH), F32)
    zero_state = jnp.zeros((1, LRU_WIDTH), F32)

    split = 2 * LRU_WIDTH + Q_LORA + KV_LORA
    zcols = lambda n: jnp.zeros((depth, D_MODEL, n), w_in.dtype)
    w_in_p = jnp.concatenate(
        [w_in[..., :split], zcols(QK_NOPE), w_in[..., split:],
         zcols(HEAD_PAD - QK_NOPE - QK_ROPE)], axis=-1).astype(BF16)
    w_uq4 = mla_w_uq.reshape(depth, Q_LORA, MLA_HEADS, QK_NOPE + QK_ROPE)
    half = QK_ROPE // 2
    w_uq_p = jnp.concatenate(
        [w_uq4[..., :QK_NOPE].reshape(depth, Q_LORA, Q_NOPE_ALL),
         w_uq4[..., QK_NOPE:QK_NOPE + half].reshape(depth, Q_LORA, Q_HALF_ALL),
         w_uq4[..., QK_NOPE + half:].reshape(depth, Q_LORA, Q_HALF_ALL)], axis=-1)
    w_ukv4 = mla_w_ukv.reshape(depth, KV_LORA, MLA_HEADS, QK_NOPE + V_DIM)
    w_uk_p = jnp.pad(w_ukv4[..., :QK_NOPE], ((0, 0), (0, 0), (0, 0), (0, HEAD_PAD - QK_NOPE)))
    mixp = {
        "mix_pre_g": _row(mix_pre_g),
        "w_in": w_in_p,
        "conv_w": lru_conv_w,
        "conv_b": _row(lru_conv_b),
        "wax": jnp.concatenate([_block_diag(lru_w_a), _block_diag(lru_w_x)], axis=-1).astype(BF16),
        "b_a": _row(lru_b_a),
        "b_x": _row(lru_b_x),
        "lam": _row(lru_lambda),
        "q_g": _row(mla_q_norm_g),
        "w_uq": w_uq_p.astype(BF16),
        "kv_g": _row(mla_kv_norm_g),
        "w_uk": w_uk_p.reshape(depth, KV_LORA, QK_WIDTH).astype(BF16),
        "w_uv": w_ukv4[..., QK_NOPE:].reshape(depth, KV_LORA, MLA_HEADS * V_DIM).astype(BF16),
        "lru_out_g": _row(lru_out_g),
        "mla_out_g": mla_out_g.reshape(depth, MLA_HEADS * V_DIM, 1),
        "w_out": w_out.astype(BF16),
        "mix_post_g": _row(mix_post_g),
    }
    ffn1 = (_row(ffn1_pre_g), ffn1_w_gate.astype(BF16), ffn1_w_up.astype(BF16),
            ffn1_w_down.astype(BF16), _row(ffn1_post_g))
    ffn2 = (_row(ffn2_pre_g), ffn2_w_gate.astype(BF16), ffn2_w_up.astype(BF16),
            ffn2_w_down.astype(BF16), _row(ffn2_post_g))

    lane = lax.broadcasted_iota(jnp.int32, (MLA_HEADS * V_DIM, META_ROWS), 1)
    for layer in range(depth):
        hm = _ffn(hm, layer, *ffn1)
        h = _ffn(h, layer, *ffn1)
        ylru_m, qt_m, k_m, vt_m, xr_m, hs_m = _mix_in(
            hm, layer, mixp, rope_meta, zero_hist, zero_state, emit_state=True)
        ylru, qt, k, vt = _mix_in(h, layer, mixp, rope_main,
                                  xr_m[0, N_META - SUBLANES:N_META], hs_m[0, N_META - 1:N_META])
        prefix = (k_m[0, :N_META], jnp.where(lane < N_META, vt_m[0], 0).astype(BF16))
        h = _ffn(h, layer, *ffn2, mixer=(ylru, _attention(qt, k, vt, prefix), mixp))
        if layer + 1 < depth:
            hm = _ffn(hm, layer, *ffn2, mixer=(ylru_m, _attention(qt_m, k_m, vt_m), mixp))
    return h
```

```python
import functools

import jax
import jax.numpy as jnp
from jax import lax
from jax.experimental import pallas as pl
from jax.experimental.pallas import tpu as pltpu

D_MODEL = 1024
N_META = 16
LRU_WIDTH = 512
LRU_HEADS = 8
LRU_HEAD_DIM = 64
CONV_WIDTH = 4
LRU_C = 8.0
MLA_HEADS = 8
QK_NOPE = 64
QK_ROPE = 32
V_DIM = 64
Q_LORA = 384
KV_LORA = 256
D_FF = 2816
ROPE_THETA = 10000.0
EPS = 1e-6

LANES = 128
SUBLANES = 8
HEAD_PAD = LANES
QK_WIDTH = MLA_HEADS * HEAD_PAD
Q_NOPE_ALL = MLA_HEADS * QK_NOPE
Q_HALF_ALL = MLA_HEADS * QK_ROPE // 2
QT_ROWS = Q_NOPE_ALL + 2 * Q_HALF_ALL
IN_COLS_PAD = 2 * LRU_WIDTH + Q_LORA + KV_LORA + HEAD_PAD
SEQ_TILE = 256
MIX_TILE = 1024
MIX_SUB = 256
META_ROWS = LANES
FFN_ROWS = 512
FFN_SUBTILES = 2
FF_CHUNK = 256
NEG_BIG = -1e30
TINY = 1e-30
LOG2_E = 1.4426950408889634
VMEM_LIMIT = 52 * 1024 * 1024

F32 = jnp.float32
BF16 = jnp.bfloat16


def _rms(x, g):
    return x * lax.rsqrt(jnp.mean(x * x, axis=-1, keepdims=True) + EPS) * g


def _const_spec(shape, layer=None):
    if layer is None:
        idx = lambda *_: (0,) * len(shape)
        return pl.BlockSpec(shape, idx, pipeline_mode=pl.Buffered(1))
    idx = lambda *_: (layer,) + (0,) * len(shape)
    return pl.BlockSpec((None,) + shape, idx, pipeline_mode=pl.Buffered(1))


def _mix_out_rows(h, ylru, ot, gmla, wout_ref, gpost):
    ms = jnp.mean(ot * ot, axis=0, keepdims=True)
    on = (ot * lax.rsqrt(ms + EPS) * gmla).T.astype(BF16)
    y = jnp.dot(ylru, wout_ref[:LRU_WIDTH, :], preferred_element_type=F32)
    y = y + jnp.dot(on, wout_ref[LRU_WIDTH:, :], preferred_element_type=F32)
    return h + _rms(y, gpost)


def _ffn_kernel(after_mixer, h_ref, *refs):
    if after_mixer:
        ylru_ref, ot_ref, gmla_ref, wout_ref, gmix_ref = refs[:5]
        refs = refs[5:]
    gpre_ref, wg_ref, wu_ref, wd_ref, gpost_ref, o_ref, a_ref = refs
    rows = h_ref.shape[0]
    sub = rows // FFN_SUBTILES if after_mixer and rows % (FFN_SUBTILES * LANES) == 0 else rows
    parts = [slice(i * sub, (i + 1) * sub) for i in range(rows // sub)]
    xs = []
    for r in parts:
        x = h_ref[r, :]
        if after_mixer:
            x = _mix_out_rows(x, ylru_ref[r, :], ot_ref[:, r], gmla_ref[...], wout_ref,
                              gmix_ref[...])
        xs.append(x)
    for r, x in zip(parts, xs):
        u = _rms(x, gpre_ref[...]).astype(BF16)
        for c in range(D_FF // FF_CHUNK):
            sl = slice(c * FF_CHUNK, (c + 1) * FF_CHUNK)
            gate = jnp.dot(u, wg_ref[:, sl], preferred_element_type=F32)
            up = jnp.dot(u, wu_ref[:, sl], preferred_element_type=F32)
            a_ref[r, sl] = (gate * jax.nn.sigmoid(gate) * up).astype(BF16)
    for r, x in zip(parts, xs):
        f = jnp.dot(a_ref[r, :], wd_ref[...], preferred_element_type=F32)
        o_ref[r, :] = x + 0.5 * _rms(f, gpost_ref[...])


def _ffn(h, layer, gpre, wg, wu, wd, gpost, mixer=None):
    bsz, tp, _ = h.shape
    tile = min(FFN_ROWS if mixer is not None else 2 * FFN_ROWS, tp)
    assert tp % tile == 0
    row_spec = pl.BlockSpec((None, tile, D_MODEL), lambda b, t: (b, t, 0))
    mixer_specs, mixer_args = [], []
    if mixer is not None:
        ylru, ot, p = mixer
        vdim = MLA_HEADS * V_DIM
        mixer_specs = [
            pl.BlockSpec((None, tile, LRU_WIDTH), lambda b, t: (b, t, 0)),
            pl.BlockSpec((None, vdim, tile), lambda b, t: (b, 0, t)),
            _const_spec((vdim, 1), layer),
            _const_spec((D_MODEL, D_MODEL), layer),
            _const_spec((1, D_MODEL), layer),
        ]
        mixer_args = [ylru, ot, p["mla_out_g"], p["w_out"], p["mix_post_g"]]
    return pl.pallas_call(
        functools.partial(_ffn_kernel, mixer is not None),
        grid=(bsz, tp // tile),
        in_specs=[row_spec] + mixer_specs + [
            _const_spec((1, D_MODEL), layer),
            _const_spec((D_MODEL, D_FF), layer),
            _const_spec((D_MODEL, D_FF), layer),
            _const_spec((D_FF, D_MODEL), layer),
            _const_spec((1, D_MODEL), layer),
        ],
        out_specs=row_spec,
        out_shape=jax.ShapeDtypeStruct(h.shape, F32),
        scratch_shapes=[pltpu.VMEM((tile, D_FF), BF16)],
        compiler_params=pltpu.CompilerParams(
            dimension_semantics=("arbitrary", "arbitrary"), vmem_limit_bytes=VMEM_LIMIT),
        name="ffn_mix" if mixer is not None else "ffn",
    )(h, *mixer_args, gpre, wg, wu, wd, gpost)


def _rope(x, c, sa, sb):
    n = x.shape[-1]
    half = QK_ROPE // 2
    return x * c + pltpu.roll(x, half, 1) * sa + pltpu.roll(x, n - half, 1) * sb


def _mix_in_kernel(emit_state, h_ref, g_ref, win_ref, cw_ref, cb_ref, wax_ref, ba_ref, bx_ref,
                   lam_ref, gq_ref, wuq_ref, gkv_ref, wuk_ref, wuv_ref, glru_ref,
                   rc_ref, rsa_ref, rsb_ref, cos8_ref, sin8_ref, xprev0_ref, hcar0_ref,
                   ylru_ref, qt_ref, k_ref, vt_ref, *rest):
    if emit_state:
        xr_out_ref, hs_out_ref, xprev_ref, hcar_ref = rest
    else:
        xprev_ref, hcar_ref = rest
    tt = h_ref.shape[0]

    @pl.when(pl.program_id(1) == 0)
    def _():
        xprev_ref[...] = xprev0_ref[...]
        hcar_ref[...] = hcar0_ref[...]

    sub = min(MIX_SUB, tt)
    half = LRU_WIDTH // 2
    nl = -lam_ref[...]
    softplus = jnp.maximum(nl, 0.0) + jnp.log1p(jnp.exp(-jnp.abs(nl)))
    decay = (-LRU_C * LOG2_E) * softplus
    gq_scaled = gq_ref[...] * (LOG2_E * (QK_NOPE + QK_ROPE) ** -0.5)
    tile_heads = lambda t: jnp.concatenate([t] * MLA_HEADS, axis=1)

    def project(i):
        u = _rms(h_ref[i * sub:(i + 1) * sub, :], g_ref[...]).astype(BF16)
        return jnp.dot(u, win_ref[...], preferred_element_type=F32)

    xprev = xprev_ref[...]
    carry = hcar_ref[...]
    z_next = project(0)
    for i in range(tt // sub):
        z = z_next
        if (i + 1) * sub < tt:
            z_next = project(i + 1)
        rows = slice(i * sub, (i + 1) * sub)
        xr = z[:, 0:LRU_WIDTH]
        gr = z[:, LRU_WIDTH:2 * LRU_WIDTH]
        cq = z[:, 2 * LRU_WIDTH:2 * LRU_WIDTH + Q_LORA]
        ckv = z[:, 2 * LRU_WIDTH + Q_LORA:2 * LRU_WIDTH + Q_LORA + KV_LORA]
        krt = z[:, 2 * LRU_WIDTH + Q_LORA + KV_LORA:]

        xe = jnp.concatenate([xprev, xr], axis=0)
        xc = xr * cw_ref[CONV_WIDTH - 1:CONV_WIDTH, :] + cb_ref[...]
        for j in range(1, CONV_WIDTH):
            w_j = cw_ref[CONV_WIDTH - 1 - j:CONV_WIDTH - j, :]
            xc = xc + pltpu.roll(xe, j, 0)[SUBLANES:] * w_j
        xprev = xr[sub - SUBLANES:]

        xcb = xc.astype(BF16)
        ri0 = jnp.dot(xcb[:, :half], wax_ref[0], preferred_element_type=F32)
        ri1 = jnp.dot(xcb[:, half:], wax_ref[1], preferred_element_type=F32)
        r = jax.nn.sigmoid(jnp.concatenate([ri0[:, :half], ri1[:, :half]], axis=1) + ba_ref[...])
        g_in = jax.nn.sigmoid(
            jnp.concatenate([ri0[:, half:], ri1[:, half:]], axis=1) + bx_ref[...])
        a = jnp.exp2(r * decay)
        gap = 1.0 - a * a
        b = (gap * lax.rsqrt(jnp.maximum(gap, TINY))) * (g_in * xc)

        grouped = (sub // SUBLANES, SUBLANES, LRU_WIDTH)
        a = a.reshape(grouped)
        b = b.reshape(grouped)
        row = lax.broadcasted_iota(jnp.int32, grouped, 1)
        s = 1
        while s < SUBLANES:
            valid = row >= s
            b = a * jnp.where(valid, pltpu.roll(b, s, 1), 0.0) + b
            a = a * jnp.where(valid, pltpu.roll(a, s, 1), 1.0)
            s *= 2
        a = a.reshape(sub, LRU_WIDTH)
        b = b.reshape(sub, LRU_WIDTH)
        groups = []
        for gidx in range(sub // SUBLANES):
            sl = slice(gidx * SUBLANES, (gidx + 1) * SUBLANES)
            hg = b[sl] + a[sl] * carry
            carry = hg[SUBLANES - 1:SUBLANES]
            groups.append(hg)
        hs = jnp.concatenate(groups, axis=0)
        if emit_state:
            xr_out_ref[rows, :] = xr
            hs_out_ref[rows, :] = hs

        cg = 0.7978845608028654
        half_gr = 0.5 * gr
        gelu = half_gr + half_gr * jnp.tanh(gr * (cg + (cg * 0.044715) * (gr * gr)))
        ylru_ref[rows, :] = _rms(hs * gelu, glru_ref[...]).astype(BF16)

        rc = rc_ref[rows, :]
        rsa = rsa_ref[rows, :]
        rsb = rsb_ref[rows, :]
        cqn = _rms(cq, gq_scaled).astype(BF16)
        q = jnp.dot(cqn, wuq_ref[...], preferred_element_type=F32)
        q1 = q[:, Q_NOPE_ALL:Q_NOPE_ALL + Q_HALF_ALL]
        q2 = q[:, Q_NOPE_ALL + Q_HALF_ALL:]
        cos8 = cos8_ref[rows, :]
        sin8 = sin8_ref[rows, :]
        q = jnp.concatenate([q[:, :Q_NOPE_ALL], q1 * cos8 - q2 * sin8, q2 * cos8 + q1 * sin8],
                            axis=1)
        qt_ref[:, rows] = q.T.astype(BF16)

        ckvn = _rms(ckv, gkv_ref[...]).astype(BF16)
        kn = jnp.dot(ckvn, wuk_ref[...], preferred_element_type=F32)
        kr = _rope(krt, rc, rsa, rsb)
        k_ref[rows, :] = (kn + tile_heads(kr)).astype(BF16)
        v = jnp.dot(ckvn, wuv_ref[...], preferred_element_type=F32)
        vt_ref[:, rows] = v.T.astype(BF16)
    xprev_ref[...] = xprev
    hcar_ref[...] = carry


def _mix_in(h, layer, p, rope, xprev0, hcar0, emit_state=False):
    bsz, tp, _ = h.shape
    tt = min(MIX_TILE, tp)
    nt = tp // tt
    tbl_spec = pl.BlockSpec((tt, HEAD_PAD), lambda b, t: (t, 0))
    seq_spec = pl.BlockSpec((None, tt, LRU_WIDTH), lambda b, t: (b, t, 0))
    state_specs = [seq_spec, seq_spec] if emit_state else []
    state_shapes = [jax.ShapeDtypeStruct((bsz, tp, LRU_WIDTH), F32)] * 2 if emit_state else []
    return pl.pallas_call(
        functools.partial(_mix_in_kernel, emit_state),
        grid=(bsz, nt),
        in_specs=[
            pl.BlockSpec((None, tt, D_MODEL), lambda b, t: (b, t, 0)),
            _const_spec((1, D_MODEL), layer),
            _const_spec((D_MODEL, IN_COLS_PAD), layer),
            _const_spec((CONV_WIDTH, LRU_WIDTH), layer),
            _const_spec((1, LRU_WIDTH), layer),
            _const_spec((2, LRU_WIDTH // 2, LRU_WIDTH), layer),
            _const_spec((1, LRU_WIDTH), layer),
            _const_spec((1, LRU_WIDTH), layer),
            _const_spec((1, LRU_WIDTH), layer),
            _const_spec((1, Q_LORA), layer),
            _const_spec((Q_LORA, QT_ROWS), layer),
            _const_spec((1, KV_LORA), layer),
            _const_spec((KV_LORA, QK_WIDTH), layer),
            _const_spec((KV_LORA, MLA_HEADS * V_DIM), layer),
            _const_spec((1, LRU_WIDTH), layer),
            tbl_spec, tbl_spec, tbl_spec, tbl_spec, tbl_spec,
            _const_spec((SUBLANES, LRU_WIDTH)),
            _const_spec((1, LRU_WIDTH)),
        ],
        out_specs=[
            seq_spec,
            pl.BlockSpec((None, QT_ROWS, tt), lambda b, t: (b, 0, t)),
            pl.BlockSpec((None, tt, QK_WIDTH), lambda b, t: (b, t, 0)),
            pl.BlockSpec((None, MLA_HEADS * V_DIM, tt), lambda b, t: (b, 0, t)),
        ] + state_specs,
        out_shape=[
            jax.ShapeDtypeStruct((bsz, tp, LRU_WIDTH), BF16),
            jax.ShapeDtypeStruct((bsz, QT_ROWS, tp), BF16),
            jax.ShapeDtypeStruct((bsz, tp, QK_WIDTH), BF16),
            jax.ShapeDtypeStruct((bsz, MLA_HEADS * V_DIM, tp), BF16),
        ] + state_shapes,
        scratch_shapes=[pltpu.VMEM((SUBLANES, LRU_WIDTH), F32),
                        pltpu.VMEM((1, LRU_WIDTH), F32)],
        compiler_params=pltpu.CompilerParams(
            dimension_semantics=("arbitrary", "arbitrary"), vmem_limit_bytes=VMEM_LIMIT),
        name="mix_in",
    )(h, p["mix_pre_g"], p["w_in"], p["conv_w"], p["conv_b"], p["wax"], p["b_a"], p["b_x"],
      p["lam"], p["q_g"], p["w_uq"], p["kv_g"], p["w_uk"], p["w_uv"], p["lru_out_g"], *rope,
      xprev0, hcar0)


def _group_max(s):
    m = s[0:SUBLANES]
    for r in range(1, s.shape[0] // SUBLANES):
        m = jnp.maximum(m, s[r * SUBLANES:(r + 1) * SUBLANES])
    return m


def _sublane_allmax(m):
    for shift in (1, 2, 4):
        m = jnp.maximum(m, pltpu.roll(m, shift, 0))
    return m


def _attn_kernel(has_prefix, qt_ref, k_ref, vt_ref, *rest):
    if has_prefix:
        kpre_ref, vtpre_ref = rest[:2]
        rest = rest[2:]
        spre_ref = rest[-1]
        rest = rest[:-1]
    o_ref, m_ref, l_ref, acc_ref, s0_ref, s1_ref, bm0_ref, bm1_ref = rest
    step = pl.program_id(1)
    tk, tq = s0_ref.shape[1:]
    tiles = o_ref.shape[1] // tq
    assert tiles in (1, 2)
    last_tile = pl.num_programs(1) * tiles - 1
    s_refs = (s0_ref, s1_ref)
    bm_refs = (bm0_ref, bm1_ref)
    ones_rows = 2 * SUBLANES

    def with_ones(vb):
        return jnp.concatenate([vb, jnp.ones((ones_rows, vb.shape[1]), BF16)], axis=0)

    def q_head(g, tile):
        half = QK_ROPE // 2
        r1 = Q_NOPE_ALL + g * half
        r2 = r1 + Q_HALF_ALL
        cols = pl.ds(pl.multiple_of(tile * tq, tq), tq)
        return jnp.concatenate(
            [qt_ref[g * QK_NOPE:(g + 1) * QK_NOPE, cols], qt_ref[r1:r1 + half, cols],
             qt_ref[r2:r2 + half, cols], jnp.zeros((HEAD_PAD - QK_NOPE - QK_ROPE, tq), BF16)],
            axis=0)

    def scores(j, g, slot, tile):
        off = pl.multiple_of(j * tk, tk)
        s = jnp.dot(k_ref[pl.ds(off, tk), g * HEAD_PAD:(g + 1) * HEAD_PAD], q_head(g, tile),
                    preferred_element_type=F32)
        s_refs[slot][g] = s
        bm_refs[slot][g] = _group_max(s)

    def first_scores(g, tile):
        scores(0, g, 0, tile)
        if has_prefix:
            spre_ref[g] = jnp.dot(kpre_ref[:, g * HEAD_PAD:(g + 1) * HEAD_PAD], q_head(g, tile),
                                  preferred_element_type=F32)

    def softmax_pv(j, g, slot, masked):
        s = s_refs[slot][g]
        if masked:
            key = lax.broadcasted_iota(jnp.int32, s.shape, 0)
            qry = lax.broadcasted_iota(jnp.int32, s.shape, 1)
            s = jnp.where(key <= qry, s, NEG_BIG)
            bm = _group_max(s)
            if has_prefix:
                s_pre = spre_ref[g]
                bm = jnp.maximum(bm, _group_max(s_pre))
        else:
            bm = bm_refs[slot][g]
        m_old = m_ref[g]
        m_new = jnp.maximum(m_old, _sublane_allmax(bm))
        alpha = jnp.exp2(m_old - m_new)
        p = jnp.exp2(s - jnp.concatenate([m_new] * (tk // SUBLANES), axis=0))
        m_ref[g] = m_new
        off = pl.multiple_of(j * tk, tk)
        vb = with_ones(vt_ref[g * V_DIM:(g + 1) * V_DIM, pl.ds(off, tk)])
        pv = jnp.dot(vb, p.astype(BF16), preferred_element_type=F32)
        if masked and has_prefix:
            p_pre = jnp.exp2(s_pre - jnp.concatenate([m_new] * (N_META // SUBLANES), axis=0))
            p_pre = jnp.concatenate(
                [p_pre.astype(BF16), jnp.zeros((LANES - N_META, tq), BF16)], axis=0)
            pv = pv + jnp.dot(with_ones(vtpre_ref[g * V_DIM:(g + 1) * V_DIM, :]), p_pre,
                              preferred_element_type=F32)
        acc_ref[g] = (jnp.concatenate([alpha] * (V_DIM // SUBLANES), axis=0) * acc_ref[g]
                      + pv[:V_DIM])
        l_ref[g] = alpha * l_ref[g] + pv[V_DIM:V_DIM + SUBLANES]

    def query_tile(sub):
        qi = step * tiles + sub
        m_ref[...] = jnp.full(m_ref.shape, NEG_BIG, F32)
        l_ref[...] = jnp.zeros(l_ref.shape, F32)
        acc_ref[...] = jnp.zeros(acc_ref.shape, F32)

        def block_step(j_next, slot_next, j, slot):
            for g in range(MLA_HEADS):
                scores(j_next, g, slot_next, qi)
                softmax_pv(j, g, slot, False)

        def last_step(slot):
            nxt = jnp.minimum(qi + 1, last_tile)
            for g in range(MLA_HEADS):
                softmax_pv(qi, g, slot, True)
                first_scores(g, nxt)

        if sub == 0:
            @pl.when(step == 0)
            def _():
                for g in range(MLA_HEADS):
                    first_scores(g, 0)

        def pair(i, carry):
            j = 2 * i
            block_step(j + 1, 1, j, 0)
            block_step(j + 2, 0, j + 1, 1)
            return carry

        if tiles == 2:
            lax.fori_loop(0, step, pair, 0)
        if sub == 0:
            last_step(0)
        else:
            block_step(qi, 1, qi - 1, 0)
            last_step(1)

        cols = slice(sub * tq, (sub + 1) * tq)
        for g in range(MLA_HEADS):
            inv_l = 1.0 / jnp.concatenate([l_ref[g]] * (V_DIM // SUBLANES), axis=0)
            o_ref[g * V_DIM:(g + 1) * V_DIM, cols] = acc_ref[g] * inv_l

    for sub in range(tiles):
        query_tile(sub)


def _attention(qt, k, vt, prefix=None):
    bsz, _, tp = qt.shape
    tq = tk = min(SEQ_TILE, tp)
    tiles = 1 if tp == tq else 2
    assert tp % (tiles * tq) == 0
    vdim = MLA_HEADS * V_DIM
    stat = pltpu.VMEM((MLA_HEADS, SUBLANES, tq), F32)
    has_prefix = prefix is not None
    prefix_specs = ([_const_spec((N_META, QK_WIDTH)), _const_spec((vdim, LANES))]
                    if has_prefix else [])
    return pl.pallas_call(
        functools.partial(_attn_kernel, has_prefix),
        grid=(bsz, tp // (tiles * tq)),
        in_specs=[
            pl.BlockSpec((None, QT_ROWS, tp), lambda b, i: (b, 0, 0)),
            pl.BlockSpec((None, tp, QK_WIDTH), lambda b, i: (b, 0, 0)),
            pl.BlockSpec((None, vdim, tp), lambda b, i: (b, 0, 0)),
        ] + prefix_specs,
        out_specs=pl.BlockSpec((None, vdim, tiles * tq), lambda b, i: (b, 0, i)),
        out_shape=jax.ShapeDtypeStruct((bsz, vdim, tp), F32),
        scratch_shapes=[stat, stat, pltpu.VMEM((MLA_HEADS, V_DIM, tq), F32),
                        pltpu.VMEM((MLA_HEADS, tk, tq), F32),
                        pltpu.VMEM((MLA_HEADS, tk, tq), F32), stat, stat]
        + ([pltpu.VMEM((MLA_HEADS, N_META, tq), F32)] if has_prefix else []),
        compiler_params=pltpu.CompilerParams(
            dimension_semantics=("arbitrary", "arbitrary"), vmem_limit_bytes=VMEM_LIMIT),
        name="attention",
    )(qt, k, vt, *(prefix or ()))


def _rope_tables(tp):
    pos = jnp.arange(tp, dtype=F32)
    inv_freq = 1.0 / (ROPE_THETA ** (jnp.arange(0, QK_ROPE, 2, dtype=F32) / QK_ROPE))
    ang = pos[:, None] * inv_freq[None, :]
    cos, sin = jnp.cos(ang), jnp.sin(ang)
    half = QK_ROPE // 2
    zeros = lambda n: jnp.zeros((tp, n), F32)
    pad = HEAD_PAD - QK_NOPE - QK_ROPE
    c = jnp.concatenate([jnp.ones((tp, QK_NOPE), F32), cos, cos, zeros(pad)], axis=1)
    sa = jnp.concatenate([zeros(QK_NOPE + half), sin, zeros(pad)], axis=1)
    sb = jnp.concatenate([zeros(QK_NOPE), -sin, zeros(half + pad)], axis=1)
    return c, sa, sb, jnp.tile(cos, (1, MLA_HEADS)), jnp.tile(sin, (1, MLA_HEADS))


def _block_diag(w):
    depth = w.shape[0]
    per = LRU_HEADS // 2
    w5 = w.reshape(depth, 2, per, LRU_HEAD_DIM, LRU_HEAD_DIM)
    bd = jnp.einsum("dnhij,hg->dnhigj", w5, jnp.eye(per, dtype=w.dtype))
    return bd.reshape(depth, 2, per * LRU_HEAD_DIM, per * LRU_HEAD_DIM)


def _row(g):
    return g.reshape(g.shape[0], 1, g.shape[1])


@jax.jit
def kernel(x, meta_tokens, ffn1_pre_g, ffn1_w_gate, ffn1_w_up, ffn1_w_down, ffn1_post_g, mix_pre_g, w_in, lru_conv_w, lru_conv_b, lru_w_a, lru_b_a, lru_w_x, lru_b_x, lru_lambda, mla_q_norm_g, mla_w_uq, mla_kv_norm_g, mla_w_ukv, lru_out_g, mla_out_g, w_out, mix_post_g, ffn2_pre_g, ffn2_w_gate, ffn2_w_up, ffn2_w_down, ffn2_post_g):
    bsz, seq, _ = x.shape
    depth = w_in.shape[0]
    assert seq % SEQ_TILE == 0 and seq % MIX_TILE == 0
    h = x
    hm = jnp.concatenate([meta_tokens.astype(x.dtype),
                          jnp.zeros((META_ROWS - N_META, D_MODEL), x.dtype)], axis=0)[None]
    rope_all = _rope_tables(N_META + max(seq, META_ROWS))
    rope_meta = tuple(t[:META_ROWS] for t in rope_all)
    rope_main = tuple(t[N_META:N_META + seq] for t in rope_all)
    zero_hist = jnp.zeros((SUBLANES, LRU_WIDTH), F32)
    zero_state = jnp.zeros((1, LRU_WIDTH), F32)

    split = 2 * LRU_WIDTH + Q_LORA + KV_LORA
    zcols = lambda n: jnp.zeros((depth, D_MODEL, n), w_in.dtype)
    w_in_p = jnp.concatenate(
        [w_in[..., :split], zcols(QK_NOPE), w_in[..., split:],
         zcols(HEAD_PAD - QK_NOPE - QK_ROPE)], axis=-1).astype(BF16)
    w_uq4 = mla_w_uq.reshape(depth, Q_LORA, MLA_HEADS, QK_NOPE + QK_ROPE)
    half = QK_ROPE // 2
    w_uq_p = jnp.concatenate(
        [w_uq4[..., :QK_NOPE].reshape(depth, Q_LORA, Q_NOPE_ALL),
         w_uq4[..., QK_NOPE:QK_NOPE + half].reshape(depth, Q_LORA, Q_HALF_ALL),
         w_uq4[..., QK_NOPE + half:].reshape(depth, Q_LORA, Q_HALF_ALL)], axis=-1)
    w_ukv4 = mla_w_ukv.reshape(depth, KV_LORA, MLA_HEADS, QK_NOPE + V_DIM)
    w_uk_p = jnp.pad(w_ukv4[..., :QK_NOPE], ((0, 0), (0, 0), (0, 0), (0, HEAD_PAD - QK_NOPE)))
    mixp = {
        "mix_pre_g": _row(mix_pre_g),
        "w_in": w_in_p,
        "conv_w": lru_conv_w,
        "conv_b": _row(lru_conv_b),
        "wax": jnp.concatenate([_block_diag(lru_w_a), _block_diag(lru_w_x)], axis=-1).astype(BF16),
        "b_a": _row(lru_b_a),
        "b_x": _row(lru_b_x),
        "lam": _row(lru_lambda),
        "q_g": _row(mla_q_norm_g),
        "w_uq": w_uq_p.astype(BF16),
        "kv_g": _row(mla_kv_norm_g),
        "w_uk": w_uk_p.reshape(depth, KV_LORA, QK_WIDTH).astype(BF16),
        "w_uv": w_ukv4[..., QK_NOPE:].reshape(depth, KV_LORA, MLA_HEADS * V_DIM).astype(BF16),
        "lru_out_g": _row(lru_out_g),
        "mla_out_g": mla_out_g.reshape(depth, MLA_HEADS * V_DIM, 1),
        "w_out": w_out.astype(BF16),
        "mix_post_g": _row(mix_post_g),
    }
    ffn1 = (_row(ffn1_pre_g), ffn1_w_gate.astype(BF16), ffn1_w_up.astype(BF16),
            ffn1_w_down.astype(BF16), _row(ffn1_post_g))
    ffn2 = (_row(ffn2_pre_g), ffn2_w_gate.astype(BF16), ffn2_w_up.astype(BF16),
            ffn2_w_down.astype(BF16), _row(ffn2_post_g))

    lane = lax.broadcasted_iota(jnp.int32, (MLA_HEADS * V_DIM, META_ROWS), 1)
    for layer in range(depth):
        hm = _ffn(hm, layer, *ffn1)
        h = _ffn(h, layer, *ffn1)
        ylru_m, qt_m, k_m, vt_m, xr_m, hs_m = _mix_in(
            hm, layer, mixp, rope_meta, zero_hist, zero_state, emit_state=True)
        ylru, qt, k, vt = _mix_in(h, layer, mixp, rope_main,
                                  xr_m[0, N_META - SUBLANES:N_META], hs_m[0, N_META - 1:N_META])
        prefix = (k_m[0, :N_META], jnp.where(lane < N_META, vt_m[0], 0).astype(BF16))
        h = _ffn(h, layer, *ffn2, mixer=(ylru, _attention(qt, k, vt, prefix), mixp))
        if layer + 1 < depth:
            hm = _ffn(hm, layer, *ffn2, mixer=(ylru_m, _attention(qt_m, k_m, vt_m), mixp))
    return h
```

```python
import functools

import jax
import jax.numpy as jnp
from jax import lax
from jax.experimental import pallas as pl
from jax.experimental.pallas import tpu as pltpu

D_MODEL = 1024
N_META = 16
LRU_WIDTH = 512
LRU_HEADS = 8
LRU_HEAD_DIM = 64
CONV_WIDTH = 4
LRU_C = 8.0
MLA_HEADS = 8
QK_NOPE = 64
QK_ROPE = 32
V_DIM = 64
Q_LORA = 384
KV_LORA = 256
D_FF = 2816
ROPE_THETA = 10000.0
EPS = 1e-6

LANES = 128
SUBLANES = 8
HEAD_PAD = LANES
QK_WIDTH = MLA_HEADS * HEAD_PAD
Q_NOPE_ALL = MLA_HEADS * QK_NOPE
Q_HALF_ALL = MLA_HEADS * QK_ROPE // 2
QT_ROWS = Q_NOPE_ALL + 2 * Q_HALF_ALL
IN_COLS_PAD = 2 * LRU_WIDTH + Q_LORA + KV_LORA + HEAD_PAD
SEQ_TILE = 256
MIX_TILE = 1024
MIX_SUB = 256
META_ROWS = LANES
FFN_ROWS = 1024
FFN_SUBTILES = 2
FF_CHUNK = 256
NEG_BIG = -1e30
TINY = 1e-30
LOG2_E = 1.4426950408889634
VMEM_LIMIT = 56 * 1024 * 1024

F32 = jnp.float32
BF16 = jnp.bfloat16


def _rms(x, g):
    return x * lax.rsqrt(jnp.mean(x * x, axis=-1, keepdims=True) + EPS) * g


def _const_spec(shape, layer=None):
    if layer is None:
        idx = lambda *_: (0,) * len(shape)
        return pl.BlockSpec(shape, idx, pipeline_mode=pl.Buffered(1))
    idx = lambda *_: (layer,) + (0,) * len(shape)
    return pl.BlockSpec((None,) + shape, idx, pipeline_mode=pl.Buffered(1))


def _mix_out_rows(h, ylru, ot, gmla, wout_ref, gpost):
    ms = jnp.mean(ot * ot, axis=0, keepdims=True)
    on = (ot * lax.rsqrt(ms + EPS) * gmla).T.astype(BF16)
    y = jnp.dot(ylru, wout_ref[:LRU_WIDTH, :], preferred_element_type=F32)
    y = y + jnp.dot(on, wout_ref[LRU_WIDTH:, :], preferred_element_type=F32)
    return h + _rms(y, gpost)


def _ffn_kernel(after_mixer, h_ref, *refs):
    if after_mixer:
        ylru_ref, ot_ref, gmla_ref, wout_ref, gmix_ref = refs[:5]
        refs = refs[5:]
    gpre_ref, wg_ref, wu_ref, wd_ref, gpost_ref, o_ref, a_ref = refs
    rows = h_ref.shape[0]
    sub = rows // FFN_SUBTILES if after_mixer and rows % (FFN_SUBTILES * LANES) == 0 else rows
    parts = [slice(i * sub, (i + 1) * sub) for i in range(rows // sub)]
    xs = []
    for r in parts:
        x = h_ref[r, :]
        if after_mixer:
            x = _mix_out_rows(x, ylru_ref[r, :], ot_ref[:, r], gmla_ref[...], wout_ref,
                              gmix_ref[...])
        xs.append(x)
    for r, x in zip(parts, xs):
        u = _rms(x, gpre_ref[...]).astype(BF16)
        for c in range(D_FF // FF_CHUNK):
            sl = slice(c * FF_CHUNK, (c + 1) * FF_CHUNK)
            gate = jnp.dot(u, wg_ref[:, sl], preferred_element_type=F32)
            up = jnp.dot(u, wu_ref[:, sl], preferred_element_type=F32)
            a_ref[r, sl] = (gate * jax.nn.sigmoid(gate) * up).astype(BF16)
    for r, x in zip(parts, xs):
        f = jnp.dot(a_ref[r, :], wd_ref[...], preferred_element_type=F32)
        o_ref[r, :] = x + 0.5 * _rms(f, gpost_ref[...])


def _ffn(h, layer, gpre, wg, wu, wd, gpost, mixer=None):
    bsz, tp, _ = h.shape
    tile = min(FFN_ROWS, tp)
    assert tp % tile == 0
    row_spec = pl.BlockSpec((None, tile, D_MODEL), lambda b, t: (b, t, 0))
    mixer_specs, mixer_args = [], []
    if mixer is not None:
        ylru, ot, p = mixer
        vdim = MLA_HEADS * V_DIM
        mixer_specs = [
            pl.BlockSpec((None, tile, LRU_WIDTH), lambda b, t: (b, t, 0)),
            pl.BlockSpec((None, vdim, tile), lambda b, t: (b, 0, t)),
            _const_spec((vdim, 1), layer),
            _const_spec((D_MODEL, D_MODEL), layer),
            _const_spec((1, D_MODEL), layer),
        ]
        mixer_args = [ylru, ot, p["mla_out_g"], p["w_out"], p["mix_post_g"]]
    return pl.pallas_call(
        functools.partial(_ffn_kernel, mixer is not None),
        grid=(bsz, tp // tile),
        in_specs=[row_spec] + mixer_specs + [
            _const_spec((1, D_MODEL), layer),
            _const_spec((D_MODEL, D_FF), layer),
            _const_spec((D_MODEL, D_FF), layer),
            _const_spec((D_FF, D_MODEL), layer),
            _const_spec((1, D_MODEL), layer),
        ],
        out_specs=row_spec,
        out_shape=jax.ShapeDtypeStruct(h.shape, F32),
        scratch_shapes=[pltpu.VMEM((tile, D_FF), BF16)],
        compiler_params=pltpu.CompilerParams(
            dimension_semantics=("arbitrary", "arbitrary"), vmem_limit_bytes=VMEM_LIMIT),
        name="ffn_mix" if mixer is not None else "ffn",
    )(h, *mixer_args, gpre, wg, wu, wd, gpost)


def _rope(x, c, sa, sb):
    n = x.shape[-1]
    half = QK_ROPE // 2
    return x * c + pltpu.roll(x, half, 1) * sa + pltpu.roll(x, n - half, 1) * sb


def _mix_in_kernel(emit_state, h_ref, g_ref, win_ref, cw_ref, cb_ref, wax_ref, ba_ref, bx_ref,
                   lam_ref, gq_ref, wuq_ref, gkv_ref, wuk_ref, wuv_ref, glru_ref,
                   rc_ref, rsa_ref, rsb_ref, cos8_ref, sin8_ref, xprev0_ref, hcar0_ref,
                   ylru_ref, qt_ref, k_ref, vt_ref, *rest):
    if emit_state:
        xr_out_ref, hs_out_ref, xprev_ref, hcar_ref = rest
    else:
        xprev_ref, hcar_ref = rest
    tt = h_ref.shape[0]

    @pl.when(pl.program_id(1) == 0)
    def _():
        xprev_ref[...] = xprev0_ref[...]
        hcar_ref[...] = hcar0_ref[...]

    sub = min(MIX_SUB, tt)
    half = LRU_WIDTH // 2
    nl = -lam_ref[...]
    softplus = jnp.maximum(nl, 0.0) + jnp.log1p(jnp.exp(-jnp.abs(nl)))
    decay = (-LRU_C * LOG2_E) * softplus
    gq_scaled = gq_ref[...] * (LOG2_E * (QK_NOPE + QK_ROPE) ** -0.5)
    tile_heads = lambda t: jnp.concatenate([t] * MLA_HEADS, axis=1)

    def project(i):
        u = _rms(h_ref[i * sub:(i + 1) * sub, :], g_ref[...]).astype(BF16)
        return jnp.dot(u, win_ref[...], preferred_element_type=F32)

    xprev = xprev_ref[...]
    carry = hcar_ref[...]
    z_next = project(0)
    for i in range(tt // sub):
        z = z_next
        if (i + 1) * sub < tt:
            z_next = project(i + 1)
        rows = slice(i * sub, (i + 1) * sub)
        xr = z[:, 0:LRU_WIDTH]
        gr = z[:, LRU_WIDTH:2 * LRU_WIDTH]
        cq = z[:, 2 * LRU_WIDTH:2 * LRU_WIDTH + Q_LORA]
        ckv = z[:, 2 * LRU_WIDTH + Q_LORA:2 * LRU_WIDTH + Q_LORA + KV_LORA]
        krt = z[:, 2 * LRU_WIDTH + Q_LORA + KV_LORA:]

        xe = jnp.concatenate([xprev, xr], axis=0)
        xc = xr * cw_ref[CONV_WIDTH - 1:CONV_WIDTH, :] + cb_ref[...]
        for j in range(1, CONV_WIDTH):
            w_j = cw_ref[CONV_WIDTH - 1 - j:CONV_WIDTH - j, :]
            xc = xc + pltpu.roll(xe, j, 0)[SUBLANES:] * w_j
        xprev = xr[sub - SUBLANES:]

        xcb = xc.astype(BF16)
        ri0 = jnp.dot(xcb[:, :half], wax_ref[0], preferred_element_type=F32)
        ri1 = jnp.dot(xcb[:, half:], wax_ref[1], preferred_element_type=F32)
        r = jax.nn.sigmoid(jnp.concatenate([ri0[:, :half], ri1[:, :half]], axis=1) + ba_ref[...])
        g_in = jax.nn.sigmoid(
            jnp.concatenate([ri0[:, half:], ri1[:, half:]], axis=1) + bx_ref[...])
        a = jnp.exp2(r * decay)
        gap = 1.0 - a * a
        b = (gap * lax.rsqrt(jnp.maximum(gap, TINY))) * (g_in * xc)

        grouped = (sub // SUBLANES, SUBLANES, LRU_WIDTH)
        a = a.reshape(grouped)
        b = b.reshape(grouped)
        row = lax.broadcasted_iota(jnp.int32, grouped, 1)
        s = 1
        while s < SUBLANES:
            valid = row >= s
            b = a * jnp.where(valid, pltpu.roll(b, s, 1), 0.0) + b
            a = a * jnp.where(valid, pltpu.roll(a, s, 1), 1.0)
            s *= 2
        a = a.reshape(sub, LRU_WIDTH)
        b = b.reshape(sub, LRU_WIDTH)
        groups = []
        for gidx in range(sub // SUBLANES):
            sl = slice(gidx * SUBLANES, (gidx + 1) * SUBLANES)
            hg = b[sl] + a[sl] * carry
            carry = hg[SUBLANES - 1:SUBLANES]
            groups.append(hg)
        hs = jnp.concatenate(groups, axis=0)
        if emit_state:
            xr_out_ref[rows, :] = xr
            hs_out_ref[rows, :] = hs

        cg = 0.7978845608028654
        half_gr = 0.5 * gr
        gelu = half_gr + half_gr * jnp.tanh(gr * (cg + (cg * 0.044715) * (gr * gr)))
        ylru_ref[rows, :] = _rms(hs * gelu, glru_ref[...]).astype(BF16)

        rc = rc_ref[rows, :]
        rsa = rsa_ref[rows, :]
        rsb = rsb_ref[rows, :]
        cqn = _rms(cq, gq_scaled).astype(BF16)
        q = jnp.dot(cqn, wuq_ref[...], preferred_element_type=F32)
        q1 = q[:, Q_NOPE_ALL:Q_NOPE_ALL + Q_HALF_ALL]
        q2 = q[:, Q_NOPE_ALL + Q_HALF_ALL:]
        cos8 = cos8_ref[rows, :]
        sin8 = sin8_ref[rows, :]
        q = jnp.concatenate([q[:, :Q_NOPE_ALL], q1 * cos8 - q2 * sin8, q2 * cos8 + q1 * sin8],
                            axis=1)
        qt_ref[:, rows] = q.T.astype(BF16)

        ckvn = _rms(ckv, gkv_ref[...]).astype(BF16)
        kn = jnp.dot(ckvn, wuk_ref[...], preferred_element_type=F32)
        kr = _rope(krt, rc, rsa, rsb)
        k_ref[rows, :] = (kn + tile_heads(kr)).astype(BF16)
        v = jnp.dot(ckvn, wuv_ref[...], preferred_element_type=F32)
        vt_ref[:, rows] = v.T.astype(BF16)
    xprev_ref[...] = xprev
    hcar_ref[...] = carry


def _mix_in(h, layer, p, rope, xprev0, hcar0, emit_state=False):
    bsz, tp, _ = h.shape
    tt = min(MIX_TILE, tp)
    nt = tp // tt
    tbl_spec = pl.BlockSpec((tt, HEAD_PAD), lambda b, t: (t, 0))
    seq_spec = pl.BlockSpec((None, tt, LRU_WIDTH), lambda b, t: (b, t, 0))
    state_specs = [seq_spec, seq_spec] if emit_state else []
    state_shapes = [jax.ShapeDtypeStruct((bsz, tp, LRU_WIDTH), F32)] * 2 if emit_state else []
    return pl.pallas_call(
        functools.partial(_mix_in_kernel, emit_state),
        grid=(bsz, nt),
        in_specs=[
            pl.BlockSpec((None, tt, D_MODEL), lambda b, t: (b, t, 0)),
            _const_spec((1, D_MODEL), layer),
            _const_spec((D_MODEL, IN_COLS_PAD), layer),
            _const_spec((CONV_WIDTH, LRU_WIDTH), layer),
            _const_spec((1, LRU_WIDTH), layer),
            _const_spec((2, LRU_WIDTH // 2, LRU_WIDTH), layer),
            _const_spec((1, LRU_WIDTH), layer),
            _const_spec((1, LRU_WIDTH), layer),
            _const_spec((1, LRU_WIDTH), layer),
            _const_spec((1, Q_LORA), layer),
            _const_spec((Q_LORA, QT_ROWS), layer),
            _const_spec((1, KV_LORA), layer),
            _const_spec((KV_LORA, QK_WIDTH), layer),
            _const_spec((KV_LORA, MLA_HEADS * V_DIM), layer),
            _const_spec((1, LRU_WIDTH), layer),
            tbl_spec, tbl_spec, tbl_spec, tbl_spec, tbl_spec,
            _const_spec((SUBLANES, LRU_WIDTH)),
            _const_spec((1, LRU_WIDTH)),
        ],
        out_specs=[
            seq_spec,
            pl.BlockSpec((None, QT_ROWS, tt), lambda b, t: (b, 0, t)),
            pl.BlockSpec((None, tt, QK_WIDTH), lambda b, t: (b, t, 0)),
            pl.BlockSpec((None, MLA_HEADS * V_DIM, tt), lambda b, t: (b, 0, t)),
        ] + state_specs,
        out_shape=[
            jax.ShapeDtypeStruct((bsz, tp, LRU_WIDTH), BF16),
            jax.ShapeDtypeStruct((bsz, QT_ROWS, tp), BF16),
            jax.ShapeDtypeStruct((bsz, tp, QK_WIDTH), BF16),
            jax.ShapeDtypeStruct((bsz, MLA_HEADS * V_DIM, tp), BF16),
        ] + state_shapes,
        scratch_shapes=[pltpu.VMEM((SUBLANES, LRU_WIDTH), F32),
                        pltpu.VMEM((1, LRU_WIDTH), F32)],
        compiler_params=pltpu.CompilerParams(
            dimension_semantics=("arbitrary", "arbitrary"), vmem_limit_bytes=VMEM_LIMIT),
        name="mix_in",
    )(h, p["mix_pre_g"], p["w_in"], p["conv_w"], p["conv_b"], p["wax"], p["b_a"], p["b_x"],
      p["lam"], p["q_g"], p["w_uq"], p["kv_g"], p["w_uk"], p["w_uv"], p["lru_out_g"], *rope,
      xprev0, hcar0)


def _group_max(s):
    m = s[0:SUBLANES]
    for r in range(1, s.shape[0] // SUBLANES):
        m = jnp.maximum(m, s[r * SUBLANES:(r + 1) * SUBLANES])
    return m


def _sublane_allmax(m):
    for shift in (1, 2, 4):
        m = jnp.maximum(m, pltpu.roll(m, shift, 0))
    return m


def _attn_kernel(has_prefix, qt_ref, k_ref, vt_ref, *rest):
    if has_prefix:
        kpre_ref, vtpre_ref = rest[:2]
        rest = rest[2:]
        spre_ref = rest[-1]
        rest = rest[:-1]
    o_ref, m_ref, l_ref, acc_ref, s0_ref, s1_ref, bm0_ref, bm1_ref = rest
    step = pl.program_id(1)
    tk, tq = s0_ref.shape[1:]
    tiles = o_ref.shape[1] // tq
    assert tiles in (1, 2)
    last_tile = pl.num_programs(1) * tiles - 1
    s_refs = (s0_ref, s1_ref)
    bm_refs = (bm0_ref, bm1_ref)
    ones_rows = 2 * SUBLANES

    def with_ones(vb):
        return jnp.concatenate([vb, jnp.ones((ones_rows, vb.shape[1]), BF16)], axis=0)

    def q_head(g, tile):
        half = QK_ROPE // 2
        r1 = Q_NOPE_ALL + g * half
        r2 = r1 + Q_HALF_ALL
        cols = pl.ds(pl.multiple_of(tile * tq, tq), tq)
        return jnp.concatenate(
            [qt_ref[g * QK_NOPE:(g + 1) * QK_NOPE, cols], qt_ref[r1:r1 + half, cols],
             qt_ref[r2:r2 + half, cols], jnp.zeros((HEAD_PAD - QK_NOPE - QK_ROPE, tq), BF16)],
            axis=0)

    def scores(j, g, slot, tile):
        off = pl.multiple_of(j * tk, tk)
        s = jnp.dot(k_ref[pl.ds(off, tk), g * HEAD_PAD:(g + 1) * HEAD_PAD], q_head(g, tile),
                    preferred_element_type=F32)
        s_refs[slot][g] = s
        bm_refs[slot][g] = _group_max(s)

    def first_scores(g, tile):
        scores(0, g, 0, tile)
        if has_prefix:
            spre_ref[g] = jnp.dot(kpre_ref[:, g * HEAD_PAD:(g + 1) * HEAD_PAD], q_head(g, tile),
                                  preferred_element_type=F32)

    def softmax_pv(j, g, slot, masked):
        s = s_refs[slot][g]
        if masked:
            key = lax.broadcasted_iota(jnp.int32, s.shape, 0)
            qry = lax.broadcasted_iota(jnp.int32, s.shape, 1)
            s = jnp.where(key <= qry, s, NEG_BIG)
            bm = _group_max(s)
            if has_prefix:
                s_pre = spre_ref[g]
                bm = jnp.maximum(bm, _group_max(s_pre))
        else:
            bm = bm_refs[slot][g]
        m_old = m_ref[g]
        m_new = jnp.maximum(m_old, _sublane_allmax(bm))
        alpha = jnp.exp2(m_old - m_new)
        p = jnp.exp2(s - jnp.concatenate([m_new] * (tk // SUBLANES), axis=0))
        m_ref[g] = m_new
        off = pl.multiple_of(j * tk, tk)
        vb = with_ones(vt_ref[g * V_DIM:(g + 1) * V_DIM, pl.ds(off, tk)])
        pv = jnp.dot(vb, p.astype(BF16), preferred_element_type=F32)
        if masked and has_prefix:
            p_pre = jnp.exp2(s_pre - jnp.concatenate([m_new] * (N_META // SUBLANES), axis=0))
            p_pre = jnp.concatenate(
                [p_pre.astype(BF16), jnp.zeros((LANES - N_META, tq), BF16)], axis=0)
            pv = pv + jnp.dot(with_ones(vtpre_ref[g * V_DIM:(g + 1) * V_DIM, :]), p_pre,
                              preferred_element_type=F32)
        acc_ref[g] = (jnp.concatenate([alpha] * (V_DIM // SUBLANES), axis=0) * acc_ref[g]
                      + pv[:V_DIM])
        l_ref[g] = alpha * l_ref[g] + pv[V_DIM:V_DIM + SUBLANES]

    def query_tile(sub):
        qi = step * tiles + sub
        m_ref[...] = jnp.full(m_ref.shape, NEG_BIG, F32)
        l_ref[...] = jnp.zeros(l_ref.shape, F32)
        acc_ref[...] = jnp.zeros(acc_ref.shape, F32)

        def block_step(j_next, slot_next, j, slot):
            for g in range(MLA_HEADS):
                scores(j_next, g, slot_next, qi)
                softmax_pv(j, g, slot, False)

        def last_step(slot):
            nxt = jnp.minimum(qi + 1, last_tile)
            for g in range(MLA_HEADS):
                softmax_pv(qi, g, slot, True)
                first_scores(g, nxt)

        if sub == 0:
            @pl.when(step == 0)
            def _():
                for g in range(MLA_HEADS):
                    first_scores(g, 0)

        def pair(i, carry):
            j = 2 * i
            block_step(j + 1, 1, j, 0)
            block_step(j + 2, 0, j + 1, 1)
            return carry

        if tiles == 2:
            lax.fori_loop(0, step, pair, 0)
        if sub == 0:
            last_step(0)
        else:
            block_step(qi, 1, qi - 1, 0)
            last_step(1)

        cols = slice(sub * tq, (sub + 1) * tq)
        for g in range(MLA_HEADS):
            inv_l = 1.0 / jnp.concatenate([l_ref[g]] * (V_DIM // SUBLANES), axis=0)
            o_ref[g * V_DIM:(g + 1) * V_DIM, cols] = acc_ref[g] * inv_l

    for sub in range(tiles):
        query_tile(sub)


def _attention(qt, k, vt, prefix=None):
    bsz, _, tp = qt.shape
    tq = tk = min(SEQ_TILE, tp)
    tiles = 1 if tp == tq else 2
    assert tp % (tiles * tq) == 0
    vdim = MLA_HEADS * V_DIM
    stat = pltpu.VMEM((MLA_HEADS, SUBLANES, tq), F32)
    has_prefix = prefix is not None
    prefix_specs = ([_const_spec((N_META, QK_WIDTH)), _const_spec((vdim, LANES))]
                    if has_prefix else [])
    return pl.pallas_call(
        functools.partial(_attn_kernel, has_prefix),
        grid=(bsz, tp // (tiles * tq)),
        in_specs=[
            pl.BlockSpec((None, QT_ROWS, tp), lambda b, i: (b, 0, 0)),
            pl.BlockSpec((None, tp, QK_WIDTH), lambda b, i: (b, 0, 0)),
            pl.BlockSpec((None, vdim, tp), lambda b, i: (b, 0, 0)),
        ] + prefix_specs,
        out_specs=pl.BlockSpec((None, vdim, tiles * tq), lambda b, i: (b, 0, i)),
        out_shape=jax.ShapeDtypeStruct((bsz, vdim, tp), F32),
        scratch_shapes=[stat, stat, pltpu.VMEM((MLA_HEADS, V_DIM, tq), F32),
                        pltpu.VMEM((MLA_HEADS, tk, tq), F32),
                        pltpu.VMEM((MLA_HEADS, tk, tq), F32), stat, stat]
        + ([pltpu.VMEM((MLA_HEADS, N_META, tq), F32)] if has_prefix else []),
        compiler_params=pltpu.CompilerParams(
            dimension_semantics=("arbitrary", "arbitrary"), vmem_limit_bytes=VMEM_LIMIT),
        name="attention",
    )(qt, k, vt, *(prefix or ()))


def _rope_tables(tp):
    pos = jnp.arange(tp, dtype=F32)
    inv_freq = 1.0 / (ROPE_THETA ** (jnp.arange(0, QK_ROPE, 2, dtype=F32) / QK_ROPE))
    ang = pos[:, None] * inv_freq[None, :]
    cos, sin = jnp.cos(ang), jnp.sin(ang)
    half = QK_ROPE // 2
    zeros = lambda n: jnp.zeros((tp, n), F32)
    pad = HEAD_PAD - QK_NOPE - QK_ROPE
    c = jnp.concatenate([jnp.ones((tp, QK_NOPE), F32), cos, cos, zeros(pad)], axis=1)
    sa = jnp.concatenate([zeros(QK_NOPE + half), sin, zeros(pad)], axis=1)
    sb = jnp.concatenate([zeros(QK_NOPE), -sin, zeros(half + pad)], axis=1)
    return c, sa, sb, jnp.tile(cos, (1, MLA_HEADS)), jnp.tile(sin, (1, MLA_HEADS))


def _block_diag(w):
    depth = w.shape[0]
    per = LRU_HEADS // 2
    w5 = w.reshape(depth, 2, per, LRU_HEAD_DIM, LRU_HEAD_DIM)
    bd = jnp.einsum("dnhij,hg->dnhigj", w5, jnp.eye(per, dtype=w.dtype))
    return bd.reshape(depth, 2, per * LRU_HEAD_DIM, per * LRU_HEAD_DIM)


def _row(g):
    return g.reshape(g.shape[0], 1, g.shape[1])


@jax.jit
def kernel(x, meta_tokens, ffn1_pre_g, ffn1_w_gate, ffn1_w_up, ffn1_w_down, ffn1_post_g, mix_pre_g, w_in, lru_conv_w, lru_conv_b, lru_w_a, lru_b_a, lru_w_x, lru_b_x, lru_lambda, mla_q_norm_g, mla_w_uq, mla_kv_norm_g, mla_w_ukv, lru_out_g, mla_out_g, w_out, mix_post_g, ffn2_pre_g, ffn2_w_gate, ffn2_w_up, ffn2_w_down, ffn2_post_g):
    bsz, seq, _ = x.shape
    depth = w_in.shape[0]
    assert seq % SEQ_TILE == 0 and seq % MIX_TILE == 0
    h = x
    hm = jnp.concatenate([meta_tokens.astype(x.dtype),
                          jnp.zeros((META_ROWS - N_META, D_MODEL), x.dtype)], axis=0)[None]
    rope_all = _rope_tables(N_META + max(seq, META_ROWS))
    rope_meta = tuple(t[:META_ROWS] for t in rope_all)
    rope_main = tuple(t[N_META:N_META + seq] for t in rope_all)
    zero_hist = jnp.zeros((SUBLANES, LRU_WIDTH), F32)
    zero_state = jnp.zeros((1, LRU_WIDTH), F32)

    split = 2 * LRU_WIDTH + Q_LORA + KV_LORA
    zcols = lambda n: jnp.zeros((depth, D_MODEL, n), w_in.dtype)
    w_in_p = jnp.concatenate(
        [w_in[..., :split], zcols(QK_NOPE), w_in[..., split:],
         zcols(HEAD_PAD - QK_NOPE - QK_ROPE)], axis=-1).astype(BF16)
    w_uq4 = mla_w_uq.reshape(depth, Q_LORA, MLA_HEADS, QK_NOPE + QK_ROPE)
    half = QK_ROPE // 2
    w_uq_p = jnp.concatenate(
        [w_uq4[..., :QK_NOPE].reshape(depth, Q_LORA, Q_NOPE_ALL),
         w_uq4[..., QK_NOPE:QK_NOPE + half].reshape(depth, Q_LORA, Q_HALF_ALL),
         w_uq4[..., QK_NOPE + half:].reshape(depth, Q_LORA, Q_HALF_ALL)], axis=-1)
    w_ukv4 = mla_w_ukv.reshape(depth, KV_LORA, MLA_HEADS, QK_NOPE + V_DIM)
    w_uk_p = jnp.pad(w_ukv4[..., :QK_NOPE], ((0, 0), (0, 0), (0, 0), (0, HEAD_PAD - QK_NOPE)))
    mixp = {
        "mix_pre_g": _row(mix_pre_g),
        "w_in": w_in_p,
        "conv_w": lru_conv_w,
        "conv_b": _row(lru_conv_b),
        "wax": jnp.concatenate([_block_diag(lru_w_a), _block_diag(lru_w_x)], axis=-1).astype(BF16),
        "b_a": _row(lru_b_a),
        "b_x": _row(lru_b_x),
        "lam": _row(lru_lambda),
        "q_g": _row(mla_q_norm_g),
        "w_uq": w_uq_p.astype(BF16),
        "kv_g": _row(mla_kv_norm_g),
        "w_uk": w_uk_p.reshape(depth, KV_LORA, QK_WIDTH).astype(BF16),
        "w_uv": w_ukv4[..., QK_NOPE:].reshape(depth, KV_LORA, MLA_HEADS * V_DIM).astype(BF16),
        "lru_out_g": _row(lru_out_g),
        "mla_out_g": mla_out_g.reshape(depth, MLA_HEADS * V_DIM, 1),
        "w_out": w_out.astype(BF16),
        "mix_post_g": _row(mix_post_g),
    }
    ffn1 = (_row(ffn1_pre_g), ffn1_w_gate.astype(BF16), ffn1_w_up.astype(BF16),
            ffn1_w_down.astype(BF16), _row(ffn1_post_g))
    ffn2 = (_row(ffn2_pre_g), ffn2_w_gate.astype(BF16), ffn2_w_up.astype(BF16),
            ffn2_w_down.astype(BF16), _row(ffn2_post_g))

    lane = lax.broadcasted_iota(jnp.int32, (MLA_HEADS * V_DIM, META_ROWS), 1)
    for layer in range(depth):
        hm = _ffn(hm, layer, *ffn1)
        h = _ffn(h, layer, *ffn1)
        ylru_m, qt_m, k_m, vt_m, xr_m, hs_m = _mix_in(
            hm, layer, mixp, rope_meta, zero_hist, zero_state, emit_state=True)
        ylru, qt, k, vt = _mix_in(h, layer, mixp, rope_main,
                                  xr_m[0, N_META - SUBLANES:N_META], hs_m[0, N_META - 1:N_META])
        prefix = (k_m[0, :N_META], jnp.where(lane < N_META, vt_m[0], 0).astype(BF16))
        h = _ffn(h, layer, *ffn2, mixer=(ylru, _attention(qt, k, vt, prefix), mixp))
        if layer + 1 < depth:
            hm = _ffn(hm, layer, *ffn2, mixer=(ylru_m, _attention(qt_m, k_m, vt_m), mixp))
    return h
```

```python
import functools

import jax
import jax.numpy as jnp
from jax import lax
from jax.experimental import pallas as pl
from jax.experimental.pallas import tpu as pltpu

D_MODEL = 1024
N_META = 16
LRU_WIDTH = 512
LRU_HEADS = 8
LRU_HEAD_DIM = 64
CONV_WIDTH = 4
LRU_C = 8.0
MLA_HEADS = 8
QK_NOPE = 64
QK_ROPE = 32
V_DIM = 64
Q_LORA = 384
KV_LORA = 256
D_FF = 2816
ROPE_THETA = 10000.0
EPS = 1e-6

LANES = 128
SUBLANES = 8
HEAD_PAD = LANES
QK_WIDTH = MLA_HEADS * HEAD_PAD
Q_NOPE_ALL = MLA_HEADS * QK_NOPE
Q_HALF_ALL = MLA_HEADS * QK_ROPE // 2
QT_ROWS = Q_NOPE_ALL + 2 * Q_HALF_ALL
IN_COLS_PAD = 2 * LRU_WIDTH + Q_LORA + KV_LORA + HEAD_PAD
SEQ_TILE = 256
MIX_TILE = 1024
MIX_SUB = 256
META_ROWS = LANES
FFN_ROWS = 512
FFN_SUBTILES = 2
FF_CHUNK = 256
NEG_BIG = -1e30
TINY = 1e-30
LOG2_E = 1.4426950408889634
VMEM_LIMIT = 52 * 1024 * 1024

F32 = jnp.float32
BF16 = jnp.bfloat16


def _rms(x, g):
    return x * lax.rsqrt(jnp.mean(x * x, axis=-1, keepdims=True) + EPS) * g


def _const_spec(shape, layer=None):
    if layer is None:
        idx = lambda *_: (0,) * len(shape)
        return pl.BlockSpec(shape, idx, pipeline_mode=pl.Buffered(1))
    idx = lambda *_: (layer,) + (0,) * len(shape)
    return pl.BlockSpec((None,) + shape, idx, pipeline_mode=pl.Buffered(1))


def _mix_out_rows(h, ylru, ot, gmla, wout_ref, gpost):
    ms = jnp.mean(ot * ot, axis=0, keepdims=True)
    on = (ot * lax.rsqrt(ms + EPS) * gmla).T.astype(BF16)
    y = jnp.dot(ylru, wout_ref[:LRU_WIDTH, :], preferred_element_type=F32)
    y = y + jnp.dot(on, wout_ref[LRU_WIDTH:, :], preferred_element_type=F32)
    return h + _rms(y, gpost)


def _ffn_kernel(after_mixer, h_ref, *refs):
    if after_mixer:
        ylru_ref, ot_ref, gmla_ref, wout_ref, gmix_ref = refs[:5]
        refs = refs[5:]
    gpre_ref, wg_ref, wu_ref, wd_ref, gpost_ref, o_ref, a_ref = refs
    rows = h_ref.shape[0]
    sub = rows // FFN_SUBTILES if after_mixer and rows % (FFN_SUBTILES * LANES) == 0 else rows
    parts = [slice(i * sub, (i + 1) * sub) for i in range(rows // sub)]
    xs = []
    for r in parts:
        x = h_ref[r, :]
        if after_mixer:
            x = _mix_out_rows(x, ylru_ref[r, :], ot_ref[:, r], gmla_ref[...], wout_ref,
                              gmix_ref[...])
        xs.append(x)
    for r, x in zip(parts, xs):
        u = _rms(x, gpre_ref[...]).astype(BF16)
        for c in range(D_FF // FF_CHUNK):
            sl = slice(c * FF_CHUNK, (c + 1) * FF_CHUNK)
            gate = jnp.dot(u, wg_ref[:, sl], preferred_element_type=F32)
            up = jnp.dot(u, wu_ref[:, sl], preferred_element_type=F32)
            a_ref[r, sl] = (gate * jax.nn.sigmoid(gate) * up).astype(BF16)
    for r, x in zip(parts, xs):
        f = jnp.dot(a_ref[r, :], wd_ref[...], preferred_element_type=F32)
        o_ref[r, :] = x + 0.5 * _rms(f, gpost_ref[...])


def _ffn(h, layer, gpre, wg, wu, wd, gpost, mixer=None):
    bsz, tp, _ = h.shape
    tile = min(FFN_ROWS if mixer is not None else 2 * FFN_ROWS, tp)
    assert tp % tile == 0
    row_spec = pl.BlockSpec((None, tile, D_MODEL), lambda b, t: (b, t, 0))
    mixer_specs, mixer_args = [], []
    if mixer is not None:
        ylru, ot, p = mixer
        vdim = MLA_HEADS * V_DIM
        mixer_specs = [
            pl.BlockSpec((None, tile, LRU_WIDTH), lambda b, t: (b, t, 0)),
            pl.BlockSpec((None, vdim, tile), lambda b, t: (b, 0, t)),
            _const_spec((vdim, 1), layer),
            _const_spec((D_MODEL, D_MODEL), layer),
            _const_spec((1, D_MODEL), layer),
        ]
        mixer_args = [ylru, ot, p["mla_out_g"], p["w_out"], p["mix_post_g"]]
    return pl.pallas_call(
        functools.partial(_ffn_kernel, mixer is not None),
        grid=(bsz, tp // tile),
        in_specs=[row_spec] + mixer_specs + [
            _const_spec((1, D_MODEL), layer),
            _const_spec((D_MODEL, D_FF), layer),
            _const_spec((D_MODEL, D_FF), layer),
            _const_spec((D_FF, D_MODEL), layer),
            _const_spec((1, D_MODEL), layer),
        ],
        out_specs=row_spec,
        out_shape=jax.ShapeDtypeStruct(h.shape, F32),
        scratch_shapes=[pltpu.VMEM((tile, D_FF), BF16)],
        compiler_params=pltpu.CompilerParams(
            dimension_semantics=("arbitrary", "arbitrary"), vmem_limit_bytes=VMEM_LIMIT),
        name="ffn_mix" if mixer is not None else "ffn",
    )(h, *mixer_args, gpre, wg, wu, wd, gpost)


def _rope(x, c, sa, sb):
    n = x.shape[-1]
    half = QK_ROPE // 2
    return x * c + pltpu.roll(x, half, 1) * sa + pltpu.roll(x, n - half, 1) * sb


def _mix_in_kernel(emit_state, h_ref, g_ref, win_ref, cw_ref, cb_ref, wax_ref, ba_ref, bx_ref,
                   lam_ref, gq_ref, wuq_ref, gkv_ref, wuk_ref, wuv_ref, glru_ref,
                   rc_ref, rsa_ref, rsb_ref, cos8_ref, sin8_ref, xprev0_ref, hcar0_ref,
                   ylru_ref, qt_ref, k_ref, vt_ref, *rest):
    if emit_state:
        xr_out_ref, hs_out_ref, xprev_ref, hcar_ref = rest
    else:
        xprev_ref, hcar_ref = rest
    tt = h_ref.shape[0]

    @pl.when(pl.program_id(1) == 0)
    def _():
        xprev_ref[...] = xprev0_ref[...]
        hcar_ref[...] = hcar0_ref[...]

    sub = min(MIX_SUB, tt)
    half = LRU_WIDTH // 2
    nl = -lam_ref[...]
    softplus = jnp.maximum(nl, 0.0) + jnp.log1p(jnp.exp(-jnp.abs(nl)))
    decay = (-LRU_C * LOG2_E) * softplus
    gq_scaled = gq_ref[...] * (LOG2_E * (QK_NOPE + QK_ROPE) ** -0.5)
    tile_heads = lambda t: jnp.concatenate([t] * MLA_HEADS, axis=1)

    def project(i):
        u = _rms(h_ref[i * sub:(i + 1) * sub, :], g_ref[...]).astype(BF16)
        return jnp.dot(u, win_ref[...], preferred_element_type=F32)

    xprev = xprev_ref[...]
    carry = hcar_ref[...]
    z_next = project(0)
    for i in range(tt // sub):
        z = z_next
        if (i + 1) * sub < tt:
            z_next = project(i + 1)
        rows = slice(i * sub, (i + 1) * sub)
        xr = z[:, 0:LRU_WIDTH]
        gr = z[:, LRU_WIDTH:2 * LRU_WIDTH]
        cq = z[:, 2 * LRU_WIDTH:2 * LRU_WIDTH + Q_LORA]
        ckv = z[:, 2 * LRU_WIDTH + Q_LORA:2 * LRU_WIDTH + Q_LORA + KV_LORA]
        krt = z[:, 2 * LRU_WIDTH + Q_LORA + KV_LORA:]

        xe = jnp.concatenate([xprev, xr], axis=0)
        xc = xr * cw_ref[CONV_WIDTH - 1:CONV_WIDTH, :] + cb_ref[...]
        for j in range(1, CONV_WIDTH):
            w_j = cw_ref[CONV_WIDTH - 1 - j:CONV_WIDTH - j, :]
            xc = xc + pltpu.roll(xe, j, 0)[SUBLANES:] * w_j
        xprev = xr[sub - SUBLANES:]

        xcb = xc.astype(BF16)
        ri0 = jnp.dot(xcb[:, :half], wax_ref[0], preferred_element_type=F32)
        ri1 = jnp.dot(xcb[:, half:], wax_ref[1], preferred_element_type=F32)
        r = jax.nn.sigmoid(jnp.concatenate([ri0[:, :half], ri1[:, :half]], axis=1) + ba_ref[...])
        g_in = jax.nn.sigmoid(
            jnp.concatenate([ri0[:, half:], ri1[:, half:]], axis=1) + bx_ref[...])
        a = jnp.exp2(r * decay)
        gap = 1.0 - a * a
        b = (gap * lax.rsqrt(jnp.maximum(gap, TINY))) * (g_in * xc)

        grouped = (sub // SUBLANES, SUBLANES, LRU_WIDTH)
        a = a.reshape(grouped)
        b = b.reshape(grouped)
        row = lax.broadcasted_iota(jnp.int32, grouped, 1)
        s = 1
        while s < SUBLANES:
            valid = row >= s
            b = a * jnp.where(valid, pltpu.roll(b, s, 1), 0.0) + b
            a = a * jnp.where(valid, pltpu.roll(a, s, 1), 1.0)
            s *= 2
        a = a.reshape(sub, LRU_WIDTH)
        b = b.reshape(sub, LRU_WIDTH)
        groups = []
        for gidx in range(sub // SUBLANES):
            sl = slice(gidx * SUBLANES, (gidx + 1) * SUBLANES)
            hg = b[sl] + a[sl] * carry
            carry = hg[SUBLANES - 1:SUBLANES]
            groups.append(hg)
        hs = jnp.concatenate(groups, axis=0)
        if emit_state:
            xr_out_ref[rows, :] = xr
            hs_out_ref[rows, :] = hs

        cg = 0.7978845608028654
        half_gr = 0.5 * gr
        gelu = half_gr + half_gr * jnp.tanh(gr * (cg + (cg * 0.044715) * (gr * gr)))
        ylru_ref[rows, :] = _rms(hs * gelu, glru_ref[...]).astype(BF16)

        rc = rc_ref[rows, :]
        rsa = rsa_ref[rows, :]
        rsb = rsb_ref[rows, :]
        cqn = _rms(cq, gq_scaled).astype(BF16)
        q = jnp.dot(cqn, wuq_ref[...], preferred_element_type=F32)
        q1 = q[:, Q_NOPE_ALL:Q_NOPE_ALL + Q_HALF_ALL]
        q2 = q[:, Q_NOPE_ALL + Q_HALF_ALL:]
        cos8 = cos8_ref[rows, :]
        sin8 = sin8_ref[rows, :]
        q = jnp.concatenate([q[:, :Q_NOPE_ALL], q1 * cos8 - q2 * sin8, q2 * cos8 + q1 * sin8],
                            axis=1)
        qt_ref[:, rows] = q.T.astype(BF16)

        ckvn = _rms(ckv, gkv_ref[...]).astype(BF16)
        kn = jnp.dot(ckvn, wuk_ref[...], preferred_element_type=F32)
        kr = _rope(krt, rc, rsa, rsb)
        k_ref[rows, :] = (kn + tile_heads(kr)).astype(BF16)
        v = jnp.dot(ckvn, wuv_ref[...], preferred_element_type=F32)
        vt_ref[:, rows] = v.T.astype(BF16)
    xprev_ref[...] = xprev
    hcar_ref[...] = carry


def _mix_in(h, layer, p, rope, xprev0, hcar0, emit_state=False):
    bsz, tp, _ = h.shape
    tt = min(MIX_TILE, tp)
    nt = tp // tt
    tbl_spec = pl.BlockSpec((tt, HEAD_PAD), lambda b, t: (t, 0))
    seq_spec = pl.BlockSpec((None, tt, LRU_WIDTH), lambda b, t: (b, t, 0))
    state_specs = [seq_spec, seq_spec] if emit_state else []
    state_shapes = [jax.ShapeDtypeStruct((bsz, tp, LRU_WIDTH), F32)] * 2 if emit_state else []
    return pl.pallas_call(
        functools.partial(_mix_in_kernel, emit_state),
        grid=(bsz, nt),
        in_specs=[
            pl.BlockSpec((None, tt, D_MODEL), lambda b, t: (b, t, 0)),
            _const_spec((1, D_MODEL), layer),
            _const_spec((D_MODEL, IN_COLS_PAD), layer),
            _const_spec((CONV_WIDTH, LRU_WIDTH), layer),
            _const_spec((1, LRU_WIDTH), layer),
            _const_spec((2, LRU_WIDTH // 2, LRU_WIDTH), layer),
            _const_spec((1, LRU_WIDTH), layer),
            _const_spec((1, LRU_WIDTH), layer),
            _const_spec((1, LRU_WIDTH), layer),
            _const_spec((1, Q_LORA), layer),
            _const_spec((Q_LORA, QT_ROWS), layer),
            _const_spec((1, KV_LORA), layer),
            _const_spec((KV_LORA, QK_WIDTH), layer),
            _const_spec((KV_LORA, MLA_HEADS * V_DIM), layer),
            _const_spec((1, LRU_WIDTH), layer),
            tbl_spec, tbl_spec, tbl_spec, tbl_spec, tbl_spec,
            _const_spec((SUBLANES, LRU_WIDTH)),
            _const_spec((1, LRU_WIDTH)),
        ],
        out_specs=[
            seq_spec,
            pl.BlockSpec((None, QT_ROWS, tt), lambda b, t: (b, 0, t)),
            pl.BlockSpec((None, tt, QK_WIDTH), lambda b, t: (b, t, 0)),
            pl.BlockSpec((None, MLA_HEADS * V_DIM, tt), lambda b, t: (b, 0, t)),
        ] + state_specs,
        out_shape=[
            jax.ShapeDtypeStruct((bsz, tp, LRU_WIDTH), BF16),
            jax.ShapeDtypeStruct((bsz, QT_ROWS, tp), BF16),
            jax.ShapeDtypeStruct((bsz, tp, QK_WIDTH), BF16),
            jax.ShapeDtypeStruct((bsz, MLA_HEADS * V_DIM, tp), BF16),
        ] + state_shapes,
        scratch_shapes=[pltpu.VMEM((SUBLANES, LRU_WIDTH), F32),
                        pltpu.VMEM((1, LRU_WIDTH), F32)],
        compiler_params=pltpu.CompilerParams(
            dimension_semantics=("arbitrary", "arbitrary"), vmem_limit_bytes=VMEM_LIMIT),
        name="mix_in",
    )(h, p["mix_pre_g"], p["w_in"], p["conv_w"], p["conv_b"], p["wax"], p["b_a"], p["b_x"],
      p["lam"], p["q_g"], p["w_uq"], p["kv_g"], p["w_uk"], p["w_uv"], p["lru_out_g"], *rope,
      xprev0, hcar0)


def _group_max(s):
    m = s[0:SUBLANES]
    for r in range(1, s.shape[0] // SUBLANES):
        m = jnp.maximum(m, s[r * SUBLANES:(r + 1) * SUBLANES])
    return m


def _sublane_allmax(m):
    for shift in (1, 2, 4):
        m = jnp.maximum(m, pltpu.roll(m, shift, 0))
    return m


def _attn_kernel(has_prefix, qt_ref, k_ref, vt_ref, *rest):
    if has_prefix:
        kpre_ref, vtpre_ref = rest[:2]
        rest = rest[2:]
        spre_ref = rest[-1]
        rest = rest[:-1]
    o_ref, m_ref, l_ref, acc_ref, s0_ref, s1_ref, bm0_ref, bm1_ref = rest
    step = pl.program_id(1)
    tk, tq = s0_ref.shape[1:]
    tiles = o_ref.shape[1] // tq
    assert tiles in (1, 2)
    last_tile = pl.num_programs(1) * tiles - 1
    s_refs = (s0_ref, s1_ref)
    bm_refs = (bm0_ref, bm1_ref)
    ones_rows = 2 * SUBLANES

    def with_ones(vb):
        return jnp.concatenate([vb, jnp.ones((ones_rows, vb.shape[1]), BF16)], axis=0)

    def q_head(g, tile):
        half = QK_ROPE // 2
        r1 = Q_NOPE_ALL + g * half
        r2 = r1 + Q_HALF_ALL
        cols = pl.ds(pl.multiple_of(tile * tq, tq), tq)
        return jnp.concatenate(
            [qt_ref[g * QK_NOPE:(g + 1) * QK_NOPE, cols], qt_ref[r1:r1 + half, cols],
             qt_ref[r2:r2 + half, cols], jnp.zeros((HEAD_PAD - QK_NOPE - QK_ROPE, tq), BF16)],
            axis=0)

    def scores(j, g, slot, tile):
        off = pl.multiple_of(j * tk, tk)
        s = jnp.dot(k_ref[pl.ds(off, tk), g * HEAD_PAD:(g + 1) * HEAD_PAD], q_head(g, tile),
                    preferred_element_type=F32)
        s_refs[slot][g] = s
        bm_refs[slot][g] = _group_max(s)

    def first_scores(g, tile):
        scores(0, g, 0, tile)
        if has_prefix:
            spre_ref[g] = jnp.dot(kpre_ref[:, g * HEAD_PAD:(g + 1) * HEAD_PAD], q_head(g, tile),
                                  preferred_element_type=F32)

    def softmax_pv(j, g, slot, masked):
        s = s_refs[slot][g]
        if masked:
            key = lax.broadcasted_iota(jnp.int32, s.shape, 0)
            qry = lax.broadcasted_iota(jnp.int32, s.shape, 1)
            s = jnp.where(key <= qry, s, NEG_BIG)
            bm = _group_max(s)
            if has_prefix:
                s_pre = spre_ref[g]
                bm = jnp.maximum(bm, _group_max(s_pre))
        else:
            bm = bm_refs[slot][g]
        m_old = m_ref[g]
        m_new = jnp.maximum(m_old, _sublane_allmax(bm))
        alpha = jnp.exp2(m_old - m_new)
        p = jnp.exp2(s - jnp.concatenate([m_new] * (tk // SUBLANES), axis=0))
        m_ref[g] = m_new
        off = pl.multiple_of(j * tk, tk)
        vb = with_ones(vt_ref[g * V_DIM:(g + 1) * V_DIM, pl.ds(off, tk)])
        pv = jnp.dot(vb, p.astype(BF16), preferred_element_type=F32)
        if masked and has_prefix:
            p_pre = jnp.exp2(s_pre - jnp.concatenate([m_new] * (N_META // SUBLANES), axis=0))
            p_pre = jnp.concatenate(
                [p_pre.astype(BF16), jnp.zeros((LANES - N_META, tq), BF16)], axis=0)
            pv = pv + jnp.dot(with_ones(vtpre_ref[g * V_DIM:(g + 1) * V_DIM, :]), p_pre,
                              preferred_element_type=F32)
        acc_ref[g] = (jnp.concatenate([alpha] * (V_DIM // SUBLANES), axis=0) * acc_ref[g]
                      + pv[:V_DIM])
        l_ref[g] = alpha * l_ref[g] + pv[V_DIM:V_DIM + SUBLANES]

    def query_tile(sub):
        qi = step * tiles + sub
        m_ref[...] = jnp.full(m_ref.shape, NEG_BIG, F32)
        l_ref[...] = jnp.zeros(l_ref.shape, F32)
        acc_ref[...] = jnp.zeros(acc_ref.shape, F32)

        def block_step(j_next, slot_next, j, slot):
            for g in range(MLA_HEADS):
                scores(j_next, g, slot_next, qi)
                softmax_pv(j, g, slot, False)

        def last_step(slot):
            nxt = jnp.minimum(qi + 1, last_tile)
            for g in range(MLA_HEADS):
                softmax_pv(qi, g, slot, True)
                first_scores(g, nxt)

        if sub == 0:
            @pl.when(step == 0)
            def _():
                for g in range(MLA_HEADS):
                    first_scores(g, 0)

        def pair(j):
            block_step(j + 1, 1, j, 0)
            block_step(j + 2, 0, j + 1, 1)

        def two_pairs(i, carry):
            pair(4 * i)
            pair(4 * i + 2)
            return carry

        if tiles == 2:
            lax.fori_loop(0, step // 2, two_pairs, 0)

            @pl.when(step % 2 == 1)
            def _():
                pair(2 * (step - 1))
        if sub == 0:
            last_step(0)
        else:
            block_step(qi, 1, qi - 1, 0)
            last_step(1)

        cols = slice(sub * tq, (sub + 1) * tq)
        for g in range(MLA_HEADS):
            inv_l = 1.0 / jnp.concatenate([l_ref[g]] * (V_DIM // SUBLANES), axis=0)
            o_ref[g * V_DIM:(g + 1) * V_DIM, cols] = acc_ref[g] * inv_l

    for sub in range(tiles):
        query_tile(sub)


def _attention(qt, k, vt, prefix=None):
    bsz, _, tp = qt.shape
    tq = tk = min(SEQ_TILE, tp)
    tiles = 1 if tp == tq else 2
    assert tp % (tiles * tq) == 0
    vdim = MLA_HEADS * V_DIM
    stat = pltpu.VMEM((MLA_HEADS, SUBLANES, tq), F32)
    has_prefix = prefix is not None
    prefix_specs = ([_const_spec((N_META, QK_WIDTH)), _const_spec((vdim, LANES))]
                    if has_prefix else [])
    return pl.pallas_call(
        functools.partial(_attn_kernel, has_prefix),
        grid=(bsz, tp // (tiles * tq)),
        in_specs=[
            pl.BlockSpec((None, QT_ROWS, tp), lambda b, i: (b, 0, 0)),
            pl.BlockSpec((None, tp, QK_WIDTH), lambda b, i: (b, 0, 0)),
            pl.BlockSpec((None, vdim, tp), lambda b, i: (b, 0, 0)),
        ] + prefix_specs,
        out_specs=pl.BlockSpec((None, vdim, tiles * tq), lambda b, i: (b, 0, i)),
        out_shape=jax.ShapeDtypeStruct((bsz, vdim, tp), F32),
        scratch_shapes=[stat, stat, pltpu.VMEM((MLA_HEADS, V_DIM, tq), F32),
                        pltpu.VMEM((MLA_HEADS, tk, tq), F32),
                        pltpu.VMEM((MLA_HEADS, tk, tq), F32), stat, stat]
        + ([pltpu.VMEM((MLA_HEADS, N_META, tq), F32)] if has_prefix else []),
        compiler_params=pltpu.CompilerParams(
            dimension_semantics=("arbitrary", "arbitrary"), vmem_limit_bytes=VMEM_LIMIT),
        name="attention",
    )(qt, k, vt, *(prefix or ()))


def _rope_tables(tp):
    pos = jnp.arange(tp, dtype=F32)
    inv_freq = 1.0 / (ROPE_THETA ** (jnp.arange(0, QK_ROPE, 2, dtype=F32) / QK_ROPE))
    ang = pos[:, None] * inv_freq[None, :]
    cos, sin = jnp.cos(ang), jnp.sin(ang)
    half = QK_ROPE // 2
    zeros = lambda n: jnp.zeros((tp, n), F32)
    pad = HEAD_PAD - QK_NOPE - QK_ROPE
    c = jnp.concatenate([jnp.ones((tp, QK_NOPE), F32), cos, cos, zeros(pad)], axis=1)
    sa = jnp.concatenate([zeros(QK_NOPE + half), sin, zeros(pad)], axis=1)
    sb = jnp.concatenate([zeros(QK_NOPE), -sin, zeros(half + pad)], axis=1)
    return c, sa, sb, jnp.tile(cos, (1, MLA_HEADS)), jnp.tile(sin, (1, MLA_HEADS))


def _block_diag(w):
    depth = w.shape[0]
    per = LRU_HEADS // 2
    w5 = w.reshape(depth, 2, per, LRU_HEAD_DIM, LRU_HEAD_DIM)
    bd = jnp.einsum("dnhij,hg->dnhigj", w5, jnp.eye(per, dtype=w.dtype))
    return bd.reshape(depth, 2, per * LRU_HEAD_DIM, per * LRU_HEAD_DIM)


def _row(g):
    return g.reshape(g.shape[0], 1, g.shape[1])


@jax.jit
def kernel(x, meta_tokens, ffn1_pre_g, ffn1_w_gate, ffn1_w_up, ffn1_w_down, ffn1_post_g, mix_pre_g, w_in, lru_conv_w, lru_conv_b, lru_w_a, lru_b_a, lru_w_x, lru_b_x, lru_lambda, mla_q_norm_g, mla_w_uq, mla_kv_norm_g, mla_w_ukv, lru_out_g, mla_out_g, w_out, mix_post_g, ffn2_pre_g, ffn2_w_gate, ffn2_w_up, ffn2_w_down, ffn2_post_g):
    bsz, seq, _ = x.shape
    depth = w_in.shape[0]
    assert seq % SEQ_TILE == 0 and seq % MIX_TILE == 0
    h = x
    hm = jnp.concatenate([meta_tokens.astype(x.dtype),
                          jnp.zeros((META_ROWS - N_META, D_MODEL), x.dtype)], axis=0)[None]
    rope_all = _rope_tables(N_META + max(seq, META_ROWS))
    rope_meta = tuple(t[:META_ROWS] for t in rope_all)
    rope_main = tuple(t[N_META:N_META + seq] for t in rope_all)
    zero_hist = jnp.zeros((SUBLANES, LRU_WIDTH), F32)
    zero_state = jnp.zeros((1, LRU_WIDTH), F32)

    split = 2 * LRU_WIDTH + Q_LORA + KV_LORA
    zcols = lambda n: jnp.zeros((depth, D_MODEL, n), w_in.dtype)
    w_in_p = jnp.concatenate(
        [w_in[..., :split], zcols(QK_NOPE), w_in[..., split:],
         zcols(HEAD_PAD - QK_NOPE - QK_ROPE)], axis=-1).astype(BF16)
    w_uq4 = mla_w_uq.reshape(depth, Q_LORA, MLA_HEADS, QK_NOPE + QK_ROPE)
    half = QK_ROPE // 2
    w_uq_p = jnp.concatenate(
        [w_uq4[..., :QK_NOPE].reshape(depth, Q_LORA, Q_NOPE_ALL),
         w_uq4[..., QK_NOPE:QK_NOPE + half].reshape(depth, Q_LORA, Q_HALF_ALL),
         w_uq4[..., QK_NOPE + half:].reshape(depth, Q_LORA, Q_HALF_ALL)], axis=-1)
    w_ukv4 = mla_w_ukv.reshape(depth, KV_LORA, MLA_HEADS, QK_NOPE + V_DIM)
    w_uk_p = jnp.pad(w_ukv4[..., :QK_NOPE], ((0, 0), (0, 0), (0, 0), (0, HEAD_PAD - QK_NOPE)))
    mixp = {
        "mix_pre_g": _row(mix_pre_g),
        "w_in": w_in_p,
        "conv_w": lru_conv_w,
        "conv_b": _row(lru_conv_b),
        "wax": jnp.concatenate([_block_diag(lru_w_a), _block_diag(lru_w_x)], axis=-1).astype(BF16),
        "b_a": _row(lru_b_a),
        "b_x": _row(lru_b_x),
        "lam": _row(lru_lambda),
        "q_g": _row(mla_q_norm_g),
        "w_uq": w_uq_p.astype(BF16),
        "kv_g": _row(mla_kv_norm_g),
        "w_uk": w_uk_p.reshape(depth, KV_LORA, QK_WIDTH).astype(BF16),
        "w_uv": w_ukv4[..., QK_NOPE:].reshape(depth, KV_LORA, MLA_HEADS * V_DIM).astype(BF16),
        "lru_out_g": _row(lru_out_g),
        "mla_out_g": mla_out_g.reshape(depth, MLA_HEADS * V_DIM, 1),
        "w_out": w_out.astype(BF16),
        "mix_post_g": _row(mix_post_g),
    }
    ffn1 = (_row(ffn1_pre_g), ffn1_w_gate.astype(BF16), ffn1_w_up.astype(BF16),
            ffn1_w_down.astype(BF16), _row(ffn1_post_g))
    ffn2 = (_row(ffn2_pre_g), ffn2_w_gate.astype(BF16), ffn2_w_up.astype(BF16),
            ffn2_w_down.astype(BF16), _row(ffn2_post_g))

    lane = lax.broadcasted_iota(jnp.int32, (MLA_HEADS * V_DIM, META_ROWS), 1)
    for layer in range(depth):
        hm = _ffn(hm, layer, *ffn1)
        h = _ffn(h, layer, *ffn1)
        ylru_m, qt_m, k_m, vt_m, xr_m, hs_m = _mix_in(
            hm, layer, mixp, rope_meta, zero_hist, zero_state, emit_state=True)
        ylru, qt, k, vt = _mix_in(h, layer, mixp, rope_main,
                                  xr_m[0, N_META - SUBLANES:N_META], hs_m[0, N_META - 1:N_META])
        prefix = (k_m[0, :N_META], jnp.where(lane < N_META, vt_m[0], 0).astype(BF16))
        h = _ffn(h, layer, *ffn2, mixer=(ylru, _attention(qt, k, vt, prefix), mixp))
        if layer + 1 < depth:
            hm = _ffn(hm, layer, *ffn2, mixer=(ylru_m, _attention(qt_m, k_m, vt_m), mixp))
    return h
```

```python
import functools

import jax
import jax.numpy as jnp
from jax import lax
from jax.experimental import pallas as pl
from jax.experimental.pallas import tpu as pltpu

D_MODEL = 1024
N_META = 16
LRU_WIDTH = 512
LRU_HEADS = 8
LRU_HEAD_DIM = 64
CONV_WIDTH = 4
LRU_C = 8.0
MLA_HEADS = 8
QK_NOPE = 64
QK_ROPE = 32
V_DIM = 64
Q_LORA = 384
KV_LORA = 256
D_FF = 2816
ROPE_THETA = 10000.0
EPS = 1e-6

LANES = 128
SUBLANES = 8
HEAD_PAD = LANES
QK_WIDTH = MLA_HEADS * HEAD_PAD
Q_NOPE_ALL = MLA_HEADS * QK_NOPE
Q_HALF_ALL = MLA_HEADS * QK_ROPE // 2
QT_ROWS = Q_NOPE_ALL + 2 * Q_HALF_ALL
IN_COLS_PAD = 2 * LRU_WIDTH + Q_LORA + KV_LORA + HEAD_PAD
SEQ_TILE = 256
MIX_TILE = 1024
MIX_SUB = 256
META_ROWS = LANES
FFN_ROWS = 512
FFN_SUBTILES = 2
FF_CHUNK = 256
NEG_BIG = -1e30
TINY = 1e-30
LOG2_E = 1.4426950408889634
VMEM_LIMIT = 52 * 1024 * 1024

F32 = jnp.float32
BF16 = jnp.bfloat16


def _rms(x, g):
    return x * lax.rsqrt(jnp.mean(x * x, axis=-1, keepdims=True) + EPS) * g


def _const_spec(shape, layer=None):
    if layer is None:
        idx = lambda *_: (0,) * len(shape)
        return pl.BlockSpec(shape, idx, pipeline_mode=pl.Buffered(1))
    idx = lambda *_: (layer,) + (0,) * len(shape)
    return pl.BlockSpec((None,) + shape, idx, pipeline_mode=pl.Buffered(1))


def _mix_out_rows(h, ylru, ot, gmla, wout_ref, gpost):
    ms = jnp.mean(ot * ot, axis=0, keepdims=True)
    on = (ot * lax.rsqrt(ms + EPS) * gmla).T.astype(BF16)
    y = jnp.dot(ylru, wout_ref[:LRU_WIDTH, :], preferred_element_type=F32)
    y = y + jnp.dot(on, wout_ref[LRU_WIDTH:, :], preferred_element_type=F32)
    return h + _rms(y, gpost)


def _ffn_kernel(after_mixer, h_ref, *refs):
    if after_mixer:
        ylru_ref, ot_ref, gmla_ref, wout_ref, gmix_ref = refs[:5]
        refs = refs[5:]
    gpre_ref, wg_ref, wu_ref, wd_ref, gpost_ref, o_ref, a_ref = refs
    rows = h_ref.shape[0]
    sub = rows // FFN_SUBTILES if after_mixer and rows % (FFN_SUBTILES * LANES) == 0 else rows
    parts = [slice(i * sub, (i + 1) * sub) for i in range(rows // sub)]
    xs = []
    for r in parts:
        x = h_ref[r, :]
        if after_mixer:
            x = _mix_out_rows(x, ylru_ref[r, :], ot_ref[:, r], gmla_ref[...], wout_ref,
                              gmix_ref[...])
        xs.append(x)
    for r, x in zip(parts, xs):
        u = _rms(x, gpre_ref[...]).astype(BF16)
        for c in range(D_FF // FF_CHUNK):
            sl = slice(c * FF_CHUNK, (c + 1) * FF_CHUNK)
            gate = jnp.dot(u, wg_ref[:, sl], preferred_element_type=F32)
            up = jnp.dot(u, wu_ref[:, sl], preferred_element_type=F32)
            a_ref[r, sl] = (gate * jax.nn.sigmoid(gate) * up).astype(BF16)
    for r, x in zip(parts, xs):
        f = jnp.dot(a_ref[r, :], wd_ref[...], preferred_element_type=F32)
        o_ref[r, :] = x + 0.5 * _rms(f, gpost_ref[...])


def _ffn(h, layer, gpre, wg, wu, wd, gpost, mixer=None):
    bsz, tp, _ = h.shape
    tile = min(FFN_ROWS if mixer is not None else 2 * FFN_ROWS, tp)
    assert tp % tile == 0
    row_spec = pl.BlockSpec((None, tile, D_MODEL), lambda b, t: (b, t, 0))
    mixer_specs, mixer_args = [], []
    if mixer is not None:
        ylru, ot, p = mixer
        vdim = MLA_HEADS * V_DIM
        mixer_specs = [
            pl.BlockSpec((None, tile, LRU_WIDTH), lambda b, t: (b, t, 0)),
            pl.BlockSpec((None, vdim, tile), lambda b, t: (b, 0, t)),
            _const_spec((vdim, 1), layer),
            _const_spec((D_MODEL, D_MODEL), layer),
            _const_spec((1, D_MODEL), layer),
        ]
        mixer_args = [ylru, ot, p["mla_out_g"], p["w_out"], p["mix_post_g"]]
    return pl.pallas_call(
        functools.partial(_ffn_kernel, mixer is not None),
        grid=(bsz, tp // tile),
        in_specs=[row_spec] + mixer_specs + [
            _const_spec((1, D_MODEL), layer),
            _const_spec((D_MODEL, D_FF), layer),
            _const_spec((D_MODEL, D_FF), layer),
            _const_spec((D_FF, D_MODEL), layer),
            _const_spec((1, D_MODEL), layer),
        ],
        out_specs=row_spec,
        out_shape=jax.ShapeDtypeStruct(h.shape, F32),
        scratch_shapes=[pltpu.VMEM((tile, D_FF), BF16)],
        compiler_params=pltpu.CompilerParams(
            dimension_semantics=("arbitrary", "arbitrary"), vmem_limit_bytes=VMEM_LIMIT),
        name="ffn_mix" if mixer is not None else "ffn",
    )(h, *mixer_args, gpre, wg, wu, wd, gpost)


def _rope(x, c, sa, sb):
    n = x.shape[-1]
    half = QK_ROPE // 2
    return x * c + pltpu.roll(x, half, 1) * sa + pltpu.roll(x, n - half, 1) * sb


def _mix_in_kernel(emit_state, h_ref, g_ref, win_ref, cw_ref, cb_ref, wax_ref, ba_ref, bx_ref,
                   lam_ref, gq_ref, wuq_ref, gkv_ref, wuk_ref, wuv_ref, glru_ref,
                   rc_ref, rsa_ref, rsb_ref, cos8_ref, sin8_ref, xprev0_ref, hcar0_ref,
                   ylru_ref, qt_ref, k_ref, vt_ref, *rest):
    if emit_state:
        xr_out_ref, hs_out_ref, xprev_ref, hcar_ref = rest
    else:
        xprev_ref, hcar_ref = rest
    tt = h_ref.shape[0]

    @pl.when(pl.program_id(1) == 0)
    def _():
        xprev_ref[...] = xprev0_ref[...]
        hcar_ref[...] = hcar0_ref[...]

    sub = min(MIX_SUB, tt)
    half = LRU_WIDTH // 2
    nl = -lam_ref[...]
    softplus = jnp.maximum(nl, 0.0) + jnp.log1p(jnp.exp(-jnp.abs(nl)))
    decay = (-LRU_C * LOG2_E) * softplus
    gq_scaled = gq_ref[...] * (LOG2_E * (QK_NOPE + QK_ROPE) ** -0.5)
    tile_heads = lambda t: jnp.concatenate([t] * MLA_HEADS, axis=1)

    def project(i):
        u = _rms(h_ref[i * sub:(i + 1) * sub, :], g_ref[...]).astype(BF16)
        return jnp.dot(u, win_ref[...], preferred_element_type=F32)

    xprev = xprev_ref[...]
    carry = hcar_ref[...]
    z_next = project(0)
    for i in range(tt // sub):
        z = z_next
        if (i + 1) * sub < tt:
            z_next = project(i + 1)
        rows = slice(i * sub, (i + 1) * sub)
        xr = z[:, 0:LRU_WIDTH]
        gr = z[:, LRU_WIDTH:2 * LRU_WIDTH]
        cq = z[:, 2 * LRU_WIDTH:2 * LRU_WIDTH + Q_LORA]
        ckv = z[:, 2 * LRU_WIDTH + Q_LORA:2 * LRU_WIDTH + Q_LORA + KV_LORA]
        krt = z[:, 2 * LRU_WIDTH + Q_LORA + KV_LORA:]

        xe = jnp.concatenate([xprev, xr], axis=0)
        xc = xr * cw_ref[CONV_WIDTH - 1:CONV_WIDTH, :] + cb_ref[...]
        for j in range(1, CONV_WIDTH):
            w_j = cw_ref[CONV_WIDTH - 1 - j:CONV_WIDTH - j, :]
            xc = xc + pltpu.roll(xe, j, 0)[SUBLANES:] * w_j
        xprev = xr[sub - SUBLANES:]

        xcb = xc.astype(BF16)
        ri0 = jnp.dot(xcb[:, :half], wax_ref[0], preferred_element_type=F32)
        ri1 = jnp.dot(xcb[:, half:], wax_ref[1], preferred_element_type=F32)
        r = jax.nn.sigmoid(jnp.concatenate([ri0[:, :half], ri1[:, :half]], axis=1) + ba_ref[...])
        g_in = jax.nn.sigmoid(
            jnp.concatenate([ri0[:, half:], ri1[:, half:]], axis=1) + bx_ref[...])
        a = jnp.exp2(r * decay)
        gap = 1.0 - a * a
        b = (gap * lax.rsqrt(jnp.maximum(gap, TINY))) * (g_in * xc)

        grouped = (sub // SUBLANES, SUBLANES, LRU_WIDTH)
        a = a.reshape(grouped)
        b = b.reshape(grouped)
        row = lax.broadcasted_iota(jnp.int32, grouped, 1)
        s = 1
        while s < SUBLANES:
            valid = row >= s
            b = a * jnp.where(valid, pltpu.roll(b, s, 1), 0.0) + b
            a = a * jnp.where(valid, pltpu.roll(a, s, 1), 1.0)
            s *= 2
        a = a.reshape(sub, LRU_WIDTH)
        b = b.reshape(sub, LRU_WIDTH)
        groups = []
        for gidx in range(sub // SUBLANES):
            sl = slice(gidx * SUBLANES, (gidx + 1) * SUBLANES)
            hg = b[sl] + a[sl] * carry
            carry = hg[SUBLANES - 1:SUBLANES]
            groups.append(hg)
        hs = jnp.concatenate(groups, axis=0)
        if emit_state:
            xr_out_ref[rows, :] = xr
            hs_out_ref[rows, :] = hs

        cg = 0.7978845608028654
        half_gr = 0.5 * gr
        gelu = half_gr + half_gr * jnp.tanh(gr * (cg + (cg * 0.044715) * (gr * gr)))
        ylru_ref[rows, :] = _rms(hs * gelu, glru_ref[...]).astype(BF16)

        rc = rc_ref[rows, :]
        rsa = rsa_ref[rows, :]
        rsb = rsb_ref[rows, :]
        cqn = _rms(cq, gq_scaled).astype(BF16)
        q = jnp.dot(cqn, wuq_ref[...], preferred_element_type=F32)
        q1 = q[:, Q_NOPE_ALL:Q_NOPE_ALL + Q_HALF_ALL]
        q2 = q[:, Q_NOPE_ALL + Q_HALF_ALL:]
        cos8 = cos8_ref[rows, :]
        sin8 = sin8_ref[rows, :]
        q = jnp.concatenate([q[:, :Q_NOPE_ALL], q1 * cos8 - q2 * sin8, q2 * cos8 + q1 * sin8],
                            axis=1)
        qt_ref[:, rows] = q.T.astype(BF16)

        ckvn = _rms(ckv, gkv_ref[...]).astype(BF16)
        kn = jnp.dot(ckvn, wuk_ref[...], preferred_element_type=F32)
        kr = _rope(krt, rc, rsa, rsb)
        k_ref[rows, :] = (kn + tile_heads(kr)).astype(BF16)
        v = jnp.dot(ckvn, wuv_ref[...], preferred_element_type=F32)
        vt_ref[:, rows] = v.T.astype(BF16)
    xprev_ref[...] = xprev
    hcar_ref[...] = carry


def _mix_in(h, layer, p, rope, xprev0, hcar0, emit_state=False):
    bsz, tp, _ = h.shape
    tt = min(MIX_TILE, tp)
    nt = tp // tt
    tbl_spec = pl.BlockSpec((tt, HEAD_PAD), lambda b, t: (t, 0))
    seq_spec = pl.BlockSpec((None, tt, LRU_WIDTH), lambda b, t: (b, t, 0))
    state_specs = [seq_spec, seq_spec] if emit_state else []
    state_shapes = [jax.ShapeDtypeStruct((bsz, tp, LRU_WIDTH), F32)] * 2 if emit_state else []
    return pl.pallas_call(
        functools.partial(_mix_in_kernel, emit_state),
        grid=(bsz, nt),
        in_specs=[
            pl.BlockSpec((None, tt, D_MODEL), lambda b, t: (b, t, 0)),
            _const_spec((1, D_MODEL), layer),
            _const_spec((D_MODEL, IN_COLS_PAD), layer),
            _const_spec((CONV_WIDTH, LRU_WIDTH), layer),
            _const_spec((1, LRU_WIDTH), layer),
            _const_spec((2, LRU_WIDTH // 2, LRU_WIDTH), layer),
            _const_spec((1, LRU_WIDTH), layer),
            _const_spec((1, LRU_WIDTH), layer),
            _const_spec((1, LRU_WIDTH), layer),
            _const_spec((1, Q_LORA), layer),
            _const_spec((Q_LORA, QT_ROWS), layer),
            _const_spec((1, KV_LORA), layer),
            _const_spec((KV_LORA, QK_WIDTH), layer),
            _const_spec((KV_LORA, MLA_HEADS * V_DIM), layer),
            _const_spec((1, LRU_WIDTH), layer),
            tbl_spec, tbl_spec, tbl_spec, tbl_spec, tbl_spec,
            _const_spec((SUBLANES, LRU_WIDTH)),
            _const_spec((1, LRU_WIDTH)),
        ],
        out_specs=[
            seq_spec,
            pl.BlockSpec((None, QT_ROWS, tt), lambda b, t: (b, 0, t)),
            pl.BlockSpec((None, tt, QK_WIDTH), lambda b, t: (b, t, 0)),
            pl.BlockSpec((None, MLA_HEADS * V_DIM, tt), lambda b, t: (b, 0, t)),
        ] + state_specs,
        out_shape=[
            jax.ShapeDtypeStruct((bsz, tp, LRU_WIDTH), BF16),
            jax.ShapeDtypeStruct((bsz, QT_ROWS, tp), BF16),
            jax.ShapeDtypeStruct((bsz, tp, QK_WIDTH), BF16),
            jax.ShapeDtypeStruct((bsz, MLA_HEADS * V_DIM, tp), BF16),
        ] + state_shapes,
        scratch_shapes=[pltpu.VMEM((SUBLANES, LRU_WIDTH), F32),
                        pltpu.VMEM((1, LRU_WIDTH), F32)],
        compiler_params=pltpu.CompilerParams(
            dimension_semantics=("arbitrary", "arbitrary"), vmem_limit_bytes=VMEM_LIMIT),
        name="mix_in",
    )(h, p["mix_pre_g"], p["w_in"], p["conv_w"], p["conv_b"], p["wax"], p["b_a"], p["b_x"],
      p["lam"], p["q_g"], p["w_uq"], p["kv_g"], p["w_uk"], p["w_uv"], p["lru_out_g"], *rope,
      xprev0, hcar0)


def _group_max(s):
    m = s[0:SUBLANES]
    for r in range(1, s.shape[0] // SUBLANES):
        m = jnp.maximum(m, s[r * SUBLANES:(r + 1) * SUBLANES])
    return m


def _sublane_allmax(m):
    for shift in (1, 2, 4):
        m = jnp.maximum(m, pltpu.roll(m, shift, 0))
    return m


def _attn_kernel(has_prefix, max_pairs, qt_ref, k_ref, vt_ref, *rest):
    if has_prefix:
        kpre_ref, vtpre_ref = rest[:2]
        rest = rest[2:]
        spre_ref = rest[-1]
        rest = rest[:-1]
    o_ref, m_ref, l_ref, acc_ref, s0_ref, s1_ref, bm0_ref, bm1_ref = rest
    step = pl.program_id(1)
    tk, tq = s0_ref.shape[1:]
    tiles = o_ref.shape[1] // tq
    assert tiles in (1, 2)
    last_tile = pl.num_programs(1) * tiles - 1
    s_refs = (s0_ref, s1_ref)
    bm_refs = (bm0_ref, bm1_ref)
    ones_rows = 2 * SUBLANES

    def with_ones(vb):
        return jnp.concatenate([vb, jnp.ones((ones_rows, vb.shape[1]), BF16)], axis=0)

    def q_head(g, tile):
        half = QK_ROPE // 2
        r1 = Q_NOPE_ALL + g * half
        r2 = r1 + Q_HALF_ALL
        cols = pl.ds(pl.multiple_of(tile * tq, tq), tq)
        return jnp.concatenate(
            [qt_ref[g * QK_NOPE:(g + 1) * QK_NOPE, cols], qt_ref[r1:r1 + half, cols],
             qt_ref[r2:r2 + half, cols], jnp.zeros((HEAD_PAD - QK_NOPE - QK_ROPE, tq), BF16)],
            axis=0)

    def scores(j, g, slot, tile):
        off = pl.multiple_of(j * tk, tk)
        s = jnp.dot(k_ref[pl.ds(off, tk), g * HEAD_PAD:(g + 1) * HEAD_PAD], q_head(g, tile),
                    preferred_element_type=F32)
        s_refs[slot][g] = s
        bm_refs[slot][g] = _group_max(s)

    def first_scores(g, tile):
        scores(0, g, 0, tile)
        if has_prefix:
            spre_ref[g] = jnp.dot(kpre_ref[:, g * HEAD_PAD:(g + 1) * HEAD_PAD], q_head(g, tile),
                                  preferred_element_type=F32)

    def softmax_pv(j, g, slot, masked):
        s = s_refs[slot][g]
        if masked:
            key = lax.broadcasted_iota(jnp.int32, s.shape, 0)
            qry = lax.broadcasted_iota(jnp.int32, s.shape, 1)
            s = jnp.where(key <= qry, s, NEG_BIG)
            bm = _group_max(s)
            if has_prefix:
                s_pre = spre_ref[g]
                bm = jnp.maximum(bm, _group_max(s_pre))
        else:
            bm = bm_refs[slot][g]
        m_old = m_ref[g]
        m_new = jnp.maximum(m_old, _sublane_allmax(bm))
        alpha = jnp.exp2(m_old - m_new)
        p = jnp.exp2(s - jnp.concatenate([m_new] * (tk // SUBLANES), axis=0))
        m_ref[g] = m_new
        off = pl.multiple_of(j * tk, tk)
        vb = with_ones(vt_ref[g * V_DIM:(g + 1) * V_DIM, pl.ds(off, tk)])
        pv = jnp.dot(vb, p.astype(BF16), preferred_element_type=F32)
        if masked and has_prefix:
            p_pre = jnp.exp2(s_pre - jnp.concatenate([m_new] * (N_META // SUBLANES), axis=0))
            p_pre = jnp.concatenate(
                [p_pre.astype(BF16), jnp.zeros((LANES - N_META, tq), BF16)], axis=0)
            pv = pv + jnp.dot(with_ones(vtpre_ref[g * V_DIM:(g + 1) * V_DIM, :]), p_pre,
                              preferred_element_type=F32)
        acc_ref[g] = (jnp.concatenate([alpha] * (V_DIM // SUBLANES), axis=0) * acc_ref[g]
                      + pv[:V_DIM])
        l_ref[g] = alpha * l_ref[g] + pv[V_DIM:V_DIM + SUBLANES]

    def query_tile(sub):
        qi = step * tiles + sub
        m_ref[...] = jnp.full(m_ref.shape, NEG_BIG, F32)
        l_ref[...] = jnp.zeros(l_ref.shape, F32)
        acc_ref[...] = jnp.zeros(acc_ref.shape, F32)

        def block_step(j_next, slot_next, j, slot):
            for g in range(MLA_HEADS):
                scores(j_next, g, slot_next, qi)
                softmax_pv(j, g, slot, False)

        def last_step(slot):
            nxt = jnp.minimum(qi + 1, last_tile)
            for g in range(MLA_HEADS):
                softmax_pv(qi, g, slot, True)
                first_scores(g, nxt)

        if sub == 0:
            @pl.when(step == 0)
            def _():
                for g in range(MLA_HEADS):
                    first_scores(g, 0)

        def pair(j):
            block_step(j + 1, 1, j, 0)
            block_step(j + 2, 0, j + 1, 1)

        bit = 1
        while bit * 2 <= max_pairs:
            bit *= 2
        while tiles == 2 and bit >= 1:
            @pl.when((step & bit) != 0)
            def _(bit=bit):
                first = step & ~(2 * bit - 1)
                for q in range(bit):
                    pair(2 * (first + q))
            bit //= 2
        if sub == 0:
            last_step(0)
        else:
            block_step(qi, 1, qi - 1, 0)
            last_step(1)

        cols = slice(sub * tq, (sub + 1) * tq)
        for g in range(MLA_HEADS):
            inv_l = 1.0 / jnp.concatenate([l_ref[g]] * (V_DIM // SUBLANES), axis=0)
            o_ref[g * V_DIM:(g + 1) * V_DIM, cols] = acc_ref[g] * inv_l

    for sub in range(tiles):
        query_tile(sub)


def _attention(qt, k, vt, prefix=None):
    bsz, _, tp = qt.shape
    tq = tk = min(SEQ_TILE, tp)
    tiles = 1 if tp == tq else 2
    assert tp % (tiles * tq) == 0
    vdim = MLA_HEADS * V_DIM
    stat = pltpu.VMEM((MLA_HEADS, SUBLANES, tq), F32)
    has_prefix = prefix is not None
    prefix_specs = ([_const_spec((N_META, QK_WIDTH)), _const_spec((vdim, LANES))]
                    if has_prefix else [])
    return pl.pallas_call(
        functools.partial(_attn_kernel, has_prefix, max(tp // (tiles * tq) - 1, 1)),
        grid=(bsz, tp // (tiles * tq)),
        in_specs=[
            pl.BlockSpec((None, QT_ROWS, tp), lambda b, i: (b, 0, 0)),
            pl.BlockSpec((None, tp, QK_WIDTH), lambda b, i: (b, 0, 0)),
            pl.BlockSpec((None, vdim, tp), lambda b, i: (b, 0, 0)),
        ] + prefix_specs,
        out_specs=pl.BlockSpec((None, vdim, tiles * tq), lambda b, i: (b, 0, i)),
        out_shape=jax.ShapeDtypeStruct((bsz, vdim, tp), F32),
        scratch_shapes=[stat, stat, pltpu.VMEM((MLA_HEADS, V_DIM, tq), F32),
                        pltpu.VMEM((MLA_HEADS, tk, tq), F32),
                        pltpu.VMEM((MLA_HEADS, tk, tq), F32), stat, stat]
        + ([pltpu.VMEM((MLA_HEADS, N_META, tq), F32)] if has_prefix else []),
        compiler_params=pltpu.CompilerParams(
            dimension_semantics=("arbitrary", "arbitrary"), vmem_limit_bytes=VMEM_LIMIT),
        name="attention",
    )(qt, k, vt, *(prefix or ()))


def _rope_tables(tp):
    pos = jnp.arange(tp, dtype=F32)
    inv_freq = 1.0 / (ROPE_THETA ** (jnp.arange(0, QK_ROPE, 2, dtype=F32) / QK_ROPE))
    ang = pos[:, None] * inv_freq[None, :]
    cos, sin = jnp.cos(ang), jnp.sin(ang)
    half = QK_ROPE // 2
    zeros = lambda n: jnp.zeros((tp, n), F32)
    pad = HEAD_PAD - QK_NOPE - QK_ROPE
    c = jnp.concatenate([jnp.ones((tp, QK_NOPE), F32), cos, cos, zeros(pad)], axis=1)
    sa = jnp.concatenate([zeros(QK_NOPE + half), sin, zeros(pad)], axis=1)
    sb = jnp.concatenate([zeros(QK_NOPE), -sin, zeros(half + pad)], axis=1)
    return c, sa, sb, jnp.tile(cos, (1, MLA_HEADS)), jnp.tile(sin, (1, MLA_HEADS))


def _block_diag(w):
    depth = w.shape[0]
    per = LRU_HEADS // 2
    w5 = w.reshape(depth, 2, per, LRU_HEAD_DIM, LRU_HEAD_DIM)
    bd = jnp.einsum("dnhij,hg->dnhigj", w5, jnp.eye(per, dtype=w.dtype))
    return bd.reshape(depth, 2, per * LRU_HEAD_DIM, per * LRU_HEAD_DIM)


def _row(g):
    return g.reshape(g.shape[0], 1, g.shape[1])


@jax.jit
def kernel(x, meta_tokens, ffn1_pre_g, ffn1_w_gate, ffn1_w_up, ffn1_w_down, ffn1_post_g, mix_pre_g, w_in, lru_conv_w, lru_conv_b, lru_w_a, lru_b_a, lru_w_x, lru_b_x, lru_lambda, mla_q_norm_g, mla_w_uq, mla_kv_norm_g, mla_w_ukv, lru_out_g, mla_out_g, w_out, mix_post_g, ffn2_pre_g, ffn2_w_gate, ffn2_w_up, ffn2_w_down, ffn2_post_g):
    bsz, seq, _ = x.shape
    depth = w_in.shape[0]
    assert seq % SEQ_TILE == 0 and seq % MIX_TILE == 0
    h = x
    hm = jnp.concatenate([meta_tokens.astype(x.dtype),
                          jnp.zeros((META_ROWS - N_META, D_MODEL), x.dtype)], axis=0)[None]
    rope_all = _rope_tables(N_META + max(seq, META_ROWS))
    rope_meta = tuple(t[:META_ROWS] for t in rope_all)
    rope_main = tuple(t[N_META:N_META + seq] for t in rope_all)
    zero_hist = jnp.zeros((SUBLANES, LRU_WIDTH), F32)
    zero_state = jnp.zeros((1, LRU_WIDTH), F32)

    split = 2 * LRU_WIDTH + Q_LORA + KV_LORA
    zcols = lambda n: jnp.zeros((depth, D_MODEL, n), w_in.dtype)
    w_in_p = jnp.concatenate(
        [w_in[..., :split], zcols(QK_NOPE), w_in[..., split:],
         zcols(HEAD_PAD - QK_NOPE - QK_ROPE)], axis=-1).astype(BF16)
    w_uq4 = mla_w_uq.reshape(depth, Q_LORA, MLA_HEADS, QK_NOPE + QK_ROPE)
    half = QK_ROPE // 2
    w_uq_p = jnp.concatenate(
        [w_uq4[..., :QK_NOPE].reshape(depth, Q_LORA, Q_NOPE_ALL),
         w_uq4[..., QK_NOPE:QK_NOPE + half].reshape(depth, Q_LORA, Q_HALF_ALL),
         w_uq4[..., QK_NOPE + half:].reshape(depth, Q_LORA, Q_HALF_ALL)], axis=-1)
    w_ukv4 = mla_w_ukv.reshape(depth, KV_LORA, MLA_HEADS, QK_NOPE + V_DIM)
    w_uk_p = jnp.pad(w_ukv4[..., :QK_NOPE], ((0, 0), (0, 0), (0, 0), (0, HEAD_PAD - QK_NOPE)))
    mixp = {
        "mix_pre_g": _row(mix_pre_g),
        "w_in": w_in_p,
        "conv_w": lru_conv_w,
        "conv_b": _row(lru_conv_b),
        "wax": jnp.concatenate([_block_diag(lru_w_a), _block_diag(lru_w_x)], axis=-1).astype(BF16),
        "b_a": _row(lru_b_a),
        "b_x": _row(lru_b_x),
        "lam": _row(lru_lambda),
        "q_g": _row(mla_q_norm_g),
        "w_uq": w_uq_p.astype(BF16),
        "kv_g": _row(mla_kv_norm_g),
        "w_uk": w_uk_p.reshape(depth, KV_LORA, QK_WIDTH).astype(BF16),
        "w_uv": w_ukv4[..., QK_NOPE:].reshape(depth, KV_LORA, MLA_HEADS * V_DIM).astype(BF16),
        "lru_out_g": _row(lru_out_g),
        "mla_out_g": mla_out_g.reshape(depth, MLA_HEADS * V_DIM, 1),
        "w_out": w_out.astype(BF16),
        "mix_post_g": _row(mix_post_g),
    }
    ffn1 = (_row(ffn1_pre_g), ffn1_w_gate.astype(BF16), ffn1_w_up.astype(BF16),
            ffn1_w_down.astype(BF16), _row(ffn1_post_g))
    ffn2 = (_row(ffn2_pre_g), ffn2_w_gate.astype(BF16), ffn2_w_up.astype(BF16),
            ffn2_w_down.astype(BF16), _row(ffn2_post_g))

    lane = lax.broadcasted_iota(jnp.int32, (MLA_HEADS * V_DIM, META_ROWS), 1)
    for layer in range(depth):
        hm = _ffn(hm, layer, *ffn1)
        h = _ffn(h, layer, *ffn1)
        ylru_m, qt_m, k_m, vt_m, xr_m, hs_m = _mix_in(
            hm, layer, mixp, rope_meta, zero_hist, zero_state, emit_state=True)
        ylru, qt, k, vt = _mix_in(h, layer, mixp, rope_main,
                                  xr_m[0, N_META - SUBLANES:N_META], hs_m[0, N_META - 1:N_META])
        prefix = (k_m[0, :N_META], jnp.where(lane < N_META, vt_m[0], 0).astype(BF16))
        h = _ffn(h, layer, *ffn2, mixer=(ylru, _attention(qt, k, vt, prefix), mixp))
        if layer + 1 < depth:
            hm = _ffn(hm, layer, *ffn2, mixer=(ylru_m, _attention(qt_m, k_m, vt_m), mixp))
    return h
```

```python
import functools

import jax
import jax.numpy as jnp
from jax import lax
from jax.experimental import pallas as pl
from jax.experimental.pallas import tpu as pltpu

D_MODEL = 1024
N_META = 16
LRU_WIDTH = 512
LRU_HEADS = 8
LRU_HEAD_DIM = 64
CONV_WIDTH = 4
LRU_C = 8.0
MLA_HEADS = 8
QK_NOPE = 64
QK_ROPE = 32
V_DIM = 64
Q_LORA = 384
KV_LORA = 256
D_FF = 2816
ROPE_THETA = 10000.0
EPS = 1e-6

LANES = 128
SUBLANES = 8
HEAD_PAD = LANES
QK_WIDTH = MLA_HEADS * HEAD_PAD
Q_NOPE_ALL = MLA_HEADS * QK_NOPE
Q_HALF_ALL = MLA_HEADS * QK_ROPE // 2
QT_ROWS = Q_NOPE_ALL + 2 * Q_HALF_ALL
IN_COLS_PAD = 2 * LRU_WIDTH + Q_LORA + KV_LORA + HEAD_PAD
SEQ_TILE = 256
MIX_TILE = 1024
MIX_SUB = 256
META_ROWS = LANES
FFN_ROWS = 512
FFN_SUBTILES = 2
FF_CHUNK = 256
NEG_BIG = -1e30
TINY = 1e-30
LOG2_E = 1.4426950408889634
VMEM_LIMIT = 52 * 1024 * 1024

F32 = jnp.float32
BF16 = jnp.bfloat16


def _rms(x, g):
    return x * lax.rsqrt(jnp.mean(x * x, axis=-1, keepdims=True) + EPS) * g


def _const_spec(shape, layer=None):
    if layer is None:
        idx = lambda *_: (0,) * len(shape)
        return pl.BlockSpec(shape, idx, pipeline_mode=pl.Buffered(1))
    idx = lambda *_: (layer,) + (0,) * len(shape)
    return pl.BlockSpec((None,) + shape, idx, pipeline_mode=pl.Buffered(1))


def _mix_out_rows(h, ylru, ot, gmla, wout_ref, gpost):
    ms = jnp.mean(ot * ot, axis=0, keepdims=True)
    on = (ot * lax.rsqrt(ms + EPS) * gmla).T.astype(BF16)
    y = jnp.dot(ylru, wout_ref[:LRU_WIDTH, :], preferred_element_type=F32)
    y = y + jnp.dot(on, wout_ref[LRU_WIDTH:, :], preferred_element_type=F32)
    return h + _rms(y, gpost)


def _ffn_kernel(after_mixer, h_ref, *refs):
    if after_mixer:
        ylru_ref, ot_ref, gmla_ref, wout_ref, gmix_ref = refs[:5]
        refs = refs[5:]
    gpre_ref, wg_ref, wu_ref, wd_ref, gpost_ref, o_ref, a_ref = refs
    rows = h_ref.shape[0]
    sub = rows // FFN_SUBTILES if after_mixer and rows % (FFN_SUBTILES * LANES) == 0 else rows
    parts = [slice(i * sub, (i + 1) * sub) for i in range(rows // sub)]
    xs = []
    for r in parts:
        x = h_ref[r, :]
        if after_mixer:
            x = _mix_out_rows(x, ylru_ref[r, :], ot_ref[:, r], gmla_ref[...], wout_ref,
                              gmix_ref[...])
        xs.append(x)
    for r, x in zip(parts, xs):
        u = _rms(x, gpre_ref[...]).astype(BF16)
        for c in range(D_FF // FF_CHUNK):
            sl = slice(c * FF_CHUNK, (c + 1) * FF_CHUNK)
            gate = jnp.dot(u, wg_ref[:, sl], preferred_element_type=F32)
            up = jnp.dot(u, wu_ref[:, sl], preferred_element_type=F32)
            a_ref[r, sl] = (gate * jax.nn.sigmoid(gate) * up).astype(BF16)
    for r, x in zip(parts, xs):
        f = jnp.dot(a_ref[r, :], wd_ref[...], preferred_element_type=F32)
        o_ref[r, :] = x + 0.5 * _rms(f, gpost_ref[...])


def _ffn(h, layer, gpre, wg, wu, wd, gpost, mixer=None):
    bsz, tp, _ = h.shape
    tile = min(FFN_ROWS if mixer is not None else 2 * FFN_ROWS, tp)
    assert tp % tile == 0
    row_spec = pl.BlockSpec((None, tile, D_MODEL), lambda b, t: (b, t, 0))
    mixer_specs, mixer_args = [], []
    if mixer is not None:
        ylru, ot, p = mixer
        vdim = MLA_HEADS * V_DIM
        mixer_specs = [
            pl.BlockSpec((None, tile, LRU_WIDTH), lambda b, t: (b, t, 0)),
            pl.BlockSpec((None, vdim, tile), lambda b, t: (b, 0, t)),
            _const_spec((vdim, 1), layer),
            _const_spec((D_MODEL, D_MODEL), layer),
            _const_spec((1, D_MODEL), layer),
        ]
        mixer_args = [ylru, ot, p["mla_out_g"], p["w_out"], p["mix_post_g"]]
    return pl.pallas_call(
        functools.partial(_ffn_kernel, mixer is not None),
        grid=(bsz, tp // tile),
        in_specs=[row_spec] + mixer_specs + [
            _const_spec((1, D_MODEL), layer),
            _const_spec((D_MODEL, D_FF), layer),
            _const_spec((D_MODEL, D_FF), layer),
            _const_spec((D_FF, D_MODEL), layer),
            _const_spec((1, D_MODEL), layer),
        ],
        out_specs=row_spec,
        out_shape=jax.ShapeDtypeStruct(h.shape, F32),
        scratch_shapes=[pltpu.VMEM((tile, D_FF), BF16)],
        compiler_params=pltpu.CompilerParams(
            dimension_semantics=("arbitrary", "arbitrary"), vmem_limit_bytes=VMEM_LIMIT),
        name="ffn_mix" if mixer is not None else "ffn",
    )(h, *mixer_args, gpre, wg, wu, wd, gpost)


def _rope(x, c, sa, sb):
    n = x.shape[-1]
    half = QK_ROPE // 2
    return x * c + pltpu.roll(x, half, 1) * sa + pltpu.roll(x, n - half, 1) * sb


def _mix_in_kernel(emit_state, h_ref, g_ref, win_ref, cw_ref, cb_ref, wax_ref, ba_ref, bx_ref,
                   lam_ref, gq_ref, wuq_ref, gkv_ref, wuk_ref, wuv_ref, glru_ref,
                   rc_ref, rsa_ref, rsb_ref, cos8_ref, sin8_ref, xprev0_ref, hcar0_ref,
                   ylru_ref, qt_ref, k_ref, vt_ref, *rest):
    if emit_state:
        xr_out_ref, hs_out_ref, xprev_ref, hcar_ref = rest
    else:
        xprev_ref, hcar_ref = rest
    tt = h_ref.shape[0]

    @pl.when(pl.program_id(1) == 0)
    def _():
        xprev_ref[...] = xprev0_ref[...]
        hcar_ref[...] = hcar0_ref[...]

    sub = min(MIX_SUB, tt)
    half = LRU_WIDTH // 2
    nl = -lam_ref[...]
    softplus = jnp.maximum(nl, 0.0) + jnp.log1p(jnp.exp(-jnp.abs(nl)))
    decay = (-LRU_C * LOG2_E) * softplus
    gq_scaled = gq_ref[...] * (LOG2_E * (QK_NOPE + QK_ROPE) ** -0.5)
    tile_heads = lambda t: jnp.concatenate([t] * MLA_HEADS, axis=1)

    def project(i):
        u = _rms(h_ref[i * sub:(i + 1) * sub, :], g_ref[...]).astype(BF16)
        return jnp.dot(u, win_ref[...], preferred_element_type=F32)

    xprev = xprev_ref[...]
    carry = hcar_ref[...]
    z_next = project(0)
    for i in range(tt // sub):
        z = z_next
        if (i + 1) * sub < tt:
            z_next = project(i + 1)
        rows = slice(i * sub, (i + 1) * sub)
        xr = z[:, 0:LRU_WIDTH]
        gr = z[:, LRU_WIDTH:2 * LRU_WIDTH]
        cq = z[:, 2 * LRU_WIDTH:2 * LRU_WIDTH + Q_LORA]
        ckv = z[:, 2 * LRU_WIDTH + Q_LORA:2 * LRU_WIDTH + Q_LORA + KV_LORA]
        krt = z[:, 2 * LRU_WIDTH + Q_LORA + KV_LORA:]

        xe = jnp.concatenate([xprev, xr], axis=0)
        xc = xr * cw_ref[CONV_WIDTH - 1:CONV_WIDTH, :] + cb_ref[...]
        for j in range(1, CONV_WIDTH):
            w_j = cw_ref[CONV_WIDTH - 1 - j:CONV_WIDTH - j, :]
            xc = xc + pltpu.roll(xe, j, 0)[SUBLANES:] * w_j
        xprev = xr[sub - SUBLANES:]

        xcb = xc.astype(BF16)
        ri0 = jnp.dot(xcb[:, :half], wax_ref[0], preferred_element_type=F32)
        ri1 = jnp.dot(xcb[:, half:], wax_ref[1], preferred_element_type=F32)
        r = jax.nn.sigmoid(jnp.concatenate([ri0[:, :half], ri1[:, :half]], axis=1) + ba_ref[...])
        g_in = jax.nn.sigmoid(
            jnp.concatenate([ri0[:, half:], ri1[:, half:]], axis=1) + bx_ref[...])
        a = jnp.exp2(r * decay)
        gap = 1.0 - a * a
        b = (gap * lax.rsqrt(jnp.maximum(gap, TINY))) * (g_in * xc)

        grouped = (sub // SUBLANES, SUBLANES, LRU_WIDTH)
        a = a.reshape(grouped)
        b = b.reshape(grouped)
        row = lax.broadcasted_iota(jnp.int32, grouped, 1)
        s = 1
        while s < SUBLANES:
            valid = row >= s
            b = a * jnp.where(valid, pltpu.roll(b, s, 1), 0.0) + b
            a = a * jnp.where(valid, pltpu.roll(a, s, 1), 1.0)
            s *= 2
        a = a.reshape(sub, LRU_WIDTH)
        b = b.reshape(sub, LRU_WIDTH)
        groups = []
        for gidx in range(sub // SUBLANES):
            sl = slice(gidx * SUBLANES, (gidx + 1) * SUBLANES)
            hg = b[sl] + a[sl] * carry
            carry = hg[SUBLANES - 1:SUBLANES]
            groups.append(hg)
        hs = jnp.concatenate(groups, axis=0)
        if emit_state:
            xr_out_ref[rows, :] = xr
            hs_out_ref[rows, :] = hs

        cg = 0.7978845608028654
        half_gr = 0.5 * gr
        gelu = half_gr + half_gr * jnp.tanh(gr * (cg + (cg * 0.044715) * (gr * gr)))
        ylru_ref[rows, :] = _rms(hs * gelu, glru_ref[...]).astype(BF16)

        rc = rc_ref[rows, :]
        rsa = rsa_ref[rows, :]
        rsb = rsb_ref[rows, :]
        cqn = _rms(cq, gq_scaled).astype(BF16)
        q = jnp.dot(cqn, wuq_ref[...], preferred_element_type=F32)
        q1 = q[:, Q_NOPE_ALL:Q_NOPE_ALL + Q_HALF_ALL]
        q2 = q[:, Q_NOPE_ALL + Q_HALF_ALL:]
        cos8 = cos8_ref[rows, :]
        sin8 = sin8_ref[rows, :]
        q = jnp.concatenate([q[:, :Q_NOPE_ALL], q1 * cos8 - q2 * sin8, q2 * cos8 + q1 * sin8],
                            axis=1)
        qt_ref[:, rows] = q.T.astype(BF16)

        ckvn = _rms(ckv, gkv_ref[...]).astype(BF16)
        kn = jnp.dot(ckvn, wuk_ref[...], preferred_element_type=F32)
        kr = _rope(krt, rc, rsa, rsb)
        k_ref[rows, :] = (kn + tile_heads(kr)).astype(BF16)
        v = jnp.dot(ckvn, wuv_ref[...], preferred_element_type=F32)
        vt_ref[:, rows] = v.T.astype(BF16)
    xprev_ref[...] = xprev
    hcar_ref[...] = carry


def _mix_in(h, layer, p, rope, xprev0, hcar0, emit_state=False):
    bsz, tp, _ = h.shape
    tt = min(MIX_TILE, tp)
    nt = tp // tt
    tbl_spec = pl.BlockSpec((tt, HEAD_PAD), lambda b, t: (t, 0))
    seq_spec = pl.BlockSpec((None, tt, LRU_WIDTH), lambda b, t: (b, t, 0))
    state_specs = [seq_spec, seq_spec] if emit_state else []
    state_shapes = [jax.ShapeDtypeStruct((bsz, tp, LRU_WIDTH), F32)] * 2 if emit_state else []
    return pl.pallas_call(
        functools.partial(_mix_in_kernel, emit_state),
        grid=(bsz, nt),
        in_specs=[
            pl.BlockSpec((None, tt, D_MODEL), lambda b, t: (b, t, 0)),
            _const_spec((1, D_MODEL), layer),
            _const_spec((D_MODEL, IN_COLS_PAD), layer),
            _const_spec((CONV_WIDTH, LRU_WIDTH), layer),
            _const_spec((1, LRU_WIDTH), layer),
            _const_spec((2, LRU_WIDTH // 2, LRU_WIDTH), layer),
            _const_spec((1, LRU_WIDTH), layer),
            _const_spec((1, LRU_WIDTH), layer),
            _const_spec((1, LRU_WIDTH), layer),
            _const_spec((1, Q_LORA), layer),
            _const_spec((Q_LORA, QT_ROWS), layer),
            _const_spec((1, KV_LORA), layer),
            _const_spec((KV_LORA, QK_WIDTH), layer),
            _const_spec((KV_LORA, MLA_HEADS * V_DIM), layer),
            _const_spec((1, LRU_WIDTH), layer),
            tbl_spec, tbl_spec, tbl_spec, tbl_spec, tbl_spec,
            _const_spec((SUBLANES, LRU_WIDTH)),
            _const_spec((1, LRU_WIDTH)),
        ],
        out_specs=[
            seq_spec,
            pl.BlockSpec((None, QT_ROWS, tt), lambda b, t: (b, 0, t)),
            pl.BlockSpec((None, tt, QK_WIDTH), lambda b, t: (b, t, 0)),
            pl.BlockSpec((None, MLA_HEADS * V_DIM, tt), lambda b, t: (b, 0, t)),
        ] + state_specs,
        out_shape=[
            jax.ShapeDtypeStruct((bsz, tp, LRU_WIDTH), BF16),
            jax.ShapeDtypeStruct((bsz, QT_ROWS, tp), BF16),
            jax.ShapeDtypeStruct((bsz, tp, QK_WIDTH), BF16),
            jax.ShapeDtypeStruct((bsz, MLA_HEADS * V_DIM, tp), BF16),
        ] + state_shapes,
        scratch_shapes=[pltpu.VMEM((SUBLANES, LRU_WIDTH), F32),
                        pltpu.VMEM((1, LRU_WIDTH), F32)],
        compiler_params=pltpu.CompilerParams(
            dimension_semantics=("arbitrary", "arbitrary"), vmem_limit_bytes=VMEM_LIMIT),
        name="mix_in",
    )(h, p["mix_pre_g"], p["w_in"], p["conv_w"], p["conv_b"], p["wax"], p["b_a"], p["b_x"],
      p["lam"], p["q_g"], p["w_uq"], p["kv_g"], p["w_uk"], p["w_uv"], p["lru_out_g"], *rope,
      xprev0, hcar0)


def _group_max(s):
    m = s[0:SUBLANES]
    for r in range(1, s.shape[0] // SUBLANES):
        m = jnp.maximum(m, s[r * SUBLANES:(r + 1) * SUBLANES])
    return m


def _sublane_allmax(m):
    for shift in (1, 2, 4):
        m = jnp.maximum(m, pltpu.roll(m, shift, 0))
    return m


def _attn_kernel(has_prefix, max_pairs, qt_ref, k_ref, vt_ref, *rest):
    if has_prefix:
        kpre_ref, vtpre_ref = rest[:2]
        rest = rest[2:]
        spre_ref = rest[-1]
        rest = rest[:-1]
    o_ref, m_ref, l_ref, acc_ref, s0_ref, s1_ref, bm0_ref, bm1_ref = rest
    step = pl.program_id(1)
    tk, tq = s0_ref.shape[1:]
    tiles = o_ref.shape[1] // tq
    assert tiles in (1, 2)
    last_tile = pl.num_programs(1) * tiles - 1
    s_refs = (s0_ref, s1_ref)
    bm_refs = (bm0_ref, bm1_ref)
    ones_rows = 2 * SUBLANES

    def with_ones(vb):
        return jnp.concatenate([vb, jnp.ones((ones_rows, vb.shape[1]), BF16)], axis=0)

    def q_head(g, tile):
        half = QK_ROPE // 2
        r1 = Q_NOPE_ALL + g * half
        r2 = r1 + Q_HALF_ALL
        cols = pl.ds(pl.multiple_of(tile * tq, tq), tq)
        return jnp.concatenate(
            [qt_ref[g * QK_NOPE:(g + 1) * QK_NOPE, cols], qt_ref[r1:r1 + half, cols],
             qt_ref[r2:r2 + half, cols], jnp.zeros((HEAD_PAD - QK_NOPE - QK_ROPE, tq), BF16)],
            axis=0)

    def scores(j, g, slot, tile):
        off = pl.multiple_of(j * tk, tk)
        s = jnp.dot(k_ref[pl.ds(off, tk), g * HEAD_PAD:(g + 1) * HEAD_PAD], q_head(g, tile),
                    preferred_element_type=F32)
        s_refs[slot][g] = s
        bm_refs[slot][g] = _group_max(s)

    def first_scores(g, tile):
        scores(0, g, 0, tile)
        if has_prefix:
            spre_ref[g] = jnp.dot(kpre_ref[:, g * HEAD_PAD:(g + 1) * HEAD_PAD], q_head(g, tile),
                                  preferred_element_type=F32)

    def softmax_pv(j, g, slot, masked):
        s = s_refs[slot][g]
        if masked:
            key = lax.broadcasted_iota(jnp.int32, s.shape, 0)
            qry = lax.broadcasted_iota(jnp.int32, s.shape, 1)
            s = jnp.where(key <= qry, s, NEG_BIG)
            bm = _group_max(s)
            if has_prefix:
                s_pre = spre_ref[g]
                bm = jnp.maximum(bm, _group_max(s_pre))
        else:
            bm = bm_refs[slot][g]
        m_old = m_ref[g]
        m_new = jnp.maximum(m_old, _sublane_allmax(bm))
        alpha = jnp.exp2(m_old - m_new)
        p = jnp.exp2(s - jnp.concatenate([m_new] * (tk // SUBLANES), axis=0))
        m_ref[g] = m_new
        off = pl.multiple_of(j * tk, tk)
        vb = with_ones(vt_ref[g * V_DIM:(g + 1) * V_DIM, pl.ds(off, tk)])
        pv = jnp.dot(vb, p.astype(BF16), preferred_element_type=F32)
        if masked and has_prefix:
            p_pre = jnp.exp2(s_pre - jnp.concatenate([m_new] * (N_META // SUBLANES), axis=0))
            p_pre = jnp.concatenate(
                [p_pre.astype(BF16), jnp.zeros((LANES - N_META, tq), BF16)], axis=0)
            pv = pv + jnp.dot(with_ones(vtpre_ref[g * V_DIM:(g + 1) * V_DIM, :]), p_pre,
                              preferred_element_type=F32)
        acc_ref[g] = (jnp.concatenate([alpha] * (V_DIM // SUBLANES), axis=0) * acc_ref[g]
                      + pv[:V_DIM])
        l_ref[g] = alpha * l_ref[g] + pv[V_DIM:V_DIM + SUBLANES]

    def query_tile(sub):
        qi = step * tiles + sub
        m_ref[...] = jnp.full(m_ref.shape, NEG_BIG, F32)
        l_ref[...] = jnp.zeros(l_ref.shape, F32)
        acc_ref[...] = jnp.zeros(acc_ref.shape, F32)

        def block_step(j_next, slot_next, j, slot):
            for g in range(MLA_HEADS):
                scores(j_next, g, slot_next, qi)
                softmax_pv(j, g, slot, False)

        def last_step(slot):
            nxt = jnp.minimum(qi + 1, last_tile)
            for g in range(MLA_HEADS):
                softmax_pv(qi, g, slot, True)
                first_scores(g, nxt)

        if sub == 0:
            @pl.when(step == 0)
            def _():
                for g in range(MLA_HEADS):
                    first_scores(g, 0)

        def pair(j):
            block_step(j + 1, 1, j, 0)
            block_step(j + 2, 0, j + 1, 1)

        bit = 1
        while bit * 2 <= max_pairs:
            bit *= 2
        while tiles == 2 and bit >= 2:
            @pl.when((step & bit) != 0)
            def _(bit=bit):
                first = step & ~(2 * bit - 1)
                for q in range(bit):
                    pair(2 * (first + q))
            bit //= 2

        def finish(odd_pair):
            if odd_pair:
                pair(2 * (step - 1))
            if sub == 0:
                last_step(0)
            else:
                block_step(qi, 1, qi - 1, 0)
                last_step(1)
            cols = slice(sub * tq, (sub + 1) * tq)
            for g in range(MLA_HEADS):
                inv_l = 1.0 / jnp.concatenate([l_ref[g]] * (V_DIM // SUBLANES), axis=0)
                o_ref[g * V_DIM:(g + 1) * V_DIM, cols] = acc_ref[g] * inv_l

        if tiles == 2:
            pl.when((step & 1) != 0)(lambda: finish(True))
            pl.when((step & 1) == 0)(lambda: finish(False))
        else:
            finish(False)

    for sub in range(tiles):
        query_tile(sub)


def _attention(qt, k, vt, prefix=None):
    bsz, _, tp = qt.shape
    tq = tk = min(SEQ_TILE, tp)
    tiles = 1 if tp == tq else 2
    assert tp % (tiles * tq) == 0
    vdim = MLA_HEADS * V_DIM
    stat = pltpu.VMEM((MLA_HEADS, SUBLANES, tq), F32)
    has_prefix = prefix is not None
    prefix_specs = ([_const_spec((N_META, QK_WIDTH)), _const_spec((vdim, LANES))]
                    if has_prefix else [])
    return pl.pallas_call(
        functools.partial(_attn_kernel, has_prefix, max(tp // (tiles * tq) - 1, 1)),
        grid=(bsz, tp // (tiles * tq)),
        in_specs=[
            pl.BlockSpec((None, QT_ROWS, tp), lambda b, i: (b, 0, 0)),
            pl.BlockSpec((None, tp, QK_WIDTH), lambda b, i: (b, 0, 0)),
            pl.BlockSpec((None, vdim, tp), lambda b, i: (b, 0, 0)),
        ] + prefix_specs,
        out_specs=pl.BlockSpec((None, vdim, tiles * tq), lambda b, i: (b, 0, i)),
        out_shape=jax.ShapeDtypeStruct((bsz, vdim, tp), F32),
        scratch_shapes=[stat, stat, pltpu.VMEM((MLA_HEADS, V_DIM, tq), F32),
                        pltpu.VMEM((MLA_HEADS, tk, tq), F32),
                        pltpu.VMEM((MLA_HEADS, tk, tq), F32), stat, stat]
        + ([pltpu.VMEM((MLA_HEADS, N_META, tq), F32)] if has_prefix else []),
        compiler_params=pltpu.CompilerParams(
            dimension_semantics=("arbitrary", "arbitrary"), vmem_limit_bytes=VMEM_LIMIT),
        name="attention",
    )(qt, k, vt, *(prefix or ()))


def _rope_tables(tp):
    pos = jnp.arange(tp, dtype=F32)
    inv_freq = 1.0 / (ROPE_THETA ** (jnp.arange(0, QK_ROPE, 2, dtype=F32) / QK_ROPE))
    ang = pos[:, None] * inv_freq[None, :]
    cos, sin = jnp.cos(ang), jnp.sin(ang)
    half = QK_ROPE // 2
    zeros = lambda n: jnp.zeros((tp, n), F32)
    pad = HEAD_PAD - QK_NOPE - QK_ROPE
    c = jnp.concatenate([jnp.ones((tp, QK_NOPE), F32), cos, cos, zeros(pad)], axis=1)
    sa = jnp.concatenate([zeros(QK_NOPE + half), sin, zeros(pad)], axis=1)
    sb = jnp.concatenate([zeros(QK_NOPE), -sin, zeros(half + pad)], axis=1)
    return c, sa, sb, jnp.tile(cos, (1, MLA_HEADS)), jnp.tile(sin, (1, MLA_HEADS))


def _block_diag(w):
    depth = w.shape[0]
    per = LRU_HEADS // 2
    w5 = w.reshape(depth, 2, per, LRU_HEAD_DIM, LRU_HEAD_DIM)
    bd = jnp.einsum("dnhij,hg->dnhigj", w5, jnp.eye(per, dtype=w.dtype))
    return bd.reshape(depth, 2, per * LRU_HEAD_DIM, per * LRU_HEAD_DIM)


def _row(g):
    return g.reshape(g.shape[0], 1, g.shape[1])


@jax.jit
def kernel(x, meta_tokens, ffn1_pre_g, ffn1_w_gate, ffn1_w_up, ffn1_w_down, ffn1_post_g, mix_pre_g, w_in, lru_conv_w, lru_conv_b, lru_w_a, lru_b_a, lru_w_x, lru_b_x, lru_lambda, mla_q_norm_g, mla_w_uq, mla_kv_norm_g, mla_w_ukv, lru_out_g, mla_out_g, w_out, mix_post_g, ffn2_pre_g, ffn2_w_gate, ffn2_w_up, ffn2_w_down, ffn2_post_g):
    bsz, seq, _ = x.shape
    depth = w_in.shape[0]
    assert seq % SEQ_TILE == 0 and seq % MIX_TILE == 0
    h = x
    hm = jnp.concatenate([meta_tokens.astype(x.dtype),
                          jnp.zeros((META_ROWS - N_META, D_MODEL), x.dtype)], axis=0)[None]
    rope_all = _rope_tables(N_META + max(seq, META_ROWS))
    rope_meta = tuple(t[:META_ROWS] for t in rope_all)
    rope_main = tuple(t[N_META:N_META + seq] for t in rope_all)
    zero_hist = jnp.zeros((SUBLANES, LRU_WIDTH), F32)
    zero_state = jnp.zeros((1, LRU_WIDTH), F32)

    split = 2 * LRU_WIDTH + Q_LORA + KV_LORA
    zcols = lambda n: jnp.zeros((depth, D_MODEL, n), w_in.dtype)
    w_in_p = jnp.concatenate(
        [w_in[..., :split], zcols(QK_NOPE), w_in[..., split:],
         zcols(HEAD_PAD - QK_NOPE - QK_ROPE)], axis=-1).astype(BF16)
    w_uq4 = mla_w_uq.reshape(depth, Q_LORA, MLA_HEADS, QK_NOPE + QK_ROPE)
    half = QK_ROPE // 2
    w_uq_p = jnp.concatenate(
        [w_uq4[..., :QK_NOPE].reshape(depth, Q_LORA, Q_NOPE_ALL),
         w_uq4[..., QK_NOPE:QK_NOPE + half].reshape(depth, Q_LORA, Q_HALF_ALL),
         w_uq4[..., QK_NOPE + half:].reshape(depth, Q_LORA, Q_HALF_ALL)], axis=-1)
    w_ukv4 = mla_w_ukv.reshape(depth, KV_LORA, MLA_HEADS, QK_NOPE + V_DIM)
    w_uk_p = jnp.pad(w_ukv4[..., :QK_NOPE], ((0, 0), (0, 0), (0, 0), (0, HEAD_PAD - QK_NOPE)))
    mixp = {
        "mix_pre_g": _row(mix_pre_g),
        "w_in": w_in_p,
        "conv_w": lru_conv_w,
        "conv_b": _row(lru_conv_b),
        "wax": jnp.concatenate([_block_diag(lru_w_a), _block_diag(lru_w_x)], axis=-1).astype(BF16),
        "b_a": _row(lru_b_a),
        "b_x": _row(lru_b_x),
        "lam": _row(lru_lambda),
        "q_g": _row(mla_q_norm_g),
        "w_uq": w_uq_p.astype(BF16),
        "kv_g": _row(mla_kv_norm_g),
        "w_uk": w_uk_p.reshape(depth, KV_LORA, QK_WIDTH).astype(BF16),
        "w_uv": w_ukv4[..., QK_NOPE:].reshape(depth, KV_LORA, MLA_HEADS * V_DIM).astype(BF16),
        "lru_out_g": _row(lru_out_g),
        "mla_out_g": mla_out_g.reshape(depth, MLA_HEADS * V_DIM, 1),
        "w_out": w_out.astype(BF16),
        "mix_post_g": _row(mix_post_g),
    }
    ffn1 = (_row(ffn1_pre_g), ffn1_w_gate.astype(BF16), ffn1_w_up.astype(BF16),
            ffn1_w_down.astype(BF16), _row(ffn1_post_g))
    ffn2 = (_row(ffn2_pre_g), ffn2_w_gate.astype(BF16), ffn2_w_up.astype(BF16),
            ffn2_w_down.astype(BF16), _row(ffn2_post_g))

    lane = lax.broadcasted_iota(jnp.int32, (MLA_HEADS * V_DIM, META_ROWS), 1)
    for layer in range(depth):
        hm = _ffn(hm, layer, *ffn1)
        h = _ffn(h, layer, *ffn1)
        ylru_m, qt_m, k_m, vt_m, xr_m, hs_m = _mix_in(
            hm, layer, mixp, rope_meta, zero_hist, zero_state, emit_state=True)
        ylru, qt, k, vt = _mix_in(h, layer, mixp, rope_main,
                                  xr_m[0, N_META - SUBLANES:N_META], hs_m[0, N_META - 1:N_META])
        prefix = (k_m[0, :N_META], jnp.where(lane < N_META, vt_m[0], 0).astype(BF16))
        h = _ffn(h, layer, *ffn2, mixer=(ylru, _attention(qt, k, vt, prefix), mixp))
        if layer + 1 < depth:
            hm = _ffn(hm, layer, *ffn2, mixer=(ylru_m, _attention(qt_m, k_m, vt_m), mixp))
    return h
```

```python
import functools

import jax
import jax.numpy as jnp
from jax import lax
from jax.experimental import pallas as pl
from jax.experimental.pallas import tpu as pltpu

D_MODEL = 1024
N_META = 16
LRU_WIDTH = 512
LRU_HEADS = 8
LRU_HEAD_DIM = 64
CONV_WIDTH = 4
LRU_C = 8.0
MLA_HEADS = 8
QK_NOPE = 64
QK_ROPE = 32
V_DIM = 64
Q_LORA = 384
KV_LORA = 256
D_FF = 2816
ROPE_THETA = 10000.0
EPS = 1e-6

LANES = 128
SUBLANES = 8
HEAD_PAD = LANES
QK_WIDTH = MLA_HEADS * HEAD_PAD
Q_NOPE_ALL = MLA_HEADS * QK_NOPE
Q_HALF_ALL = MLA_HEADS * QK_ROPE // 2
QT_ROWS = Q_NOPE_ALL + 2 * Q_HALF_ALL
IN_COLS_PAD = 2 * LRU_WIDTH + Q_LORA + KV_LORA + HEAD_PAD
SEQ_TILE = 256
MIX_TILE = 1024
MIX_SUB = 256
META_ROWS = LANES
FFN_ROWS = 512
FFN_SUBTILES = 2
FF_CHUNK = 256
NEG_BIG = -1e30
TINY = 1e-30
LOG2_E = 1.4426950408889634
VMEM_LIMIT = 52 * 1024 * 1024

F32 = jnp.float32
BF16 = jnp.bfloat16


def _rms(x, g):
    return x * lax.rsqrt(jnp.mean(x * x, axis=-1, keepdims=True) + EPS) * g


def _const_spec(shape, layer=None):
    if layer is None:
        idx = lambda *_: (0,) * len(shape)
        return pl.BlockSpec(shape, idx, pipeline_mode=pl.Buffered(1))
    idx = lambda *_: (layer,) + (0,) * len(shape)
    return pl.BlockSpec((None,) + shape, idx, pipeline_mode=pl.Buffered(1))


def _mix_out_rows(h, ylru, ot, gmla, wout_ref, gpost):
    ot = ot.astype(F32)
    ms = jnp.mean(ot * ot, axis=0, keepdims=True)
    on = (ot * lax.rsqrt(ms + EPS) * gmla).T.astype(BF16)
    y = jnp.dot(ylru, wout_ref[:LRU_WIDTH, :], preferred_element_type=F32)
    y = y + jnp.dot(on, wout_ref[LRU_WIDTH:, :], preferred_element_type=F32)
    return h + _rms(y, gpost)


def _ffn_kernel(after_mixer, h_ref, *refs):
    if after_mixer:
        ylru_ref, ot_ref, gmla_ref, wout_ref, gmix_ref = refs[:5]
        refs = refs[5:]
    gpre_ref, wg_ref, wu_ref, wd_ref, gpost_ref, o_ref, a_ref = refs
    rows = h_ref.shape[0]
    sub = rows // FFN_SUBTILES if after_mixer and rows % (FFN_SUBTILES * LANES) == 0 else rows
    parts = [slice(i * sub, (i + 1) * sub) for i in range(rows // sub)]
    xs = []
    for r in parts:
        x = h_ref[r, :]
        if after_mixer:
            x = _mix_out_rows(x, ylru_ref[r, :], ot_ref[:, r], gmla_ref[...], wout_ref,
                              gmix_ref[...])
        xs.append(x)
    for r, x in zip(parts, xs):
        u = _rms(x, gpre_ref[...]).astype(BF16)
        for c in range(D_FF // FF_CHUNK):
            sl = slice(c * FF_CHUNK, (c + 1) * FF_CHUNK)
            gate = jnp.dot(u, wg_ref[:, sl], preferred_element_type=F32)
            up = jnp.dot(u, wu_ref[:, sl], preferred_element_type=F32)
            a_ref[r, sl] = (gate * jax.nn.sigmoid(gate) * up).astype(BF16)
    for r, x in zip(parts, xs):
        f = jnp.dot(a_ref[r, :], wd_ref[...], preferred_element_type=F32)
        o_ref[r, :] = x + 0.5 * _rms(f, gpost_ref[...])


def _ffn(h, layer, gpre, wg, wu, wd, gpost, mixer=None):
    bsz, tp, _ = h.shape
    tile = min(FFN_ROWS if mixer is not None else 2 * FFN_ROWS, tp)
    assert tp % tile == 0
    row_spec = pl.BlockSpec((None, tile, D_MODEL), lambda b, t: (b, t, 0))
    mixer_specs, mixer_args = [], []
    if mixer is not None:
        ylru, ot, p = mixer
        vdim = MLA_HEADS * V_DIM
        mixer_specs = [
            pl.BlockSpec((None, tile, LRU_WIDTH), lambda b, t: (b, t, 0)),
            pl.BlockSpec((None, vdim, tile), lambda b, t: (b, 0, t)),
            _const_spec((vdim, 1), layer),
            _const_spec((D_MODEL, D_MODEL), layer),
            _const_spec((1, D_MODEL), layer),
        ]
        mixer_args = [ylru, ot, p["mla_out_g"], p["w_out"], p["mix_post_g"]]
    return pl.pallas_call(
        functools.partial(_ffn_kernel, mixer is not None),
        grid=(bsz, tp // tile),
        in_specs=[row_spec] + mixer_specs + [
            _const_spec((1, D_MODEL), layer),
            _const_spec((D_MODEL, D_FF), layer),
            _const_spec((D_MODEL, D_FF), layer),
            _const_spec((D_FF, D_MODEL), layer),
            _const_spec((1, D_MODEL), layer),
        ],
        out_specs=row_spec,
        out_shape=jax.ShapeDtypeStruct(h.shape, F32),
        scratch_shapes=[pltpu.VMEM((tile, D_FF), BF16)],
        compiler_params=pltpu.CompilerParams(
            dimension_semantics=("arbitrary", "arbitrary"), vmem_limit_bytes=VMEM_LIMIT),
        name="ffn_mix" if mixer is not None else "ffn",
    )(h, *mixer_args, gpre, wg, wu, wd, gpost)


def _rope(x, c, sa, sb):
    n = x.shape[-1]
    half = QK_ROPE // 2
    return x * c + pltpu.roll(x, half, 1) * sa + pltpu.roll(x, n - half, 1) * sb


def _mix_in_kernel(emit_state, h_ref, g_ref, win_ref, cw_ref, cb_ref, wax_ref, ba_ref, bx_ref,
                   lam_ref, gq_ref, wuq_ref, gkv_ref, wuk_ref, wuv_ref, glru_ref,
                   rc_ref, rsa_ref, rsb_ref, cos8_ref, sin8_ref, xprev0_ref, hcar0_ref,
                   ylru_ref, qt_ref, k_ref, vt_ref, *rest):
    if emit_state:
        xr_out_ref, hs_out_ref, xprev_ref, hcar_ref = rest
    else:
        xprev_ref, hcar_ref = rest
    tt = h_ref.shape[0]

    @pl.when(pl.program_id(1) == 0)
    def _():
        xprev_ref[...] = xprev0_ref[...]
        hcar_ref[...] = hcar0_ref[...]

    sub = min(MIX_SUB, tt)
    half = LRU_WIDTH // 2
    nl = -lam_ref[...]
    softplus = jnp.maximum(nl, 0.0) + jnp.log1p(jnp.exp(-jnp.abs(nl)))
    decay = (-LRU_C * LOG2_E) * softplus
    gq_scaled = gq_ref[...] * (LOG2_E * (QK_NOPE + QK_ROPE) ** -0.5)
    tile_heads = lambda t: jnp.concatenate([t] * MLA_HEADS, axis=1)

    def project(i):
        u = _rms(h_ref[i * sub:(i + 1) * sub, :], g_ref[...]).astype(BF16)
        return jnp.dot(u, win_ref[...], preferred_element_type=F32)

    xprev = xprev_ref[...]
    carry = hcar_ref[...]
    z_next = project(0)
    for i in range(tt // sub):
        z = z_next
        if (i + 1) * sub < tt:
            z_next = project(i + 1)
        rows = slice(i * sub, (i + 1) * sub)
        xr = z[:, 0:LRU_WIDTH]
        gr = z[:, LRU_WIDTH:2 * LRU_WIDTH]
        cq = z[:, 2 * LRU_WIDTH:2 * LRU_WIDTH + Q_LORA]
        ckv = z[:, 2 * LRU_WIDTH + Q_LORA:2 * LRU_WIDTH + Q_LORA + KV_LORA]
        krt = z[:, 2 * LRU_WIDTH + Q_LORA + KV_LORA:]

        xe = jnp.concatenate([xprev, xr], axis=0)
        xc = xr * cw_ref[CONV_WIDTH - 1:CONV_WIDTH, :] + cb_ref[...]
        for j in range(1, CONV_WIDTH):
            w_j = cw_ref[CONV_WIDTH - 1 - j:CONV_WIDTH - j, :]
            xc = xc + pltpu.roll(xe, j, 0)[SUBLANES:] * w_j
        xprev = xr[sub - SUBLANES:]

        xcb = xc.astype(BF16)
        ri0 = jnp.dot(xcb[:, :half], wax_ref[0], preferred_element_type=F32)
        ri1 = jnp.dot(xcb[:, half:], wax_ref[1], preferred_element_type=F32)
        r = jax.nn.sigmoid(jnp.concatenate([ri0[:, :half], ri1[:, :half]], axis=1) + ba_ref[...])
        g_in = jax.nn.sigmoid(
            jnp.concatenate([ri0[:, half:], ri1[:, half:]], axis=1) + bx_ref[...])
        a = jnp.exp2(r * decay)
        gap = 1.0 - a * a
        b = (gap * lax.rsqrt(jnp.maximum(gap, TINY))) * (g_in * xc)

        grouped = (sub // SUBLANES, SUBLANES, LRU_WIDTH)
        a = a.reshape(grouped)
        b = b.reshape(grouped)
        row = lax.broadcasted_iota(jnp.int32, grouped, 1)
        s = 1
        while s < SUBLANES:
            valid = row >= s
            b = a * jnp.where(valid, pltpu.roll(b, s, 1), 0.0) + b
            a = a * jnp.where(valid, pltpu.roll(a, s, 1), 1.0)
            s *= 2
        a = a.reshape(sub, LRU_WIDTH)
        b = b.reshape(sub, LRU_WIDTH)
        groups = []
        for gidx in range(sub // SUBLANES):
            sl = slice(gidx * SUBLANES, (gidx + 1) * SUBLANES)
            hg = b[sl] + a[sl] * carry
            carry = hg[SUBLANES - 1:SUBLANES]
            groups.append(hg)
        hs = jnp.concatenate(groups, axis=0)
        if emit_state:
            xr_out_ref[rows, :] = xr
            hs_out_ref[rows, :] = hs

        cg = 0.7978845608028654
        half_gr = 0.5 * gr
        gelu = half_gr + half_gr * jnp.tanh(gr * (cg + (cg * 0.044715) * (gr * gr)))
        ylru_ref[rows, :] = _rms(hs * gelu, glru_ref[...]).astype(BF16)

        rc = rc_ref[rows, :]
        rsa = rsa_ref[rows, :]
        rsb = rsb_ref[rows, :]
        cqn = _rms(cq, gq_scaled).astype(BF16)
        q = jnp.dot(cqn, wuq_ref[...], preferred_element_type=F32)
        q1 = q[:, Q_NOPE_ALL:Q_NOPE_ALL + Q_HALF_ALL]
        q2 = q[:, Q_NOPE_ALL + Q_HALF_ALL:]
        cos8 = cos8_ref[rows, :]
        sin8 = sin8_ref[rows, :]
        q = jnp.concatenate([q[:, :Q_NOPE_ALL], q1 * cos8 - q2 * sin8, q2 * cos8 + q1 * sin8],
                            axis=1)
        qt_ref[:, rows] = q.T.astype(BF16)

        ckvn = _rms(ckv, gkv_ref[...]).astype(BF16)
        kn = jnp.dot(ckvn, wuk_ref[...], preferred_element_type=F32)
        kr = _rope(krt, rc, rsa, rsb)
        k_ref[rows, :] = (kn + tile_heads(kr)).astype(BF16)
        v = jnp.dot(ckvn, wuv_ref[...], preferred_element_type=F32)
        vt_ref[:, rows] = v.T.astype(BF16)
    xprev_ref[...] = xprev
    hcar_ref[...] = carry


def _mix_in(h, layer, p, rope, xprev0, hcar0, emit_state=False):
    bsz, tp, _ = h.shape
    tt = min(MIX_TILE, tp)
    nt = tp // tt
    tbl_spec = pl.BlockSpec((tt, HEAD_PAD), lambda b, t: (t, 0))
    seq_spec = pl.BlockSpec((None, tt, LRU_WIDTH), lambda b, t: (b, t, 0))
    state_specs = [seq_spec, seq_spec] if emit_state else []
    state_shapes = [jax.ShapeDtypeStruct((bsz, tp, LRU_WIDTH), F32)] * 2 if emit_state else []
    return pl.pallas_call(
        functools.partial(_mix_in_kernel, emit_state),
        grid=(bsz, nt),
        in_specs=[
            pl.BlockSpec((None, tt, D_MODEL), lambda b, t: (b, t, 0)),
            _const_spec((1, D_MODEL), layer),
            _const_spec((D_MODEL, IN_COLS_PAD), layer),
            _const_spec((CONV_WIDTH, LRU_WIDTH), layer),
            _const_spec((1, LRU_WIDTH), layer),
            _const_spec((2, LRU_WIDTH // 2, LRU_WIDTH), layer),
            _const_spec((1, LRU_WIDTH), layer),
            _const_spec((1, LRU_WIDTH), layer),
            _const_spec((1, LRU_WIDTH), layer),
            _const_spec((1, Q_LORA), layer),
            _const_spec((Q_LORA, QT_ROWS), layer),
            _const_spec((1, KV_LORA), layer),
            _const_spec((KV_LORA, QK_WIDTH), layer),
            _const_spec((KV_LORA, MLA_HEADS * V_DIM), layer),
            _const_spec((1, LRU_WIDTH), layer),
            tbl_spec, tbl_spec, tbl_spec, tbl_spec, tbl_spec,
            _const_spec((SUBLANES, LRU_WIDTH)),
            _const_spec((1, LRU_WIDTH)),
        ],
        out_specs=[
            seq_spec,
            pl.BlockSpec((None, QT_ROWS, tt), lambda b, t: (b, 0, t)),
            pl.BlockSpec((None, tt, QK_WIDTH), lambda b, t: (b, t, 0)),
            pl.BlockSpec((None, MLA_HEADS * V_DIM, tt), lambda b, t: (b, 0, t)),
        ] + state_specs,
        out_shape=[
            jax.ShapeDtypeStruct((bsz, tp, LRU_WIDTH), BF16),
            jax.ShapeDtypeStruct((bsz, QT_ROWS, tp), BF16),
            jax.ShapeDtypeStruct((bsz, tp, QK_WIDTH), BF16),
            jax.ShapeDtypeStruct((bsz, MLA_HEADS * V_DIM, tp), BF16),
        ] + state_shapes,
        scratch_shapes=[pltpu.VMEM((SUBLANES, LRU_WIDTH), F32),
                        pltpu.VMEM((1, LRU_WIDTH), F32)],
        compiler_params=pltpu.CompilerParams(
            dimension_semantics=("arbitrary", "arbitrary"), vmem_limit_bytes=VMEM_LIMIT),
        name="mix_in",
    )(h, p["mix_pre_g"], p["w_in"], p["conv_w"], p["conv_b"], p["wax"], p["b_a"], p["b_x"],
      p["lam"], p["q_g"], p["w_uq"], p["kv_g"], p["w_uk"], p["w_uv"], p["lru_out_g"], *rope,
      xprev0, hcar0)


def _group_max(s):
    m = s[0:SUBLANES]
    for r in range(1, s.shape[0] // SUBLANES):
        m = jnp.maximum(m, s[r * SUBLANES:(r + 1) * SUBLANES])
    return m


def _sublane_allmax(m):
    for shift in (1, 2, 4):
        m = jnp.maximum(m, pltpu.roll(m, shift, 0))
    return m


def _attn_kernel(has_prefix, max_pairs, qt_ref, k_ref, vt_ref, *rest):
    if has_prefix:
        kpre_ref, vtpre_ref = rest[:2]
        rest = rest[2:]
        spre_ref = rest[-1]
        rest = rest[:-1]
    o_ref, m_ref, l_ref, acc_ref, s0_ref, s1_ref, bm0_ref, bm1_ref = rest
    step = pl.program_id(1)
    tk, tq = s0_ref.shape[1:]
    tiles = o_ref.shape[1] // tq
    assert tiles in (1, 2)
    last_tile = pl.num_programs(1) * tiles - 1
    s_refs = (s0_ref, s1_ref)
    bm_refs = (bm0_ref, bm1_ref)
    ones_rows = 2 * SUBLANES

    def with_ones(vb):
        return jnp.concatenate([vb, jnp.ones((ones_rows, vb.shape[1]), BF16)], axis=0)

    def q_head(g, tile):
        half = QK_ROPE // 2
        r1 = Q_NOPE_ALL + g * half
        r2 = r1 + Q_HALF_ALL
        cols = pl.ds(pl.multiple_of(tile * tq, tq), tq)
        return jnp.concatenate(
            [qt_ref[g * QK_NOPE:(g + 1) * QK_NOPE, cols], qt_ref[r1:r1 + half, cols],
             qt_ref[r2:r2 + half, cols], jnp.zeros((HEAD_PAD - QK_NOPE - QK_ROPE, tq), BF16)],
            axis=0)

    def scores(j, g, slot, tile):
        off = pl.multiple_of(j * tk, tk)
        s = jnp.dot(k_ref[pl.ds(off, tk), g * HEAD_PAD:(g + 1) * HEAD_PAD], q_head(g, tile),
                    preferred_element_type=F32)
        s_refs[slot][g] = s
        bm_refs[slot][g] = _group_max(s)

    def first_scores(g, tile):
        scores(0, g, 0, tile)
        if has_prefix:
            spre_ref[g] = jnp.dot(kpre_ref[:, g * HEAD_PAD:(g + 1) * HEAD_PAD], q_head(g, tile),
                                  preferred_element_type=F32)

    def softmax_pv(j, g, slot, masked):
        s = s_refs[slot][g]
        if masked:
            key = lax.broadcasted_iota(jnp.int32, s.shape, 0)
            qry = lax.broadcasted_iota(jnp.int32, s.shape, 1)
            s = jnp.where(key <= qry, s, NEG_BIG)
            bm = _group_max(s)
            if has_prefix:
                s_pre = spre_ref[g]
                bm = jnp.maximum(bm, _group_max(s_pre))
        else:
            bm = bm_refs[slot][g]
        m_old = m_ref[g]
        m_new = jnp.maximum(m_old, _sublane_allmax(bm))
        alpha = jnp.exp2(m_old - m_new)
        p = jnp.exp2(s - jnp.concatenate([m_new] * (tk // SUBLANES), axis=0))
        m_ref[g] = m_new
        off = pl.multiple_of(j * tk, tk)
        vb = with_ones(vt_ref[g * V_DIM:(g + 1) * V_DIM, pl.ds(off, tk)])
        pv = jnp.dot(vb, p.astype(BF16), preferred_element_type=F32)
        if masked and has_prefix:
            p_pre = jnp.exp2(s_pre - jnp.concatenate([m_new] * (N_META // SUBLANES), axis=0))
            p_pre = jnp.concatenate(
                [p_pre.astype(BF16), jnp.zeros((LANES - N_META, tq), BF16)], axis=0)
            pv = pv + jnp.dot(with_ones(vtpre_ref[g * V_DIM:(g + 1) * V_DIM, :]), p_pre,
                              preferred_element_type=F32)
        acc_ref[g] = (jnp.concatenate([alpha] * (V_DIM // SUBLANES), axis=0) * acc_ref[g]
                      + pv[:V_DIM])
        l_ref[g] = alpha * l_ref[g] + pv[V_DIM:V_DIM + SUBLANES]

    def query_tile(sub):
        qi = step * tiles + sub
        m_ref[...] = jnp.full(m_ref.shape, NEG_BIG, F32)
        l_ref[...] = jnp.zeros(l_ref.shape, F32)
        acc_ref[...] = jnp.zeros(acc_ref.shape, F32)

        def block_step(j_next, slot_next, j, slot):
            for g in range(MLA_HEADS):
                scores(j_next, g, slot_next, qi)
                softmax_pv(j, g, slot, False)

        def last_step(slot):
            nxt = jnp.minimum(qi + 1, last_tile)
            for g in range(MLA_HEADS):
                softmax_pv(qi, g, slot, True)
                first_scores(g, nxt)

        if sub == 0:
            @pl.when(step == 0)
            def _():
                for g in range(MLA_HEADS):
                    first_scores(g, 0)

        def pair(j):
            block_step(j + 1, 1, j, 0)
            block_step(j + 2, 0, j + 1, 1)

        bit = 1
        while bit * 2 <= max_pairs:
            bit *= 2
        while tiles == 2 and bit >= 2:
            @pl.when((step & bit) != 0)
            def _(bit=bit):
                first = step & ~(2 * bit - 1)
                for q in range(bit):
                    pair(2 * (first + q))
            bit //= 2

        def finish(odd_pair):
            if odd_pair:
                pair(2 * (step - 1))
            if sub == 0:
                last_step(0)
            else:
                block_step(qi, 1, qi - 1, 0)
                last_step(1)
            cols = slice(sub * tq, (sub + 1) * tq)
            for g in range(MLA_HEADS):
                inv_l = 1.0 / jnp.concatenate([l_ref[g]] * (V_DIM // SUBLANES), axis=0)
                o_ref[g * V_DIM:(g + 1) * V_DIM, cols] = (acc_ref[g] * inv_l).astype(o_ref.dtype)

        if tiles == 2:
            pl.when((step & 1) != 0)(lambda: finish(True))
            pl.when((step & 1) == 0)(lambda: finish(False))
        else:
            finish(False)

    for sub in range(tiles):
        query_tile(sub)


def _attention(qt, k, vt, prefix=None):
    bsz, _, tp = qt.shape
    tq = tk = min(SEQ_TILE, tp)
    tiles = 1 if tp == tq else 2
    assert tp % (tiles * tq) == 0
    vdim = MLA_HEADS * V_DIM
    stat = pltpu.VMEM((MLA_HEADS, SUBLANES, tq), F32)
    has_prefix = prefix is not None
    prefix_specs = ([_const_spec((N_META, QK_WIDTH)), _const_spec((vdim, LANES))]
                    if has_prefix else [])
    return pl.pallas_call(
        functools.partial(_attn_kernel, has_prefix, max(tp // (tiles * tq) - 1, 1)),
        grid=(bsz, tp // (tiles * tq)),
        in_specs=[
            pl.BlockSpec((None, QT_ROWS, tp), lambda b, i: (b, 0, 0)),
            pl.BlockSpec((None, tp, QK_WIDTH), lambda b, i: (b, 0, 0)),
            pl.BlockSpec((None, vdim, tp), lambda b, i: (b, 0, 0)),
        ] + prefix_specs,
        out_specs=pl.BlockSpec((None, vdim, tiles * tq), lambda b, i: (b, 0, i)),
        out_shape=jax.ShapeDtypeStruct((bsz, vdim, tp), BF16),
        scratch_shapes=[stat, stat, pltpu.VMEM((MLA_HEADS, V_DIM, tq), F32),
                        pltpu.VMEM((MLA_HEADS, tk, tq), F32),
                        pltpu.VMEM((MLA_HEADS, tk, tq), F32), stat, stat]
        + ([pltpu.VMEM((MLA_HEADS, N_META, tq), F32)] if has_prefix else []),
        compiler_params=pltpu.CompilerParams(
            dimension_semantics=("arbitrary", "arbitrary"), vmem_limit_bytes=VMEM_LIMIT),
        name="attention",
    )(qt, k, vt, *(prefix or ()))


def _rope_tables(tp):
    pos = jnp.arange(tp, dtype=F32)
    inv_freq = 1.0 / (ROPE_THETA ** (jnp.arange(0, QK_ROPE, 2, dtype=F32) / QK_ROPE))
    ang = pos[:, None] * inv_freq[None, :]
    cos, sin = jnp.cos(ang), jnp.sin(ang)
    half = QK_ROPE // 2
    zeros = lambda n: jnp.zeros((tp, n), F32)
    pad = HEAD_PAD - QK_NOPE - QK_ROPE
    c = jnp.concatenate([jnp.ones((tp, QK_NOPE), F32), cos, cos, zeros(pad)], axis=1)
    sa = jnp.concatenate([zeros(QK_NOPE + half), sin, zeros(pad)], axis=1)
    sb = jnp.concatenate([zeros(QK_NOPE), -sin, zeros(half + pad)], axis=1)
    return c, sa, sb, jnp.tile(cos, (1, MLA_HEADS)), jnp.tile(sin, (1, MLA_HEADS))


def _block_diag(w):
    depth = w.shape[0]
    per = LRU_HEADS // 2
    w5 = w.reshape(depth, 2, per, LRU_HEAD_DIM, LRU_HEAD_DIM)
    bd = jnp.einsum("dnhij,hg->dnhigj", w5, jnp.eye(per, dtype=w.dtype))
    return bd.reshape(depth, 2, per * LRU_HEAD_DIM, per * LRU_HEAD_DIM)


def _row(g):
    return g.reshape(g.shape[0], 1, g.shape[1])


@jax.jit
def kernel(x, meta_tokens, ffn1_pre_g, ffn1_w_gate, ffn1_w_up, ffn1_w_down, ffn1_post_g, mix_pre_g, w_in, lru_conv_w, lru_conv_b, lru_w_a, lru_b_a, lru_w_x, lru_b_x, lru_lambda, mla_q_norm_g, mla_w_uq, mla_kv_norm_g, mla_w_ukv, lru_out_g, mla_out_g, w_out, mix_post_g, ffn2_pre_g, ffn2_w_gate, ffn2_w_up, ffn2_w_down, ffn2_post_g):
    bsz, seq, _ = x.shape
    depth = w_in.shape[0]
    assert seq % SEQ_TILE == 0 and seq % MIX_TILE == 0
    h = x
    hm = jnp.concatenate([meta_tokens.astype(x.dtype),
                          jnp.zeros((META_ROWS - N_META, D_MODEL), x.dtype)], axis=0)[None]
    rope_all = _rope_tables(N_META + max(seq, META_ROWS))
    rope_meta = tuple(t[:META_ROWS] for t in rope_all)
    rope_main = tuple(t[N_META:N_META + seq] for t in rope_all)
    zero_hist = jnp.zeros((SUBLANES, LRU_WIDTH), F32)
    zero_state = jnp.zeros((1, LRU_WIDTH), F32)

    split = 2 * LRU_WIDTH + Q_LORA + KV_LORA
    zcols = lambda n: jnp.zeros((depth, D_MODEL, n), w_in.dtype)
    w_in_p = jnp.concatenate(
        [w_in[..., :split], zcols(QK_NOPE), w_in[..., split:],
         zcols(HEAD_PAD - QK_NOPE - QK_ROPE)], axis=-1).astype(BF16)
    w_uq4 = mla_w_uq.reshape(depth, Q_LORA, MLA_HEADS, QK_NOPE + QK_ROPE)
    half = QK_ROPE // 2
    w_uq_p = jnp.concatenate(
        [w_uq4[..., :QK_NOPE].reshape(depth, Q_LORA, Q_NOPE_ALL),
         w_uq4[..., QK_NOPE:QK_NOPE + half].reshape(depth, Q_LORA, Q_HALF_ALL),
         w_uq4[..., QK_NOPE + half:].reshape(depth, Q_LORA, Q_HALF_ALL)], axis=-1)
    w_ukv4 = mla_w_ukv.reshape(depth, KV_LORA, MLA_HEADS, QK_NOPE + V_DIM)
    w_uk_p = jnp.pad(w_ukv4[..., :QK_NOPE], ((0, 0), (0, 0), (0, 0), (0, HEAD_PAD - QK_NOPE)))
    mixp = {
        "mix_pre_g": _row(mix_pre_g),
        "w_in": w_in_p,
        "conv_w": lru_conv_w,
        "conv_b": _row(lru_conv_b),
        "wax": jnp.concatenate([_block_diag(lru_w_a), _block_diag(lru_w_x)], axis=-1).astype(BF16),
        "b_a": _row(lru_b_a),
        "b_x": _row(lru_b_x),
        "lam": _row(lru_lambda),
        "q_g": _row(mla_q_norm_g),
        "w_uq": w_uq_p.astype(BF16),
        "kv_g": _row(mla_kv_norm_g),
        "w_uk": w_uk_p.reshape(depth, KV_LORA, QK_WIDTH).astype(BF16),
        "w_uv": w_ukv4[..., QK_NOPE:].reshape(depth, KV_LORA, MLA_HEADS * V_DIM).astype(BF16),
        "lru_out_g": _row(lru_out_g),
        "mla_out_g": mla_out_g.reshape(depth, MLA_HEADS * V_DIM, 1),
        "w_out": w_out.astype(BF16),
        "mix_post_g": _row(mix_post_g),
    }
    ffn1 = (_row(ffn1_pre_g), ffn1_w_gate.astype(BF16), ffn1_w_up.astype(BF16),
            ffn1_w_down.astype(BF16), _row(ffn1_post_g))
    ffn2 = (_row(ffn2_pre_g), ffn2_w_gate.astype(BF16), ffn2_w_up.astype(BF16),
            ffn2_w_down.astype(BF16), _row(ffn2_post_g))

    lane = lax.broadcasted_iota(jnp.int32, (MLA_HEADS * V_DIM, META_ROWS), 1)
    for layer in range(depth):
        hm = _ffn(hm, layer, *ffn1)
        h = _ffn(h, layer, *ffn1)
        ylru_m, qt_m, k_m, vt_m, xr_m, hs_m = _mix_in(
            hm, layer, mixp, rope_meta, zero_hist, zero_state, emit_state=True)
        ylru, qt, k, vt = _mix_in(h, layer, mixp, rope_main,
                                  xr_m[0, N_META - SUBLANES:N_META], hs_m[0, N_META - 1:N_META])
        prefix = (k_m[0, :N_META], jnp.where(lane < N_META, vt_m[0], 0).astype(BF16))
        h = _ffn(h, layer, *ffn2, mixer=(ylru, _attention(qt, k, vt, prefix), mixp))
        if layer + 1 < depth:
            hm = _ffn(hm, layer, *ffn2, mixer=(ylru_m, _attention(qt_m, k_m, vt_m), mixp))
    return h
```

```python
import functools

import jax
import jax.numpy as jnp
from jax import lax
from jax.experimental import pallas as pl
from jax.experimental.pallas import tpu as pltpu

D_MODEL = 1024
N_META = 16
LRU_WIDTH = 512
LRU_HEADS = 8
LRU_HEAD_DIM = 64
CONV_WIDTH = 4
LRU_C = 8.0
MLA_HEADS = 8
QK_NOPE = 64
QK_ROPE = 32
V_DIM = 64
Q_LORA = 384
KV_LORA = 256
D_FF = 2816
ROPE_THETA = 10000.0
EPS = 1e-6

LANES = 128
SUBLANES = 8
HEAD_PAD = LANES
QK_WIDTH = MLA_HEADS * HEAD_PAD
Q_NOPE_ALL = MLA_HEADS * QK_NOPE
Q_HALF_ALL = MLA_HEADS * QK_ROPE // 2
QT_ROWS = Q_NOPE_ALL + 2 * Q_HALF_ALL
IN_COLS_PAD = 2 * LRU_WIDTH + Q_LORA + KV_LORA + HEAD_PAD
SEQ_TILE = 256
MIX_TILE = 1024
MIX_SUB = 256
META_ROWS = LANES
FFN_ROWS = 512
FFN_SUBTILES = 2
FF_CHUNK = 256
NEG_BIG = -1e30
TINY = 1e-30
LOG2_E = 1.4426950408889634
VMEM_LIMIT = 52 * 1024 * 1024

F32 = jnp.float32
BF16 = jnp.bfloat16


def _rms(x, g):
    return x * lax.rsqrt(jnp.mean(x * x, axis=-1, keepdims=True) + EPS) * g


def _const_spec(shape, layer=None):
    if layer is None:
        idx = lambda *_: (0,) * len(shape)
        return pl.BlockSpec(shape, idx, pipeline_mode=pl.Buffered(1))
    idx = lambda *_: (layer,) + (0,) * len(shape)
    return pl.BlockSpec((None,) + shape, idx, pipeline_mode=pl.Buffered(1))


def _mix_out_rows(h, ylru, ot, gmla, wout_ref, gpost):
    ms = jnp.mean(ot * ot, axis=0, keepdims=True)
    on = (ot * lax.rsqrt(ms + EPS) * gmla).T.astype(BF16)
    y = jnp.dot(ylru, wout_ref[:LRU_WIDTH, :], preferred_element_type=F32)
    y = y + jnp.dot(on, wout_ref[LRU_WIDTH:, :], preferred_element_type=F32)
    return h + _rms(y, gpost)


def _ffn_kernel(after_mixer, h_ref, *refs):
    if after_mixer:
        ylru_ref, ot_ref, gmla_ref, wout_ref, gmix_ref = refs[:5]
        refs = refs[5:]
    gpre_ref, wg_ref, wu_ref, wd_ref, gpost_ref, o_ref, a_ref = refs
    rows = h_ref.shape[0]
    sub = rows // FFN_SUBTILES if after_mixer and rows % (FFN_SUBTILES * LANES) == 0 else rows
    parts = [slice(i * sub, (i + 1) * sub) for i in range(rows // sub)]
    xs = []
    for r in parts:
        x = h_ref[r, :]
        if after_mixer:
            x = _mix_out_rows(x, ylru_ref[r, :], ot_ref[:, r], gmla_ref[...], wout_ref,
                              gmix_ref[...])
        xs.append(x)
    for r, x in zip(parts, xs):
        u = _rms(x, gpre_ref[...]).astype(BF16)
        for c in range(D_FF // FF_CHUNK):
            sl = slice(c * FF_CHUNK, (c + 1) * FF_CHUNK)
            gate = jnp.dot(u, wg_ref[:, sl], preferred_element_type=F32)
            up = jnp.dot(u, wu_ref[:, sl], preferred_element_type=F32)
            a_ref[r, sl] = (gate * jax.nn.sigmoid(gate) * up).astype(BF16)
    for r, x in zip(parts, xs):
        f = jnp.dot(a_ref[r, :], wd_ref[...], preferred_element_type=F32)
        o_ref[r, :] = x + 0.5 * _rms(f, gpost_ref[...])


def _ffn(h, layer, gpre, wg, wu, wd, gpost, mixer=None):
    bsz, tp, _ = h.shape
    tile = min(FFN_ROWS if mixer is not None else 2 * FFN_ROWS, tp)
    assert tp % tile == 0
    row_spec = pl.BlockSpec((None, tile, D_MODEL), lambda b, t: (b, t, 0))
    mixer_specs, mixer_args = [], []
    if mixer is not None:
        ylru, ot, p = mixer
        vdim = MLA_HEADS * V_DIM
        mixer_specs = [
            pl.BlockSpec((None, tile, LRU_WIDTH), lambda b, t: (b, t, 0)),
            pl.BlockSpec((None, vdim, tile), lambda b, t: (b, 0, t)),
            _const_spec((vdim, 1), layer),
            _const_spec((D_MODEL, D_MODEL), layer),
            _const_spec((1, D_MODEL), layer),
        ]
        mixer_args = [ylru, ot, p["mla_out_g"], p["w_out"], p["mix_post_g"]]
    return pl.pallas_call(
        functools.partial(_ffn_kernel, mixer is not None),
        grid=(bsz, tp // tile),
        in_specs=[row_spec] + mixer_specs + [
            _const_spec((1, D_MODEL), layer),
            _const_spec((D_MODEL, D_FF), layer),
            _const_spec((D_MODEL, D_FF), layer),
            _const_spec((D_FF, D_MODEL), layer),
            _const_spec((1, D_MODEL), layer),
        ],
        out_specs=row_spec,
        out_shape=jax.ShapeDtypeStruct(h.shape, F32),
        scratch_shapes=[pltpu.VMEM((tile, D_FF), BF16)],
        compiler_params=pltpu.CompilerParams(
            dimension_semantics=("arbitrary", "arbitrary"), vmem_limit_bytes=VMEM_LIMIT),
        name="ffn_mix" if mixer is not None else "ffn",
    )(h, *mixer_args, gpre, wg, wu, wd, gpost)


def _rope(x, c, sa, sb):
    n = x.shape[-1]
    half = QK_ROPE // 2
    return x * c + pltpu.roll(x, half, 1) * sa + pltpu.roll(x, n - half, 1) * sb


def _mix_in_kernel(emit_state, h_ref, g_ref, win_ref, cw_ref, cb_ref, wax_ref, ba_ref, bx_ref,
                   lam_ref, gq_ref, wuq_ref, gkv_ref, wuk_ref, wuv_ref, glru_ref,
                   rc_ref, rsa_ref, rsb_ref, cos8_ref, sin8_ref, xprev0_ref, hcar0_ref,
                   ylru_ref, qt_ref, k_ref, vt_ref, *rest):
    if emit_state:
        xr_out_ref, hs_out_ref, xprev_ref, hcar_ref = rest
    else:
        xprev_ref, hcar_ref = rest
    tt = h_ref.shape[0]

    @pl.when(pl.program_id(1) == 0)
    def _():
        xprev_ref[...] = xprev0_ref[...]
        hcar_ref[...] = hcar0_ref[...]

    sub = min(MIX_SUB, tt)
    half = LRU_WIDTH // 2
    nl = -lam_ref[...]
    softplus = jnp.maximum(nl, 0.0) + jnp.log1p(jnp.exp(-jnp.abs(nl)))
    decay = (-LRU_C * LOG2_E) * softplus
    gq_scaled = gq_ref[...] * (LOG2_E * (QK_NOPE + QK_ROPE) ** -0.5)
    tile_heads = lambda t: jnp.concatenate([t] * MLA_HEADS, axis=1)

    def project(i):
        u = _rms(h_ref[i * sub:(i + 1) * sub, :], g_ref[...]).astype(BF16)
        return jnp.dot(u, win_ref[...], preferred_element_type=F32)

    xprev = xprev_ref[...]
    carry = hcar_ref[...]
    z_next = project(0)
    for i in range(tt // sub):
        z = z_next
        if (i + 1) * sub < tt:
            z_next = project(i + 1)
        rows = slice(i * sub, (i + 1) * sub)
        xr = z[:, 0:LRU_WIDTH]
        gr = z[:, LRU_WIDTH:2 * LRU_WIDTH]
        cq = z[:, 2 * LRU_WIDTH:2 * LRU_WIDTH + Q_LORA]
        ckv = z[:, 2 * LRU_WIDTH + Q_LORA:2 * LRU_WIDTH + Q_LORA + KV_LORA]
        krt = z[:, 2 * LRU_WIDTH + Q_LORA + KV_LORA:]

        xe = jnp.concatenate([xprev, xr], axis=0)
        xc = xr * cw_ref[CONV_WIDTH - 1:CONV_WIDTH, :] + cb_ref[...]
        for j in range(1, CONV_WIDTH):
            w_j = cw_ref[CONV_WIDTH - 1 - j:CONV_WIDTH - j, :]
            xc = xc + pltpu.roll(xe, j, 0)[SUBLANES:] * w_j
        xprev = xr[sub - SUBLANES:]

        xcb = xc.astype(BF16)
        ri0 = jnp.dot(xcb[:, :half], wax_ref[0], preferred_element_type=F32)
        ri1 = jnp.dot(xcb[:, half:], wax_ref[1], preferred_element_type=F32)
        r = jax.nn.sigmoid(jnp.concatenate([ri0[:, :half], ri1[:, :half]], axis=1) + ba_ref[...])
        g_in = jax.nn.sigmoid(
            jnp.concatenate([ri0[:, half:], ri1[:, half:]], axis=1) + bx_ref[...])
        a = jnp.exp2(r * decay)
        gap = 1.0 - a * a
        b = (gap * lax.rsqrt(jnp.maximum(gap, TINY))) * (g_in * xc)

        grouped = (sub // SUBLANES, SUBLANES, LRU_WIDTH)
        a = a.reshape(grouped)
        b = b.reshape(grouped)
        row = lax.broadcasted_iota(jnp.int32, grouped, 1)
        s = 1
        while s < SUBLANES:
            valid = row >= s
            b = a * jnp.where(valid, pltpu.roll(b, s, 1), 0.0) + b
            a = a * jnp.where(valid, pltpu.roll(a, s, 1), 1.0)
            s *= 2
        a = a.reshape(sub, LRU_WIDTH)
        b = b.reshape(sub, LRU_WIDTH)
        groups = []
        for gidx in range(sub // SUBLANES):
            sl = slice(gidx * SUBLANES, (gidx + 1) * SUBLANES)
            hg = b[sl] + a[sl] * carry
            carry = hg[SUBLANES - 1:SUBLANES]
            groups.append(hg)
        hs = jnp.concatenate(groups, axis=0)
        if emit_state:
            xr_out_ref[rows, :] = xr
            hs_out_ref[rows, :] = hs

        cg = 0.7978845608028654
        half_gr = 0.5 * gr
        gelu = half_gr + half_gr * jnp.tanh(gr * (cg + (cg * 0.044715) * (gr * gr)))
        ylru_ref[rows, :] = _rms(hs * gelu, glru_ref[...]).astype(BF16)

        rc = rc_ref[rows, :]
        rsa = rsa_ref[rows, :]
        rsb = rsb_ref[rows, :]
        cqn = _rms(cq, gq_scaled).astype(BF16)
        q = jnp.dot(cqn, wuq_ref[...], preferred_element_type=F32)
        q1 = q[:, Q_NOPE_ALL:Q_NOPE_ALL + Q_HALF_ALL]
        q2 = q[:, Q_NOPE_ALL + Q_HALF_ALL:]
        cos8 = cos8_ref[rows, :]
        sin8 = sin8_ref[rows, :]
        q = jnp.concatenate([q[:, :Q_NOPE_ALL], q1 * cos8 - q2 * sin8, q2 * cos8 + q1 * sin8],
                            axis=1)
        qt_ref[:, rows] = q.T.astype(BF16)

        ckvn = _rms(ckv, gkv_ref[...]).astype(BF16)
        kn = jnp.dot(ckvn, wuk_ref[...], preferred_element_type=F32)
        kr = _rope(krt, rc, rsa, rsb)
        k_ref[rows, :] = (kn + tile_heads(kr)).astype(BF16)
        v = jnp.dot(ckvn, wuv_ref[...], preferred_element_type=F32)
        vt_ref[:, rows] = v.T.astype(BF16)
    xprev_ref[...] = xprev
    hcar_ref[...] = carry


def _mix_in(h, layer, p, rope, xprev0, hcar0, emit_state=False):
    bsz, tp, _ = h.shape
    tt = min(MIX_TILE, tp)
    nt = tp // tt
    tbl_spec = pl.BlockSpec((tt, HEAD_PAD), lambda b, t: (t, 0))
    seq_spec = pl.BlockSpec((None, tt, LRU_WIDTH), lambda b, t: (b, t, 0))
    state_specs = [seq_spec, seq_spec] if emit_state else []
    state_shapes = [jax.ShapeDtypeStruct((bsz, tp, LRU_WIDTH), F32)] * 2 if emit_state else []
    return pl.pallas_call(
        functools.partial(_mix_in_kernel, emit_state),
        grid=(bsz, nt),
        in_specs=[
            pl.BlockSpec((None, tt, D_MODEL), lambda b, t: (b, t, 0)),
            _const_spec((1, D_MODEL), layer),
            _const_spec((D_MODEL, IN_COLS_PAD), layer),
            _const_spec((CONV_WIDTH, LRU_WIDTH), layer),
            _const_spec((1, LRU_WIDTH), layer),
            _const_spec((2, LRU_WIDTH // 2, LRU_WIDTH), layer),
            _const_spec((1, LRU_WIDTH), layer),
            _const_spec((1, LRU_WIDTH), layer),
            _const_spec((1, LRU_WIDTH), layer),
            _const_spec((1, Q_LORA), layer),
            _const_spec((Q_LORA, QT_ROWS), layer),
            _const_spec((1, KV_LORA), layer),
            _const_spec((KV_LORA, QK_WIDTH), layer),
            _const_spec((KV_LORA, MLA_HEADS * V_DIM), layer),
            _const_spec((1, LRU_WIDTH), layer),
            tbl_spec, tbl_spec, tbl_spec, tbl_spec, tbl_spec,
            _const_spec((SUBLANES, LRU_WIDTH)),
            _const_spec((1, LRU_WIDTH)),
        ],
        out_specs=[
            seq_spec,
            pl.BlockSpec((None, QT_ROWS, tt), lambda b, t: (b, 0, t)),
            pl.BlockSpec((None, tt, QK_WIDTH), lambda b, t: (b, t, 0)),
            pl.BlockSpec((None, MLA_HEADS * V_DIM, tt), lambda b, t: (b, 0, t)),
        ] + state_specs,
        out_shape=[
            jax.ShapeDtypeStruct((bsz, tp, LRU_WIDTH), BF16),
            jax.ShapeDtypeStruct((bsz, QT_ROWS, tp), BF16),
            jax.ShapeDtypeStruct((bsz, tp, QK_WIDTH), BF16),
            jax.ShapeDtypeStruct((bsz, MLA_HEADS * V_DIM, tp), BF16),
        ] + state_shapes,
        scratch_shapes=[pltpu.VMEM((SUBLANES, LRU_WIDTH), F32),
                        pltpu.VMEM((1, LRU_WIDTH), F32)],
        compiler_params=pltpu.CompilerParams(
            dimension_semantics=("arbitrary", "arbitrary"), vmem_limit_bytes=VMEM_LIMIT),
        name="mix_in",
    )(h, p["mix_pre_g"], p["w_in"], p["conv_w"], p["conv_b"], p["wax"], p["b_a"], p["b_x"],
      p["lam"], p["q_g"], p["w_uq"], p["kv_g"], p["w_uk"], p["w_uv"], p["lru_out_g"], *rope,
      xprev0, hcar0)


def _group_max(s):
    m = s[0:SUBLANES]
    for r in range(1, s.shape[0] // SUBLANES):
        m = jnp.maximum(m, s[r * SUBLANES:(r + 1) * SUBLANES])
    return m


def _sublane_allmax(m):
    for shift in (1, 2, 4):
        m = jnp.maximum(m, pltpu.roll(m, shift, 0))
    return m


def _attn_kernel(has_prefix, max_pairs, qt_ref, k_ref, vt_ref, *rest):
    if has_prefix:
        kpre_ref, vtpre_ref = rest[:2]
        rest = rest[2:]
        spre_ref = rest[-1]
        rest = rest[:-1]
    o_ref, m_ref, l_ref, acc_ref, s0_ref, s1_ref, bm0_ref, bm1_ref = rest
    step = pl.program_id(1)
    tk, tq = s0_ref.shape[1:]
    tiles = o_ref.shape[1] // tq
    assert tiles in (1, 2)
    last_tile = pl.num_programs(1) * tiles - 1
    s_refs = (s0_ref, s1_ref)
    bm_refs = (bm0_ref, bm1_ref)
    ones_rows = 2 * SUBLANES

    def with_ones(vb):
        return jnp.concatenate([vb, jnp.ones((ones_rows, vb.shape[1]), BF16)], axis=0)

    def q_head(g, tile):
        half = QK_ROPE // 2
        r1 = Q_NOPE_ALL + g * half
        r2 = r1 + Q_HALF_ALL
        cols = pl.ds(pl.multiple_of(tile * tq, tq), tq)
        return jnp.concatenate(
            [qt_ref[g * QK_NOPE:(g + 1) * QK_NOPE, cols], qt_ref[r1:r1 + half, cols],
             qt_ref[r2:r2 + half, cols], jnp.zeros((HEAD_PAD - QK_NOPE - QK_ROPE, tq), BF16)],
            axis=0)

    def scores(j, g, slot, tile):
        off = pl.multiple_of(j * tk, tk)
        s = jnp.dot(k_ref[pl.ds(off, tk), g * HEAD_PAD:(g + 1) * HEAD_PAD], q_head(g, tile),
                    preferred_element_type=F32)
        s_refs[slot][g] = s
        bm_refs[slot][g] = _group_max(s)

    def first_scores(g, tile):
        scores(0, g, 0, tile)
        if has_prefix:
            spre_ref[g] = jnp.dot(kpre_ref[:, g * HEAD_PAD:(g + 1) * HEAD_PAD], q_head(g, tile),
                                  preferred_element_type=F32)

    def softmax_pv(j, g, slot, masked):
        s = s_refs[slot][g]
        if masked:
            key = lax.broadcasted_iota(jnp.int32, s.shape, 0)
            qry = lax.broadcasted_iota(jnp.int32, s.shape, 1)
            s = jnp.where(key <= qry, s, NEG_BIG)
            bm = _group_max(s)
            if has_prefix:
                s_pre = spre_ref[g]
                bm = jnp.maximum(bm, _group_max(s_pre))
        else:
            bm = bm_refs[slot][g]
        m_old = m_ref[g]
        m_new = jnp.maximum(m_old, _sublane_allmax(bm))
        alpha = jnp.exp2(m_old - m_new)
        p = jnp.exp2(s - jnp.concatenate([m_new] * (tk // SUBLANES), axis=0))
        m_ref[g] = m_new
        off = pl.multiple_of(j * tk, tk)
        vb = with_ones(vt_ref[g * V_DIM:(g + 1) * V_DIM, pl.ds(off, tk)])
        pv = jnp.dot(vb, p.astype(BF16), preferred_element_type=F32)
        if masked and has_prefix:
            p_pre = jnp.exp2(s_pre - jnp.concatenate([m_new] * (N_META // SUBLANES), axis=0))
            p_pre = jnp.concatenate(
                [p_pre.astype(BF16), jnp.zeros((LANES - N_META, tq), BF16)], axis=0)
            pv = pv + jnp.dot(with_ones(vtpre_ref[g * V_DIM:(g + 1) * V_DIM, :]), p_pre,
                              preferred_element_type=F32)
        acc_ref[g] = (jnp.concatenate([alpha] * (V_DIM // SUBLANES), axis=0) * acc_ref[g]
                      + pv[:V_DIM])
        l_ref[g] = alpha * l_ref[g] + pv[V_DIM:V_DIM + SUBLANES]

    def query_tile(sub):
        qi = step * tiles + sub
        m_ref[...] = jnp.full(m_ref.shape, NEG_BIG, F32)
        l_ref[...] = jnp.zeros(l_ref.shape, F32)
        acc_ref[...] = jnp.zeros(acc_ref.shape, F32)

        def block_step(j_next, slot_next, j, slot):
            for g in range(MLA_HEADS):
                scores(j_next, g, slot_next, qi)
                softmax_pv(j, g, slot, False)

        def last_step(slot):
            nxt = jnp.minimum(qi + 1, last_tile)
            for g in range(MLA_HEADS):
                if slot == 1:
                    scores(0, g, 0, nxt)
                softmax_pv(qi, g, slot, True)
                if slot == 0:
                    scores(0, g, 0, nxt)
                if has_prefix:
                    spre_ref[g] = jnp.dot(kpre_ref[:, g * HEAD_PAD:(g + 1) * HEAD_PAD],
                                          q_head(g, nxt), preferred_element_type=F32)

        if sub == 0:
            @pl.when(step == 0)
            def _():
                for g in range(MLA_HEADS):
                    first_scores(g, 0)

        def pair(j):
            block_step(j + 1, 1, j, 0)
            block_step(j + 2, 0, j + 1, 1)

        bit = 1
        while bit * 2 <= max_pairs:
            bit *= 2
        while tiles == 2 and bit >= 2:
            @pl.when((step & bit) != 0)
            def _(bit=bit):
                first = step & ~(2 * bit - 1)
                for q in range(bit):
                    pair(2 * (first + q))
            bit //= 2

        def finish(odd_pair):
            if odd_pair:
                pair(2 * (step - 1))
            if sub == 0:
                last_step(0)
            else:
                block_step(qi, 1, qi - 1, 0)
                last_step(1)
            cols = slice(sub * tq, (sub + 1) * tq)
            for g in range(MLA_HEADS):
                inv_l = 1.0 / jnp.concatenate([l_ref[g]] * (V_DIM // SUBLANES), axis=0)
                o_ref[g * V_DIM:(g + 1) * V_DIM, cols] = acc_ref[g] * inv_l

        if tiles == 2:
            pl.when((step & 1) != 0)(lambda: finish(True))
            pl.when((step & 1) == 0)(lambda: finish(False))
        else:
            finish(False)

    for sub in range(tiles):
        query_tile(sub)


def _attention(qt, k, vt, prefix=None):
    bsz, _, tp = qt.shape
    tq = tk = min(SEQ_TILE, tp)
    tiles = 1 if tp == tq else 2
    assert tp % (tiles * tq) == 0
    vdim = MLA_HEADS * V_DIM
    stat = pltpu.VMEM((MLA_HEADS, SUBLANES, tq), F32)
    has_prefix = prefix is not None
    prefix_specs = ([_const_spec((N_META, QK_WIDTH)), _const_spec((vdim, LANES))]
                    if has_prefix else [])
    return pl.pallas_call(
        functools.partial(_attn_kernel, has_prefix, max(tp // (tiles * tq) - 1, 1)),
        grid=(bsz, tp // (tiles * tq)),
        in_specs=[
            pl.BlockSpec((None, QT_ROWS, tp), lambda b, i: (b, 0, 0)),
            pl.BlockSpec((None, tp, QK_WIDTH), lambda b, i: (b, 0, 0)),
            pl.BlockSpec((None, vdim, tp), lambda b, i: (b, 0, 0)),
        ] + prefix_specs,
        out_specs=pl.BlockSpec((None, vdim, tiles * tq), lambda b, i: (b, 0, i)),
        out_shape=jax.ShapeDtypeStruct((bsz, vdim, tp), F32),
        scratch_shapes=[stat, stat, pltpu.VMEM((MLA_HEADS, V_DIM, tq), F32),
                        pltpu.VMEM((MLA_HEADS, tk, tq), F32),
                        pltpu.VMEM((MLA_HEADS, tk, tq), F32), stat, stat]
        + ([pltpu.VMEM((MLA_HEADS, N_META, tq), F32)] if has_prefix else []),
        compiler_params=pltpu.CompilerParams(
            dimension_semantics=("arbitrary", "arbitrary"), vmem_limit_bytes=VMEM_LIMIT),
        name="attention",
    )(qt, k, vt, *(prefix or ()))


def _rope_tables(tp):
    pos = jnp.arange(tp, dtype=F32)
    inv_freq = 1.0 / (ROPE_THETA ** (jnp.arange(0, QK_ROPE, 2, dtype=F32) / QK_ROPE))
    ang = pos[:, None] * inv_freq[None, :]
    cos, sin = jnp.cos(ang), jnp.sin(ang)
    half = QK_ROPE // 2
    zeros = lambda n: jnp.zeros((tp, n), F32)
    pad = HEAD_PAD - QK_NOPE - QK_ROPE
    c = jnp.concatenate([jnp.ones((tp, QK_NOPE), F32), cos, cos, zeros(pad)], axis=1)
    sa = jnp.concatenate([zeros(QK_NOPE + half), sin, zeros(pad)], axis=1)
    sb = jnp.concatenate([zeros(QK_NOPE), -sin, zeros(half + pad)], axis=1)
    return c, sa, sb, jnp.tile(cos, (1, MLA_HEADS)), jnp.tile(sin, (1, MLA_HEADS))


def _block_diag(w):
    depth = w.shape[0]
    per = LRU_HEADS // 2
    w5 = w.reshape(depth, 2, per, LRU_HEAD_DIM, LRU_HEAD_DIM)
    bd = jnp.einsum("dnhij,hg->dnhigj", w5, jnp.eye(per, dtype=w.dtype))
    return bd.reshape(depth, 2, per * LRU_HEAD_DIM, per * LRU_HEAD_DIM)


def _row(g):
    return g.reshape(g.shape[0], 1, g.shape[1])


@jax.jit
def kernel(x, meta_tokens, ffn1_pre_g, ffn1_w_gate, ffn1_w_up, ffn1_w_down, ffn1_post_g, mix_pre_g, w_in, lru_conv_w, lru_conv_b, lru_w_a, lru_b_a, lru_w_x, lru_b_x, lru_lambda, mla_q_norm_g, mla_w_uq, mla_kv_norm_g, mla_w_ukv, lru_out_g, mla_out_g, w_out, mix_post_g, ffn2_pre_g, ffn2_w_gate, ffn2_w_up, ffn2_w_down, ffn2_post_g):
    bsz, seq, _ = x.shape
    depth = w_in.shape[0]
    assert seq % SEQ_TILE == 0 and seq % MIX_TILE == 0
    h = x
    hm = jnp.concatenate([meta_tokens.astype(x.dtype),
                          jnp.zeros((META_ROWS - N_META, D_MODEL), x.dtype)], axis=0)[None]
    rope_all = _rope_tables(N_META + max(seq, META_ROWS))
    rope_meta = tuple(t[:META_ROWS] for t in rope_all)
    rope_main = tuple(t[N_META:N_META + seq] for t in rope_all)
    zero_hist = jnp.zeros((SUBLANES, LRU_WIDTH), F32)
    zero_state = jnp.zeros((1, LRU_WIDTH), F32)

    split = 2 * LRU_WIDTH + Q_LORA + KV_LORA
    zcols = lambda n: jnp.zeros((depth, D_MODEL, n), w_in.dtype)
    w_in_p = jnp.concatenate(
        [w_in[..., :split], zcols(QK_NOPE), w_in[..., split:],
         zcols(HEAD_PAD - QK_NOPE - QK_ROPE)], axis=-1).astype(BF16)
    w_uq4 = mla_w_uq.reshape(depth, Q_LORA, MLA_HEADS, QK_NOPE + QK_ROPE)
    half = QK_ROPE // 2
    w_uq_p = jnp.concatenate(
        [w_uq4[..., :QK_NOPE].reshape(depth, Q_LORA, Q_NOPE_ALL),
         w_uq4[..., QK_NOPE:QK_NOPE + half].reshape(depth, Q_LORA, Q_HALF_ALL),
         w_uq4[..., QK_NOPE + half:].reshape(depth, Q_LORA, Q_HALF_ALL)], axis=-1)
    w_ukv4 = mla_w_ukv.reshape(depth, KV_LORA, MLA_HEADS, QK_NOPE + V_DIM)
    w_uk_p = jnp.pad(w_ukv4[..., :QK_NOPE], ((0, 0), (0, 0), (0, 0), (0, HEAD_PAD - QK_NOPE)))
    mixp = {
        "mix_pre_g": _row(mix_pre_g),
        "w_in": w_in_p,
        "conv_w": lru_conv_w,
        "conv_b": _row(lru_conv_b),
        "wax": jnp.concatenate([_block_diag(lru_w_a), _block_diag(lru_w_x)], axis=-1).astype(BF16),
        "b_a": _row(lru_b_a),
        "b_x": _row(lru_b_x),
        "lam": _row(lru_lambda),
        "q_g": _row(mla_q_norm_g),
        "w_uq": w_uq_p.astype(BF16),
        "kv_g": _row(mla_kv_norm_g),
        "w_uk": w_uk_p.reshape(depth, KV_LORA, QK_WIDTH).astype(BF16),
        "w_uv": w_ukv4[..., QK_NOPE:].reshape(depth, KV_LORA, MLA_HEADS * V_DIM).astype(BF16),
        "lru_out_g": _row(lru_out_g),
        "mla_out_g": mla_out_g.reshape(depth, MLA_HEADS * V_DIM, 1),
        "w_out": w_out.astype(BF16),
        "mix_post_g": _row(mix_post_g),
    }
    ffn1 = (_row(ffn1_pre_g), ffn1_w_gate.astype(BF16), ffn1_w_up.astype(BF16),
            ffn1_w_down.astype(BF16), _row(ffn1_post_g))
    ffn2 = (_row(ffn2_pre_g), ffn2_w_gate.astype(BF16), ffn2_w_up.astype(BF16),
            ffn2_w_down.astype(BF16), _row(ffn2_post_g))

    lane = lax.broadcasted_iota(jnp.int32, (MLA_HEADS * V_DIM, META_ROWS), 1)
    for layer in range(depth):
        hm = _ffn(hm, layer, *ffn1)
        h = _ffn(h, layer, *ffn1)
        ylru_m, qt_m, k_m, vt_m, xr_m, hs_m = _mix_in(
            hm, layer, mixp, rope_meta, zero_hist, zero_state, emit_state=True)
        ylru, qt, k, vt = _mix_in(h, layer, mixp, rope_main,
                                  xr_m[0, N_META - SUBLANES:N_META], hs_m[0, N_META - 1:N_META])
        prefix = (k_m[0, :N_META], jnp.where(lane < N_META, vt_m[0], 0).astype(BF16))
        h = _ffn(h, layer, *ffn2, mixer=(ylru, _attention(qt, k, vt, prefix), mixp))
        if layer + 1 < depth:
            hm = _ffn(hm, layer, *ffn2, mixer=(ylru_m, _attention(qt_m, k_m, vt_m), mixp))
    return h
```
